```python
import math
import jax, jax.numpy as jnp
from jax import lax
import numpy as np

D_MODEL = 1024
BATCH = 8
SEQ = 2048
DEPTH = 2

HEAD_DIM = 64
BLK = 128
A_GROUPS = ((128, 1), (512, 4), (2048, 16))
A_HEADS = 4
B_Q_HEADS = 8
B_KV_HEADS = 2
B_WINDOW = 128
C_HEADS = 4
N_BRANCH = 3
NUM_BUCKETS = 32
MAX_DISTANCE = max(w for w, _ in A_GROUPS)
D_FF = 4 * D_MODEL
CONV_WIDTH = 3
EPS = 1e-6

A_WIDTH = A_HEADS * HEAD_DIM
B_WIDTH = B_Q_HEADS * HEAD_DIM
C_WIDTH = C_HEADS * HEAD_DIM
B_GROUP = B_Q_HEADS // B_KV_HEADS
N_A_GROUP_HEADS = len(A_GROUPS) * A_HEADS
N_BIAS_HEADS = N_A_GROUP_HEADS + B_Q_HEADS
A_QKV_COLS = 3 * N_A_GROUP_HEADS * HEAD_DIM
B_Q_COLS = B_WIDTH
B_KV_COLS = 2 * B_KV_HEADS * HEAD_DIM
C_QKV_COLS = 3 * C_WIDTH
GATE_COLS = N_BRANCH * D_MODEL
OFF_B_Q = A_QKV_COLS
OFF_B_KV = OFF_B_Q + B_Q_COLS
OFF_C = OFF_B_KV + B_KV_COLS
OFF_GATE = OFF_C + C_QKV_COLS
IN_COLS = OFF_GATE + GATE_COLS
SCALE = HEAD_DIM ** -0.5

kernel_name = 'hybrid_dilated_swa_stickbreak_convffn'


def rms_norm(x, g):
    xf = x.astype(jnp.float32)
    y = xf * lax.rsqrt(jnp.mean(xf * xf, axis=-1, keepdims=True) + EPS)
    return (y * g.astype(jnp.float32)).astype(x.dtype)


def t5_bucket(dist):
    max_exact = NUM_BUCKETS // 2
    nf = jnp.maximum(dist, 1).astype(jnp.float32)
    large = max_exact + (jnp.log(nf / max_exact) / math.log(MAX_DISTANCE / max_exact)
                         * (NUM_BUCKETS - max_exact)).astype(jnp.int32)
    large = jnp.minimum(large, NUM_BUCKETS - 1)
    return jnp.where(dist < max_exact, dist, large)


def band_layout(n_blocks, max_dist):
    a = jnp.arange(BLK)[:, None]
    b = jnp.arange(2 * BLK)[None, :]
    dist = a + BLK - b
    in_band = (dist >= 0) & (dist <= max_dist)
    key_exists = (jnp.arange(n_blocks)[:, None] > 0) | (jnp.arange(2 * BLK)[None, :] >= BLK)
    mask = in_band[None] & key_exists[:, None, :]
    return jnp.maximum(dist, 0), mask


def band_bias(table, dist):
    return jnp.transpose(table[t5_bucket(dist)], (2, 0, 1)).astype(jnp.float32)


def banded_attention(q, k, v, bias, mask):
    n, hkv, g, L, dh = q.shape
    nb = L // BLK
    qb = q.reshape(n, hkv, g, nb, BLK, dh)

    def two_blocks(t):
        tb = t.reshape(n, hkv, nb, BLK, dh)
        prev = jnp.pad(tb, ((0, 0), (0, 0), (1, 0), (0, 0), (0, 0)))[:, :, :nb]
        return jnp.concatenate([prev, tb], axis=3)

    kw, vw = two_blocks(k), two_blocks(v)
    logits = jnp.einsum('nkgbqd,nkbsd->nkgbqs', qb, kw, preferred_element_type=jnp.float32) * SCALE
    logits = jnp.where(mask, logits + bias[:, :, None], -jnp.inf)
    m = jnp.max(logits, axis=-1)
    p = jnp.exp(logits - m[..., None])
    l = jnp.sum(p, axis=-1)
    num = jnp.einsum('nkgbqs,nkbsd->nkgbqd', p, vw.astype(jnp.float32))
    return num.reshape(n, hkv, g, L, dh), m.reshape(n, hkv, g, L), l.reshape(n, hkv, g, L)


def to_sub(t, d, lp):
    b, s, h, dh = t.shape
    L = s // d
    t = t.reshape(b, L, d, h, dh).transpose(0, 2, 3, 1, 4).reshape(b * d, h, L, dh)
    return jnp.pad(t, ((0, 0), (0, 0), (0, lp - L), (0, 0)))


def dilated_attention(q, k, v, table_a):
    b, s = q.shape[:2]
    nums, ms, ls = [], [], []
    for gi, (window, d) in enumerate(A_GROUPS):
        L = s // d
        lp = -(-L // BLK) * BLK
        dist, mask = band_layout(lp // BLK, window // d)
        bias = band_bias(table_a[:, gi], dist * d)[:, None]
        num, m, l = banded_attention(to_sub(q[:, :, gi], d, lp)[:, :, None],
                                     to_sub(k[:, :, gi], d, lp), to_sub(v[:, :, gi], d, lp), bias, mask)
        num = num[:, :, 0, :L].reshape(b, d, A_HEADS, L, HEAD_DIM).transpose(0, 3, 1, 2, 4)
        nums.append(num.reshape(b, s, A_HEADS, HEAD_DIM))
        ms.append(m[:, :, 0, :L].reshape(b, d, A_HEADS, L).transpose(0, 3, 1, 2).reshape(b, s, A_HEADS))
        ls.append(l[:, :, 0, :L].reshape(b, d, A_HEADS, L).transpose(0, 3, 1, 2).reshape(b, s, A_HEADS))
    nums = jnp.stack(nums, axis=2)
    ms = jnp.stack(ms, axis=2)
    ls = jnp.stack(ls, axis=2)
    c = jnp.exp(ms - jnp.max(ms, axis=2, keepdims=True))
    out = jnp.sum(c[..., None] * nums, axis=2) / jnp.sum(c * ls, axis=2)[..., None]
    return out.reshape(b, s, A_WIDTH).astype(q.dtype)


def sliding_window_gqa(q, k, v, sinks, table_b):
    b, s = q.shape[:2]
    dist, mask = band_layout(s // BLK, B_WINDOW - 1)
    bias = band_bias(table_b, dist).reshape(B_KV_HEADS, B_GROUP, BLK, 2 * BLK)
    qh = q.reshape(b, s, B_KV_HEADS, B_GROUP, HEAD_DIM).transpose(0, 2, 3, 1, 4)
    num, m, l = banded_attention(qh, k.transpose(0, 2, 1, 3), v.transpose(0, 2, 1, 3), bias, mask)
    sink = sinks.reshape(B_KV_HEADS, B_GROUP)[None, :, :, None].astype(jnp.float32)
    mx = jnp.maximum(m, sink)
    c = jnp.exp(m - mx)
    out = num * (c / (l * c + jnp.exp(sink - mx)))[..., None]
    return out.transpose(0, 3, 1, 2, 4).reshape(b, s, B_WIDTH).astype(q.dtype)


def stick_breaking_attention(q, k, v):
    b, s = q.shape[:2]
    qh, kh, vh = (t.transpose(0, 2, 1, 3) for t in (q, k, v))
    outs = []
    for i in range(s // BLK):
        lo, hi = i * BLK, (i + 1) * BLK
        z = jnp.einsum('bhqd,bhsd->bhqs', qh[:, :, lo:hi], kh[:, :, :hi],
                       preferred_element_type=jnp.float32) * SCALE
        before = jnp.arange(hi)[None, :] < jnp.arange(lo, hi)[:, None]
        log_keep = jnp.where(before, jax.nn.log_sigmoid(-z), 0.0)
        log_rest = lax.cumsum(log_keep, axis=3, reverse=True) - log_keep
        wts = jnp.where(before, jnp.exp(jax.nn.log_sigmoid(z) + log_rest), 0.0)
        outs.append(jnp.einsum('bhqs,bhsd->bhqd', wts, vh[:, :, :hi].astype(jnp.float32)))
    out = jnp.concatenate(outs, axis=2).transpose(0, 2, 1, 3)
    return out.reshape(b, s, C_WIDTH).astype(q.dtype)


def hybrid_mixer(h, rel_bias, w_in, b_gate, sinks, w_br_a, w_br_b, w_br_c, w_out):
    b, s, _ = h.shape
    proj = h @ w_in
    a_qkv, b_q, b_kv, c_qkv, gates = jnp.split(proj, [OFF_B_Q, OFF_B_KV, OFF_C, OFF_GATE], axis=-1)
    a_qkv = a_qkv.reshape(b, s, 3, len(A_GROUPS), A_HEADS, HEAD_DIM)
    table_a = rel_bias[:, :N_A_GROUP_HEADS].reshape(NUM_BUCKETS, len(A_GROUPS), A_HEADS)
    o_a = dilated_attention(a_qkv[:, :, 0], a_qkv[:, :, 1], a_qkv[:, :, 2], table_a)
    b_kv = b_kv.reshape(b, s, 2, B_KV_HEADS, HEAD_DIM)
    o_b = sliding_window_gqa(b_q.reshape(b, s, B_Q_HEADS, HEAD_DIM), b_kv[:, :, 0], b_kv[:, :, 1],
                             sinks, rel_bias[:, N_A_GROUP_HEADS:])
    c_qkv = c_qkv.reshape(b, s, 3, C_HEADS, HEAD_DIM)
    o_c = stick_breaking_attention(c_qkv[:, :, 0], c_qkv[:, :, 1], c_qkv[:, :, 2])
    g = jax.nn.sigmoid(gates.reshape(b, s, N_BRANCH, D_MODEL) + b_gate)
    merged = g[:, :, 0] * (o_a @ w_br_a) + g[:, :, 1] * (o_b @ w_br_b) + g[:, :, 2] * (o_c @ w_br_c)
    return merged @ w_out


def conv_ffn(h, w_up, conv_w, conv_b, w_down):
    u = h @ w_up
    u = lax.conv_general_dilated(u, conv_w[:, None, :], window_strides=(1,),
                                 padding=[(CONV_WIDTH - 1, 0)],
                                 dimension_numbers=('NWC', 'WIO', 'NWC'),
                                 feature_group_count=u.shape[-1]) + conv_b
    gate, val = jnp.split(u, 2, axis=-1)
    return (jax.nn.gelu(gate, approximate=True) * val) @ w_down


def setup_inputs(seed: int = 0) -> dict:
    key = jax.random.key(seed)
    ks = jax.random.split(key, 17)

    def nrm(k, shape, scale):
        return jax.random.normal(k, shape, jnp.float32) * scale

    def gain(k):
        return 1.0 + nrm(k, (DEPTH, D_MODEL), 0.1)

    return {
        'x': nrm(ks[0], (BATCH, SEQ, D_MODEL), 1.0),
        'rel_bias': nrm(ks[1], (NUM_BUCKETS, N_BIAS_HEADS), 0.5),
        'attn_pre_norm': gain(ks[2]),
        'w_in': nrm(ks[3], (DEPTH, D_MODEL, IN_COLS), D_MODEL ** -0.5),
        'b_gate': nrm(ks[4], (DEPTH, N_BRANCH, D_MODEL), 0.1),
        'sinks': nrm(ks[5], (DEPTH, B_Q_HEADS), 0.5),
        'w_br_a': nrm(ks[6], (DEPTH, A_WIDTH, D_MODEL), A_WIDTH ** -0.5),
        'w_br_b': nrm(ks[7], (DEPTH, B_WIDTH, D_MODEL), B_WIDTH ** -0.5),
        'w_br_c': nrm(ks[8], (DEPTH, C_WIDTH, D_MODEL), C_WIDTH ** -0.5),
        'w_out': nrm(ks[9], (DEPTH, D_MODEL, D_MODEL), D_MODEL ** -0.5),
        'attn_post_norm': gain(ks[10]),
        'ffn_pre_norm': gain(ks[11]),
        'w_up': nrm(ks[12], (DEPTH, D_MODEL, 2 * D_FF), D_MODEL ** -0.5),
        'conv_w': nrm(ks[13], (DEPTH, CONV_WIDTH, 2 * D_FF), CONV_WIDTH ** -0.5),
        'conv_b': nrm(ks[14], (DEPTH, 2 * D_FF), 0.02),
        'w_down': nrm(ks[15], (DEPTH, D_FF, D_MODEL), D_FF ** -0.5),
        'ffn_post_norm': gain(ks[16]),
    }


def reference(x, rel_bias, attn_pre_norm, w_in, b_gate, sinks, w_br_a, w_br_b, w_br_c, w_out,
              attn_post_norm, ffn_pre_norm, w_up, conv_w, conv_b, w_down, ffn_post_norm):
    for layer in range(DEPTH):
        h = rms_norm(x, attn_pre_norm[layer])
        h = hybrid_mixer(h, rel_bias, w_in[layer], b_gate[layer], sinks[layer],
                         w_br_a[layer], w_br_b[layer], w_br_c[layer], w_out[layer])
        x = x + rms_norm(h, attn_post_norm[layer])
        h = rms_norm(x, ffn_pre_norm[layer])
        h = conv_ffn(h, w_up[layer], conv_w[layer], conv_b[layer], w_down[layer])
        x = x + rms_norm(h, ffn_post_norm[layer])
    return x
```

```python
import functools
import math

import numpy as np
import jax
import jax.numpy as jnp
from jax import lax
from jax.experimental import pallas as pl
from jax.experimental.pallas import tpu as pltpu

D_MODEL = 1024
BATCH = 8
SEQ = 2048
DEPTH = 2
HEAD_DIM = 64
BLK = 128
A_GROUPS = ((128, 1), (512, 4), (2048, 16))
A_HEADS = 4
B_Q_HEADS = 8
B_KV_HEADS = 2
B_WINDOW = 128
C_HEADS = 4
N_BRANCH = 3
NUM_BUCKETS = 32
MAX_DISTANCE = 2048
D_FF = 4 * D_MODEL
CONV_WIDTH = 3
EPS = 1e-6
SCALE = HEAD_DIM ** -0.5

A_WIDTH = A_HEADS * HEAD_DIM
B_WIDTH = B_Q_HEADS * HEAD_DIM
C_WIDTH = C_HEADS * HEAD_DIM
B_GROUP = B_Q_HEADS // B_KV_HEADS
N_A_GROUP_HEADS = len(A_GROUPS) * A_HEADS
N_BIAS_HEADS = N_A_GROUP_HEADS + B_Q_HEADS
A_QKV_COLS = 3 * N_A_GROUP_HEADS * HEAD_DIM
OFF_B_Q = A_QKV_COLS
OFF_B_KV = OFF_B_Q + B_WIDTH
OFF_C = OFF_B_KV + 2 * B_KV_HEADS * HEAD_DIM
OFF_GATE = OFF_C + 3 * C_WIDTH
GATE_COLS = N_BRANCH * D_MODEL

ROWS = BATCH * SEQ
N_BLK = SEQ // BLK
QKV_W = 3 * A_WIDTH
N_QKV_GROUPS = 5
NEG = -1e30

LANES = 128
V7X_VMEM_BYTES = 64 * 1024 * 1024
VMEM_LIMIT = V7X_VMEM_BYTES * 7 // 8

TM_PROJ = 512
TM_FFN = 1024
TN_FFN = 512
HALO = 8

BF16 = jnp.bfloat16
F32 = jnp.float32


def _params(*sem):
    return pltpu.CompilerParams(dimension_semantics=sem, vmem_limit_bytes=VMEM_LIMIT)


def _dot(a, b):
    return jnp.dot(a, b, preferred_element_type=F32)


def _dot_nt(a, b):
    return lax.dot_general(a, b, (((1,), (1,)), ((), ())), preferred_element_type=F32)


def _rms(xv, gain):
    y = xv * lax.rsqrt(jnp.mean(xv * xv, axis=-1, keepdims=True) + EPS)
    return y * gain


def _bucket_constants():
    a = np.arange(BLK)[:, None]
    b = np.arange(2 * BLK)[None, :]
    dist = np.maximum(a + BLK - b, 0)
    max_exact = NUM_BUCKETS // 2
    out = []
    for _, d in A_GROUPS:
        n = dist * d
        nf = np.maximum(n, 1).astype(np.float64)
        large = max_exact + (np.log(nf / max_exact) / math.log(MAX_DISTANCE / max_exact)
                             * (NUM_BUCKETS - max_exact)).astype(np.int64)
        large = np.minimum(large, NUM_BUCKETS - 1)
        out.append(np.where(n < max_exact, n, large))
    return np.stack(out).astype(np.int32)


def _bias_kernel(tab_ref, bucket_ref, out_ref):
    hh = pl.program_id(0)
    a = lax.broadcasted_iota(jnp.int32, (BLK, 2 * BLK), 0)
    b = lax.broadcasted_iota(jnp.int32, (BLK, 2 * BLK), 1)
    dist = a + BLK - b
    max_dist = jnp.where(hh < N_A_GROUP_HEADS, BLK, B_WINDOW - 1)
    valid = (dist >= 0) & (dist <= max_dist)
    bk = bucket_ref[...]
    acc = jnp.zeros((BLK, 2 * BLK), F32)
    for k in range(NUM_BUCKETS):
        acc = jnp.where(bk == k, tab_ref[k, hh], acc)
    out_ref[...] = jnp.where(valid, acc, NEG)


def _band_bias(rel_bias):
    buckets = jnp.asarray(_bucket_constants())
    return pl.pallas_call(
        _bias_kernel,
        grid=(N_BIAS_HEADS,),
        in_specs=[
            pl.BlockSpec(memory_space=pltpu.SMEM),
            pl.BlockSpec((None, BLK, 2 * BLK),
                         lambda h: (jnp.where(h < N_A_GROUP_HEADS, h // A_HEADS, 0), 0, 0)),
        ],
        out_specs=pl.BlockSpec((None, BLK, 2 * BLK), lambda h: (h, 0, 0)),
        out_shape=jax.ShapeDtypeStruct((N_BIAS_HEADS, BLK, 2 * BLK), F32),
        compiler_params=_params("arbitrary"),
        name="band_bias",
    )(rel_bias, buckets)


def _in_proj_kernel(x_ref, g_ref, w_ref, a0_ref, a1_ref, a2_ref, b_ref, c_ref, xs_s):
    gain = g_ref[...]
    n_lane_tiles = D_MODEL // LANES

    def proj(xv, group):
        h = _rms(xv, gain).astype(BF16)
        return _dot(h, w_ref[:, group * QKV_W:(group + 1) * QKV_W]).astype(BF16)

    xv = x_ref[...]
    a0_ref[...] = proj(xv, 0)
    b_ref[...] = proj(xv, 3)
    c_ref[...] = proj(xv, 4)
    for c in range(n_lane_tiles):
        xs_s[c] = xv[:, c * LANES:(c + 1) * LANES]
    for ref, (_, d), group in ((a1_ref, A_GROUPS[1], 1), (a2_ref, A_GROUPS[2], 2)):
        per = TM_PROJ // d
        xs = jnp.concatenate(
            [jnp.concatenate([xs_s[c, pl.ds(r, per, stride=d), :] for c in range(n_lane_tiles)], axis=1)
             for r in range(d)], axis=0)
        ref[...] = proj(xs, group).reshape(d, per, QKV_W)


def _in_proj(x2, gain, w_qkv):
    tiles_per_seq = SEQ // TM_PROJ
    d1, d2 = A_GROUPS[1][1], A_GROUPS[2][1]
    nat = pl.BlockSpec((TM_PROJ, QKV_W), lambda i: (i, 0))
    return pl.pallas_call(
        _in_proj_kernel,
        grid=(ROWS // TM_PROJ,),
        in_specs=[
            pl.BlockSpec((TM_PROJ, D_MODEL), lambda i: (i, 0)),
            pl.BlockSpec((1, D_MODEL), lambda i: (0, 0)),
            pl.BlockSpec((D_MODEL, N_QKV_GROUPS * QKV_W), lambda i: (0, 0)),
        ],
        out_specs=[
            nat,
            pl.BlockSpec((None, d1, TM_PROJ // d1, QKV_W),
                         lambda i: (i // tiles_per_seq, 0, i % tiles_per_seq, 0)),
            pl.BlockSpec((None, d2, TM_PROJ // d2, QKV_W),
                         lambda i: (i // tiles_per_seq, 0, i % tiles_per_seq, 0)),
            nat,
            nat,
        ],
        out_shape=[
            jax.ShapeDtypeStruct((ROWS, QKV_W), BF16),
            jax.ShapeDtypeStruct((BATCH, d1, SEQ // d1, QKV_W), BF16),
            jax.ShapeDtypeStruct((BATCH, d2, SEQ // d2, QKV_W), BF16),
            jax.ShapeDtypeStruct((ROWS, QKV_W), BF16),
            jax.ShapeDtypeStruct((ROWS, QKV_W), BF16),
        ],
        scratch_shapes=[pltpu.VMEM((D_MODEL // LANES, TM_PROJ, LANES), F32)],
        compiler_params=_params("arbitrary"),
        name="in_proj",
    )(x2, gain, w_qkv)


def _band_block(q, kp, kc, vp, vc, bias_p, bias_c):
    sp = _dot_nt(q, kp) + bias_p
    sc = _dot_nt(q, kc) + bias_c
    m = jnp.maximum(jnp.max(sp, axis=-1, keepdims=True), jnp.max(sc, axis=-1, keepdims=True))
    pp = jnp.exp(sp - m)
    pc = jnp.exp(sc - m)
    l = jnp.sum(pp, axis=-1, keepdims=True) + jnp.sum(pc, axis=-1, keepdims=True)
    num = _dot(pp.astype(BF16), vp) + _dot(pc.astype(BF16), vc)
    return num, m, l


def _attn_a_kernel(a0_ref, a1_ref, a2_ref, bias_ref, o_ref, out_s, lse_s):
    for g, (ref, (_, d)) in enumerate(zip((a0_ref, a1_ref, a2_ref), A_GROUPS)):
        nb = N_BLK // d

        def step(n, carry, g=g, ref=ref, d=d, nb=nb):
            r = n // nb
            i = n % nb
            row_c = pl.multiple_of(i * BLK, BLK)
            row_p = pl.multiple_of(jnp.maximum(i - 1, 0) * BLK, BLK)
            cur = ref[r, pl.ds(row_c, BLK), :]
            prev = ref[r, pl.ds(row_p, BLK), :]
            no_prev = jnp.where(i > 0, 0.0, NEG)
            outs, lses = [], []
            for h in range(A_HEADS):
                hs = slice(h * HEAD_DIM, (h + 1) * HEAD_DIM)
                ks = slice(A_WIDTH + h * HEAD_DIM, A_WIDTH + (h + 1) * HEAD_DIM)
                vs = slice(2 * A_WIDTH + h * HEAD_DIM, 2 * A_WIDTH + (h + 1) * HEAD_DIM)
                num, m, l = _band_block(
                    cur[:, hs], prev[:, ks], cur[:, ks], prev[:, vs], cur[:, vs],
                    bias_ref[g * A_HEADS + h, :, 0:BLK] + no_prev,
                    bias_ref[g * A_HEADS + h, :, BLK:2 * BLK])
                outs.append(num / l)
                lses.append(jnp.broadcast_to(m + jnp.log(l), (BLK, HEAD_DIM)))
            o_blk = jnp.concatenate(outs, axis=1)
            lse_blk = jnp.concatenate(lses, axis=1)
            if d == 1:
                rows = pl.ds(row_c, BLK)
            else:
                rows = pl.ds(i * (BLK * d) + r, BLK, stride=d)
            for c in range(A_WIDTH // LANES):
                out_s[g, c, rows, :] = o_blk[:, c * LANES:(c + 1) * LANES]
                lse_s[g, c, rows, :] = lse_blk[:, c * LANES:(c + 1) * LANES]
            return carry

        lax.fori_loop(0, N_BLK, step, 0)

    def combine(t, carry):
        rows = pl.ds(pl.multiple_of(t * BLK, BLK), BLK)
        halves = []
        for c in range(A_WIDTH // LANES):
            l0, l1, l2 = lse_s[0, c, rows, :], lse_s[1, c, rows, :], lse_s[2, c, rows, :]
            mx = jnp.maximum(jnp.maximum(l0, l1), l2)
            e0, e1, e2 = jnp.exp(l0 - mx), jnp.exp(l1 - mx), jnp.exp(l2 - mx)
            top = e0 * out_s[0, c, rows, :] + e1 * out_s[1, c, rows, :] + e2 * out_s[2, c, rows, :]
            halves.append(top / (e0 + e1 + e2))
        o_ref[rows, :] = jnp.concatenate(halves, axis=1).astype(BF16)
        return carry

    lax.fori_loop(0, N_BLK, combine, 0)


def _attn_a(a0, a1, a2, bias):
    d1, d2 = A_GROUPS[1][1], A_GROUPS[2][1]
    return pl.pallas_call(
        _attn_a_kernel,
        grid=(BATCH,),
        in_specs=[
            pl.BlockSpec((None, 1, SEQ, QKV_W), lambda b: (b, 0, 0, 0)),
            pl.BlockSpec((None, d1, SEQ // d1, QKV_W), lambda b: (b, 0, 0, 0)),
            pl.BlockSpec((None, d2, SEQ // d2, QKV_W), lambda b: (b, 0, 0, 0)),
            pl.BlockSpec((N_A_GROUP_HEADS, BLK, 2 * BLK), lambda b: (0, 0, 0)),
        ],
        out_specs=pl.BlockSpec((SEQ, A_WIDTH), lambda b: (b, 0)),
        out_shape=jax.ShapeDtypeStruct((ROWS, A_WIDTH), BF16),
        scratch_shapes=[
            pltpu.VMEM((len(A_GROUPS), A_WIDTH // LANES, SEQ, LANES), F32),
            pltpu.VMEM((len(A_GROUPS), A_WIDTH // LANES, SEQ, LANES), F32),
        ],
        compiler_params=_params("arbitrary"),
        name="attn_dilated",
    )(a0.reshape(BATCH, 1, SEQ, QKV_W), a1, a2, bias)


def _attn_b_kernel(sink_ref, qkv_ref, bias_ref, o_ref):
    kv_w = B_KV_HEADS * HEAD_DIM

    def step(i, carry):
        row_c = pl.multiple_of(i * BLK, BLK)
        row_p = pl.multiple_of(jnp.maximum(i - 1, 0) * BLK, BLK)
        cur = qkv_ref[pl.ds(row_c, BLK), :]
        prev = qkv_ref[pl.ds(row_p, BLK), B_WIDTH:]
        no_prev = jnp.where(i > 0, 0.0, NEG)
        outs = []
        for h in range(B_Q_HEADS):
            kvh = h // B_GROUP
            ks = slice(kvh * HEAD_DIM, (kvh + 1) * HEAD_DIM)
            vs = slice(kv_w + kvh * HEAD_DIM, kv_w + (kvh + 1) * HEAD_DIM)
            cur_kv = cur[:, B_WIDTH:]
            num, m, l = _band_block(
                cur[:, h * HEAD_DIM:(h + 1) * HEAD_DIM], prev[:, ks], cur_kv[:, ks],
                prev[:, vs], cur_kv[:, vs],
                bias_ref[h, :, 0:BLK] + no_prev, bias_ref[h, :, BLK:2 * BLK])
            sink = sink_ref[0, h]
            mx = jnp.maximum(m, sink)
            c = jnp.exp(m - mx)
            outs.append(num * (c / (l * c + jnp.exp(sink - mx))))
        o_ref[pl.ds(row_c, BLK), :] = jnp.concatenate(outs, axis=1).astype(BF16)
        return carry

    lax.fori_loop(0, N_BLK, step, 0)


def _attn_b(qkv, bias, sinks):
    return pl.pallas_call(
        _attn_b_kernel,
        grid=(BATCH,),
        in_specs=[
            pl.BlockSpec(memory_space=pltpu.SMEM),
            pl.BlockSpec((SEQ, QKV_W), lambda b: (b, 0)),
            pl.BlockSpec((B_Q_HEADS, BLK, 2 * BLK), lambda b: (0, 0, 0)),
        ],
        out_specs=pl.BlockSpec((SEQ, B_WIDTH), lambda b: (b, 0)),
        out_shape=jax.ShapeDtypeStruct((ROWS, B_WIDTH), BF16),
        compiler_params=_params("arbitrary"),
        name="attn_window",
    )(sinks.reshape(1, B_Q_HEADS), qkv, bias)


def _suffix_matrix():
    j = np.arange(2 * BLK)[:, None] % BLK
    s = np.arange(2 * BLK)[None, :]
    return ((s >= BLK) | (j > s)).astype(np.float32)


def _stick_block(q, k, v, suffix, later, strict_lower):
    z = _dot_nt(q, k)
    soft = jnp.log(1.0 + jnp.exp(-jnp.abs(z)))
    log_keep = -(jnp.maximum(z, 0.0) + soft)
    log_beta = jnp.minimum(z, 0.0) - soft
    if strict_lower is not None:
        log_keep = jnp.where(strict_lower, log_keep, 0.0)
    hi = log_keep.astype(BF16)
    lo = (log_keep - hi.astype(F32)).astype(BF16)
    sums = _dot(jnp.concatenate([hi, lo], axis=1), suffix)
    w = jnp.exp(log_beta + sums[:, :BLK] + later)
    if strict_lower is not None:
        w = jnp.where(strict_lower, w, 0.0)
    return _dot(w.astype(BF16), v), later + sums[:, BLK:]


def _attn_c_kernel(qkv_ref, suffix_ref, o_ref):
    t_idx = lax.broadcasted_iota(jnp.int32, (BLK, BLK), 0)
    s_idx = lax.broadcasted_iota(jnp.int32, (BLK, BLK), 1)
    strict_lower = s_idx < t_idx

    def q_block(i, carry):
        row_q = pl.multiple_of(i * BLK, BLK)
        outs = []
        for h in range(C_HEADS):
            qs = slice(h * HEAD_DIM, (h + 1) * HEAD_DIM)
            ks = slice(C_WIDTH + h * HEAD_DIM, C_WIDTH + (h + 1) * HEAD_DIM)
            vs = slice(2 * C_WIDTH + h * HEAD_DIM, 2 * C_WIDTH + (h + 1) * HEAD_DIM)
            q = qkv_ref[pl.ds(row_q, BLK), qs]
            acc, later = _stick_block(
                q, qkv_ref[pl.ds(row_q, BLK), ks], qkv_ref[pl.ds(row_q, BLK), vs],
                suffix_ref[...], jnp.zeros((BLK, BLK), F32), strict_lower)

            def k_block(jj, state, q=q, ks=ks, vs=vs):
                acc, later = state
                row_k = pl.multiple_of((i - 1 - jj) * BLK, BLK)
                part, later = _stick_block(
                    q, qkv_ref[pl.ds(row_k, BLK), ks], qkv_ref[pl.ds(row_k, BLK), vs],
                    suffix_ref[...], later, None)
                return acc + part, later

            acc, _ = lax.fori_loop(0, i, k_block, (acc, later))
            outs.append(acc)
        o_ref[pl.ds(row_q, BLK), :] = jnp.concatenate(outs, axis=1).astype(BF16)
        return carry

    lax.fori_loop(0, N_BLK, q_block, 0)


def _attn_c(qkv):
    suffix = jnp.asarray(_suffix_matrix(), BF16)
    return pl.pallas_call(
        _attn_c_kernel,
        grid=(BATCH,),
        in_specs=[
            pl.BlockSpec((SEQ, QKV_W), lambda b: (b, 0)),
            pl.BlockSpec((2 * BLK, 2 * BLK), lambda b: (0, 0)),
        ],
        out_specs=pl.BlockSpec((SEQ, C_WIDTH), lambda b: (b, 0)),
        out_shape=jax.ShapeDtypeStruct((ROWS, C_WIDTH), BF16),
        compiler_params=_params("arbitrary"),
        name="attn_stick",
    )(qkv, suffix)


def _merge_kernel(x_ref, oa_ref, ob_ref, oc_ref, gpre_ref, wg_ref, bg_ref, wa_ref, wb_ref, wc_ref,
                  wo_ref, gpost_ref, y_ref):
    xv = x_ref[...]
    h = _rms(xv, gpre_ref[...]).astype(BF16)
    merged = None
    for n, (o_ref, w_ref) in enumerate(((oa_ref, wa_ref), (ob_ref, wb_ref), (oc_ref, wc_ref))):
        cols = slice(n * D_MODEL, (n + 1) * D_MODEL)
        gate = jax.nn.sigmoid(_dot(h, wg_ref[:, cols]) + bg_ref[:, cols])
        term = gate * _dot(o_ref[...], w_ref[...])
        merged = term if merged is None else merged + term
    y = _dot(merged.astype(BF16), wo_ref[...])
    y_ref[...] = xv + _rms(y, gpost_ref[...])


def _merge(x2, o_a, o_b, o_c, gpre, w_gate, b_gate, w_a, w_b, w_c, w_o, gpost):
    def rows(width):
        return pl.BlockSpec((TM_PROJ, width), lambda i: (i, 0))

    def whole(shape):
        return pl.BlockSpec(shape, lambda i: (0, 0))

    return pl.pallas_call(
        _merge_kernel,
        grid=(ROWS // TM_PROJ,),
        in_specs=[
            rows(D_MODEL), rows(A_WIDTH), rows(B_WIDTH), rows(C_WIDTH),
            whole((1, D_MODEL)), whole((D_MODEL, GATE_COLS)), whole((1, GATE_COLS)),
            whole((A_WIDTH, D_MODEL)), whole((B_WIDTH, D_MODEL)), whole((C_WIDTH, D_MODEL)),
            whole((D_MODEL, D_MODEL)), whole((1, D_MODEL)),
        ],
        out_specs=rows(D_MODEL),
        out_shape=jax.ShapeDtypeStruct((ROWS, D_MODEL), F32),
        compiler_params=_params("arbitrary"),
        name="merge_out_proj",
    )(x2, o_a, o_b, o_c, gpre, w_gate, b_gate, w_a, w_b, w_c, w_o, gpost)


def _gelu_tanh(x):
    return 0.5 * x * (1.0 + jnp.tanh(math.sqrt(2.0 / math.pi) * (x + 0.044715 * (x * x * x))))


def _ffn_kernel(x_ref, gpre_ref, wg_ref, wv_ref, cwg_ref, cwv_ref, cbg_ref, cbv_ref, wd_ref,
                gpost_ref, y_ref, h_s, acc_s, tail_s):
    i = pl.program_id(0)
    j = pl.program_id(1)

    @pl.when(j == 0)
    def _():
        h_s[...] = _rms(x_ref[...], gpre_ref[...]).astype(BF16)
        acc_s[...] = jnp.zeros_like(acc_s)

    seq_start = (i % (SEQ // TM_FFN)) == 0

    def conv(u, slot, cw_ref, cb_ref):
        prev = jnp.where(seq_start, 0.0, tail_s[slot])
        tail_s[slot] = u[TM_FFN - HALO:, :]
        ext = jnp.concatenate([prev, u], axis=0)
        return (cw_ref[2:3, :] * u
                + cw_ref[1:2, :] * ext[HALO - 1:HALO - 1 + TM_FFN, :]
                + cw_ref[0:1, :] * ext[HALO - 2:HALO - 2 + TM_FFN, :]
                + cb_ref[...])

    h = h_s[...]
    gate = conv(_dot(h, wg_ref[...]), 2 * j, cwg_ref, cbg_ref)
    val = conv(_dot(h, wv_ref[...]), 2 * j + 1, cwv_ref, cbv_ref)
    act = (_gelu_tanh(gate) * val).astype(BF16)
    acc_s[...] += _dot(act, wd_ref[...])

    @pl.when(j == pl.num_programs(1) - 1)
    def _():
        y_ref[...] = x_ref[...] + _rms(acc_s[...], gpost_ref[...])


def _ffn(x2, gpre, w_up, conv_w, conv_b, w_down, gpost):
    n_chunks = D_FF // TN_FFN
    return pl.pallas_call(
        _ffn_kernel,
        grid=(ROWS // TM_FFN, n_chunks),
        in_specs=[
            pl.BlockSpec((TM_FFN, D_MODEL), lambda i, j: (i, 0)),
            pl.BlockSpec((1, D_MODEL), lambda i, j: (0, 0)),
            pl.BlockSpec((D_MODEL, TN_FFN), lambda i, j: (0, j)),
            pl.BlockSpec((D_MODEL, TN_FFN), lambda i, j: (0, j + n_chunks)),
            pl.BlockSpec((CONV_WIDTH, TN_FFN), lambda i, j: (0, j)),
            pl.BlockSpec((CONV_WIDTH, TN_FFN), lambda i, j: (0, j + n_chunks)),
            pl.BlockSpec((1, TN_FFN), lambda i, j: (0, j)),
            pl.BlockSpec((1, TN_FFN), lambda i, j: (0, j + n_chunks)),
            pl.BlockSpec((TN_FFN, D_MODEL), lambda i, j: (j, 0)),
            pl.BlockSpec((1, D_MODEL), lambda i, j: (0, 0)),
        ],
        out_specs=pl.BlockSpec((TM_FFN, D_MODEL), lambda i, j: (i, 0)),
        out_shape=jax.ShapeDtypeStruct((ROWS, D_MODEL), F32),
        scratch_shapes=[
            pltpu.VMEM((TM_FFN, D_MODEL), BF16),
            pltpu.VMEM((TM_FFN, D_MODEL), F32),
            pltpu.VMEM((2 * n_chunks, HALO, TN_FFN), F32),
        ],
        compiler_params=_params("arbitrary", "arbitrary"),
        name="conv_ffn",
    )(x2, gpre, w_up, w_up, conv_w, conv_w, conv_b, conv_b, w_down, gpost)


def _qkv_weight(w_in):
    a = w_in[:, :A_QKV_COLS].reshape(D_MODEL, 3, len(A_GROUPS), A_WIDTH)
    a = a * jnp.asarray([SCALE, 1.0, 1.0], F32)[None, :, None, None]
    a = a.transpose(0, 2, 1, 3).reshape(D_MODEL, A_QKV_COLS)
    b = jnp.concatenate([w_in[:, OFF_B_Q:OFF_B_KV] * SCALE, w_in[:, OFF_B_KV:OFF_C]], axis=1)
    c = jnp.concatenate([w_in[:, OFF_C:OFF_C + C_WIDTH] * SCALE, w_in[:, OFF_C + C_WIDTH:OFF_GATE]],
                        axis=1)
    return jnp.concatenate([a, b, c], axis=1).astype(BF16)


def kernel(x, rel_bias, attn_pre_norm, w_in, b_gate, sinks, w_br_a, w_br_b, w_br_c, w_out,
           attn_post_norm, ffn_pre_norm, w_up, conv_w, conv_b, w_down, ffn_post_norm):
    assert x.shape == (BATCH, SEQ, D_MODEL) and x.dtype == F32
    bias = _band_bias(rel_bias)
    bias_a, bias_b = bias[:N_A_GROUP_HEADS], bias[N_A_GROUP_HEADS:]
    x2 = x.reshape(ROWS, D_MODEL)
    for layer in range(DEPTH):
        gpre = attn_pre_norm[layer].reshape(1, D_MODEL)
        a0, a1, a2, bq, cq = _in_proj(x2, gpre, _qkv_weight(w_in[layer]))
        o_a = _attn_a(a0, a1, a2, bias_a)
        o_b = _attn_b(bq, bias_b, sinks[layer])
        o_c = _attn_c(cq)
        x2 = _merge(x2, o_a, o_b, o_c, gpre,
                    w_in[layer][:, OFF_GATE:].astype(BF16), b_gate[layer].reshape(1, GATE_COLS),
                    w_br_a[layer].astype(BF16), w_br_b[layer].astype(BF16),
                    w_br_c[layer].astype(BF16), w_out[layer].astype(BF16),
                    attn_post_norm[layer].reshape(1, D_MODEL))
        x2 = _ffn(x2, ffn_pre_norm[layer].reshape(1, D_MODEL), w_up[layer].astype(BF16),
                  conv_w[layer], conv_b[layer].reshape(1, 2 * D_FF), w_down[layer].astype(BF16),
                  ffn_post_norm[layer].reshape(1, D_MODEL))
    return x2.reshape(BATCH, SEQ, D_MODEL)
```

```python
import functools
import math

import numpy as np
import jax
import jax.numpy as jnp
from jax import lax
from jax.experimental import pallas as pl
from jax.experimental.pallas import tpu as pltpu

D_MODEL = 1024
BATCH = 8
SEQ = 2048
DEPTH = 2
HEAD_DIM = 64
BLK = 128
A_GROUPS = ((128, 1), (512, 4), (2048, 16))
A_HEADS = 4
B_Q_HEADS = 8
B_KV_HEADS = 2
B_WINDOW = 128
C_HEADS = 4
N_BRANCH = 3
NUM_BUCKETS = 32
MAX_DISTANCE = 2048
D_FF = 4 * D_MODEL
CONV_WIDTH = 3
EPS = 1e-6
SCALE = HEAD_DIM ** -0.5

A_WIDTH = A_HEADS * HEAD_DIM
B_WIDTH = B_Q_HEADS * HEAD_DIM
C_WIDTH = C_HEADS * HEAD_DIM
B_GROUP = B_Q_HEADS // B_KV_HEADS
N_A_GROUP_HEADS = len(A_GROUPS) * A_HEADS
N_BIAS_HEADS = N_A_GROUP_HEADS + B_Q_HEADS
A_QKV_COLS = 3 * N_A_GROUP_HEADS * HEAD_DIM
OFF_B_Q = A_QKV_COLS
OFF_B_KV = OFF_B_Q + B_WIDTH
OFF_C = OFF_B_KV + 2 * B_KV_HEADS * HEAD_DIM
OFF_GATE = OFF_C + 3 * C_WIDTH
GATE_COLS = N_BRANCH * D_MODEL

ROWS = BATCH * SEQ
N_BLK = SEQ // BLK
QKV_W = 3 * A_WIDTH
N_QKV_GROUPS = 5
NEG = -1e30
LOG2_E = 1.4426950408889634
LN_2 = 0.6931471805599453

LANES = 128
V7X_VMEM_BYTES = 64 * 1024 * 1024
VMEM_LIMIT = V7X_VMEM_BYTES * 7 // 8

TM_PROJ = 512
TM_FFN = 1024
TN_FFN = 512
HALO = 8
BAND_UNROLL = 2
STICK_UNROLL = 4

BF16 = jnp.bfloat16
F32 = jnp.float32


def _params(*sem):
    return pltpu.CompilerParams(dimension_semantics=sem, vmem_limit_bytes=VMEM_LIMIT)


def _dot(a, b):
    return jnp.dot(a, b, preferred_element_type=F32)


def _dot_nt(a, b):
    return lax.dot_general(a, b, (((1,), (1,)), ((), ())), preferred_element_type=F32)


def _rms(xv, gain):
    y = xv * lax.rsqrt(jnp.mean(xv * xv, axis=-1, keepdims=True) + EPS)
    return y * gain


def _bucket_constants():
    a = np.arange(BLK)[:, None]
    b = np.arange(2 * BLK)[None, :]
    dist = np.maximum(a + BLK - b, 0)
    max_exact = NUM_BUCKETS // 2
    out = []
    for _, d in A_GROUPS:
        n = dist * d
        nf = np.maximum(n, 1).astype(np.float64)
        large = max_exact + (np.log(nf / max_exact) / math.log(MAX_DISTANCE / max_exact)
                             * (NUM_BUCKETS - max_exact)).astype(np.int64)
        large = np.minimum(large, NUM_BUCKETS - 1)
        out.append(np.where(n < max_exact, n, large))
    return np.stack(out).astype(np.int32)


def _bias_kernel(tab_ref, bucket_ref, out_ref):
    hh = pl.program_id(0)
    a = lax.broadcasted_iota(jnp.int32, (BLK, 2 * BLK), 0)
    b = lax.broadcasted_iota(jnp.int32, (BLK, 2 * BLK), 1)
    dist = a + BLK - b
    max_dist = jnp.where(hh < N_A_GROUP_HEADS, BLK, B_WINDOW - 1)
    valid = (dist >= 0) & (dist <= max_dist)
    bk = bucket_ref[...]
    acc = jnp.zeros((BLK, 2 * BLK), F32)
    for k in range(NUM_BUCKETS):
        acc = jnp.where(bk == k, tab_ref[k, hh], acc)
    out_ref[...] = jnp.where(valid, acc, NEG)


def _band_bias(rel_bias):
    buckets = jnp.asarray(_bucket_constants())
    return pl.pallas_call(
        _bias_kernel,
        grid=(N_BIAS_HEADS,),
        in_specs=[
            pl.BlockSpec(memory_space=pltpu.SMEM),
            pl.BlockSpec((None, BLK, 2 * BLK),
                         lambda h: (jnp.where(h < N_A_GROUP_HEADS, h // A_HEADS, 0), 0, 0)),
        ],
        out_specs=pl.BlockSpec((None, BLK, 2 * BLK), lambda h: (h, 0, 0)),
        out_shape=jax.ShapeDtypeStruct((N_BIAS_HEADS, BLK, 2 * BLK), F32),
        compiler_params=_params("arbitrary"),
        name="band_bias",
    )(rel_bias, buckets)


def _in_proj_kernel(x_ref, g_ref, w_ref, a0_ref, a1_ref, a2_ref, b_ref, c_ref, xs_s):
    gain = g_ref[...]
    n_lane_tiles = D_MODEL // LANES

    def proj(xv, group):
        h = _rms(xv, gain).astype(BF16)
        return _dot(h, w_ref[:, group * QKV_W:(group + 1) * QKV_W]).astype(BF16)

    xv = x_ref[...]
    a0_ref[...] = proj(xv, 0)
    b_ref[...] = proj(xv, 3)
    c_ref[...] = proj(xv, 4)
    for c in range(n_lane_tiles):
        xs_s[c] = xv[:, c * LANES:(c + 1) * LANES]
    for ref, (_, d), group in ((a1_ref, A_GROUPS[1], 1), (a2_ref, A_GROUPS[2], 2)):
        per = TM_PROJ // d
        xs = jnp.concatenate(
            [jnp.concatenate([xs_s[c, pl.ds(r, per, stride=d), :] for c in range(n_lane_tiles)], axis=1)
             for r in range(d)], axis=0)
        ref[...] = proj(xs, group).reshape(d, per, QKV_W)


def _in_proj(x2, gain, w_qkv):
    tiles_per_seq = SEQ // TM_PROJ
    d1, d2 = A_GROUPS[1][1], A_GROUPS[2][1]
    nat = pl.BlockSpec((TM_PROJ, QKV_W), lambda i: (i, 0))
    return pl.pallas_call(
        _in_proj_kernel,
        grid=(ROWS // TM_PROJ,),
        in_specs=[
            pl.BlockSpec((TM_PROJ, D_MODEL), lambda i: (i, 0)),
            pl.BlockSpec((1, D_MODEL), lambda i: (0, 0)),
            pl.BlockSpec((D_MODEL, N_QKV_GROUPS * QKV_W), lambda i: (0, 0)),
        ],
        out_specs=[
            nat,
            pl.BlockSpec((None, d1, TM_PROJ // d1, QKV_W),
                         lambda i: (i // tiles_per_seq, 0, i % tiles_per_seq, 0)),
            pl.BlockSpec((None, d2, TM_PROJ // d2, QKV_W),
                         lambda i: (i // tiles_per_seq, 0, i % tiles_per_seq, 0)),
            nat,
            nat,
        ],
        out_shape=[
            jax.ShapeDtypeStruct((ROWS, QKV_W), BF16),
            jax.ShapeDtypeStruct((BATCH, d1, SEQ // d1, QKV_W), BF16),
            jax.ShapeDtypeStruct((BATCH, d2, SEQ // d2, QKV_W), BF16),
            jax.ShapeDtypeStruct((ROWS, QKV_W), BF16),
            jax.ShapeDtypeStruct((ROWS, QKV_W), BF16),
        ],
        scratch_shapes=[pltpu.VMEM((D_MODEL // LANES, TM_PROJ, LANES), F32)],
        compiler_params=_params("arbitrary"),
        name="in_proj",
    )(x2, gain, w_qkv)


def _band_blocks(qs, kps, kcs, vps, vcs, bias_ps, bias_cs):
    sps = [_dot_nt(q, k) for q, k in zip(qs, kps)]
    scs = [_dot_nt(q, k) for q, k in zip(qs, kcs)]
    pps, pcs, stats = [], [], []
    for sp, sc, bias_p, bias_c in zip(sps, scs, bias_ps, bias_cs):
        sp = sp + bias_p
        sc = sc + bias_c
        m = jnp.max(jnp.maximum(sp, sc), axis=-1, keepdims=True)
        pp = jnp.exp(sp - m)
        pc = jnp.exp(sc - m)
        stats.append((m, jnp.sum(pp + pc, axis=-1, keepdims=True)))
        pps.append(pp.astype(BF16))
        pcs.append(pc.astype(BF16))
    nums = [_dot(pp, vp) + _dot(pc, vc) for pp, pc, vp, vc in zip(pps, pcs, vps, vcs)]
    return nums, stats


def _attn_a_kernel(a0_ref, a1_ref, a2_ref, bias_ref, o_ref, out_s, lse_s):
    def cols(part, h):
        return slice(part * A_WIDTH + h * HEAD_DIM, part * A_WIDTH + (h + 1) * HEAD_DIM)

    for g, (ref, (_, d)) in enumerate(zip((a0_ref, a1_ref, a2_ref), A_GROUPS)):
        nb = N_BLK // d

        def step(n, carry, g=g, ref=ref, d=d, nb=nb):
            blocks, qs, kps, kcs, vps, vcs, bias_ps, bias_cs = [], [], [], [], [], [], [], []
            for u in range(BAND_UNROLL):
                blk = n * BAND_UNROLL + u
                r = blk // nb
                i = blk % nb
                row_c = pl.multiple_of(i * BLK, BLK)
                row_p = pl.multiple_of(jnp.maximum(i - 1, 0) * BLK, BLK)
                no_prev = jnp.where(i > 0, 0.0, NEG)
                blocks.append((r, i, row_c))
                for h in range(A_HEADS):
                    qs.append(ref[r, pl.ds(row_c, BLK), cols(0, h)])
                    kps.append(ref[r, pl.ds(row_p, BLK), cols(1, h)])
                    kcs.append(ref[r, pl.ds(row_c, BLK), cols(1, h)])
                    vps.append(ref[r, pl.ds(row_p, BLK), cols(2, h)])
                    vcs.append(ref[r, pl.ds(row_c, BLK), cols(2, h)])
                    bias_ps.append(bias_ref[g * A_HEADS + h, :, 0:BLK] + no_prev)
                    bias_cs.append(bias_ref[g * A_HEADS + h, :, BLK:2 * BLK])
            nums, stats = _band_blocks(qs, kps, kcs, vps, vcs, bias_ps, bias_cs)
            for u, (r, i, row_c) in enumerate(blocks):
                units = range(u * A_HEADS, (u + 1) * A_HEADS)
                o_blk = jnp.concatenate([nums[n_] / stats[n_][1] for n_ in units], axis=1)
                lse_blk = jnp.concatenate(
                    [jnp.broadcast_to(stats[n_][0] + jnp.log(stats[n_][1]), (BLK, HEAD_DIM))
                     for n_ in units], axis=1)
                if d == 1:
                    rows = pl.ds(row_c, BLK)
                else:
                    rows = pl.ds(i * (BLK * d) + r, BLK, stride=d)
                for c in range(A_WIDTH // LANES):
                    out_s[g, c, rows, :] = o_blk[:, c * LANES:(c + 1) * LANES]
                    lse_s[g, c, rows, :] = lse_blk[:, c * LANES:(c + 1) * LANES]
            return carry

        lax.fori_loop(0, N_BLK // BAND_UNROLL, step, 0)

    def combine(t, carry):
        rows = pl.ds(pl.multiple_of(t * BLK, BLK), BLK)
        halves = []
        for c in range(A_WIDTH // LANES):
            l0, l1, l2 = lse_s[0, c, rows, :], lse_s[1, c, rows, :], lse_s[2, c, rows, :]
            mx = jnp.maximum(jnp.maximum(l0, l1), l2)
            e0, e1, e2 = jnp.exp(l0 - mx), jnp.exp(l1 - mx), jnp.exp(l2 - mx)
            top = e0 * out_s[0, c, rows, :] + e1 * out_s[1, c, rows, :] + e2 * out_s[2, c, rows, :]
            halves.append(top / (e0 + e1 + e2))
        o_ref[rows, :] = jnp.concatenate(halves, axis=1).astype(BF16)
        return carry

    lax.fori_loop(0, N_BLK, combine, 0)


def _attn_a(a0, a1, a2, bias):
    d1, d2 = A_GROUPS[1][1], A_GROUPS[2][1]
    return pl.pallas_call(
        _attn_a_kernel,
        grid=(BATCH,),
        in_specs=[
            pl.BlockSpec((None, 1, SEQ, QKV_W), lambda b: (b, 0, 0, 0)),
            pl.BlockSpec((None, d1, SEQ // d1, QKV_W), lambda b: (b, 0, 0, 0)),
            pl.BlockSpec((None, d2, SEQ // d2, QKV_W), lambda b: (b, 0, 0, 0)),
            pl.BlockSpec((N_A_GROUP_HEADS, BLK, 2 * BLK), lambda b: (0, 0, 0)),
        ],
        out_specs=pl.BlockSpec((SEQ, A_WIDTH), lambda b: (b, 0)),
        out_shape=jax.ShapeDtypeStruct((ROWS, A_WIDTH), BF16),
        scratch_shapes=[
            pltpu.VMEM((len(A_GROUPS), A_WIDTH // LANES, SEQ, LANES), F32),
            pltpu.VMEM((len(A_GROUPS), A_WIDTH // LANES, SEQ, LANES), F32),
        ],
        compiler_params=_params("arbitrary"),
        name="attn_dilated",
    )(a0.reshape(BATCH, 1, SEQ, QKV_W), a1, a2, bias)


def _attn_b_kernel(sink_ref, qkv_ref, bias_ref, o_ref):
    kv_w = B_KV_HEADS * HEAD_DIM

    def step(n, carry):
        blocks, qs, kps, kcs, vps, vcs, bias_ps, bias_cs = [], [], [], [], [], [], [], []
        for u in range(BAND_UNROLL):
            i = n * BAND_UNROLL + u
            row_c = pl.multiple_of(i * BLK, BLK)
            row_p = pl.multiple_of(jnp.maximum(i - 1, 0) * BLK, BLK)
            no_prev = jnp.where(i > 0, 0.0, NEG)
            blocks.append(row_c)
            for kvh in range(B_KV_HEADS):
                heads = range(kvh * B_GROUP, (kvh + 1) * B_GROUP)
                qs.append(jnp.concatenate(
                    [qkv_ref[pl.ds(row_c, BLK), h * HEAD_DIM:(h + 1) * HEAD_DIM] for h in heads],
                    axis=0))
                ks = slice(B_WIDTH + kvh * HEAD_DIM, B_WIDTH + (kvh + 1) * HEAD_DIM)
                vs = slice(B_WIDTH + kv_w + kvh * HEAD_DIM, B_WIDTH + kv_w + (kvh + 1) * HEAD_DIM)
                kps.append(qkv_ref[pl.ds(row_p, BLK), ks])
                kcs.append(qkv_ref[pl.ds(row_c, BLK), ks])
                vps.append(qkv_ref[pl.ds(row_p, BLK), vs])
                vcs.append(qkv_ref[pl.ds(row_c, BLK), vs])
                bias_ps.append(jnp.concatenate([bias_ref[h, :, 0:BLK] for h in heads], axis=0) + no_prev)
                bias_cs.append(jnp.concatenate([bias_ref[h, :, BLK:2 * BLK] for h in heads], axis=0))
        nums, stats = _band_blocks(qs, kps, kcs, vps, vcs, bias_ps, bias_cs)
        for u, row_c in enumerate(blocks):
            outs = []
            for kvh in range(B_KV_HEADS):
                num = nums[u * B_KV_HEADS + kvh]
                m, l = stats[u * B_KV_HEADS + kvh]
                for gq in range(B_GROUP):
                    rows = slice(gq * BLK, (gq + 1) * BLK)
                    sink = sink_ref[0, kvh * B_GROUP + gq]
                    mx = jnp.maximum(m[rows], sink)
                    c = jnp.exp(m[rows] - mx)
                    outs.append(num[rows] * (c / (l[rows] * c + jnp.exp(sink - mx))))
            o_ref[pl.ds(row_c, BLK), :] = jnp.concatenate(outs, axis=1).astype(BF16)
        return carry

    lax.fori_loop(0, N_BLK // BAND_UNROLL, step, 0)


def _attn_b(qkv, bias, sinks):
    return pl.pallas_call(
        _attn_b_kernel,
        grid=(BATCH,),
        in_specs=[
            pl.BlockSpec(memory_space=pltpu.SMEM),
            pl.BlockSpec((SEQ, QKV_W), lambda b: (b, 0)),
            pl.BlockSpec((B_Q_HEADS, BLK, 2 * BLK), lambda b: (0, 0, 0)),
        ],
        out_specs=pl.BlockSpec((SEQ, B_WIDTH), lambda b: (b, 0)),
        out_shape=jax.ShapeDtypeStruct((ROWS, B_WIDTH), BF16),
        compiler_params=_params("arbitrary"),
        name="attn_window",
    )(sinks.reshape(1, B_Q_HEADS), qkv, bias)


def _suffix_matrix():
    j = np.arange(2 * BLK)[:, None] % BLK
    s = np.arange(2 * BLK)[None, :]
    return ((s >= BLK) | (j > s)).astype(np.float32)


def _stick_blocks(qs, ks, vs, suffix, laters, strict_lower):
    zs = [_dot_nt(q, k) for q, k in zip(qs, ks)]
    log_betas, splits = [], []
    for z in zs:
        neg_soft = jnp.log2(1.0 + jnp.exp2(jnp.abs(z) * -LOG2_E)) * -LN_2
        log_keep = neg_soft - jnp.maximum(z, 0.0)
        log_betas.append(z + log_keep)
        if laters is None:
            log_keep = jnp.where(strict_lower, log_keep, 0.0)
        hi = log_keep.astype(BF16)
        lo = (log_keep - hi.astype(F32)).astype(BF16)
        splits.append(jnp.concatenate([hi, lo], axis=1))
    sums = [_dot(s, suffix) for s in splits]
    ws, new_laters = [], []
    for n, (log_beta, s) in enumerate(zip(log_betas, sums)):
        if laters is None:
            w = jnp.where(strict_lower, jnp.exp(log_beta + s[:, :BLK]), 0.0)
            new_laters.append(s[:, BLK:])
        else:
            w = jnp.exp(log_beta + s[:, :BLK] + laters[n])
            new_laters.append(laters[n] + s[:, BLK:])
        ws.append(w.astype(BF16))
    return [_dot(w, v) for w, v in zip(ws, vs)], new_laters


def _stick_schedule():
    q_blk, slot, k_blk = [], [], []
    for dist in range(1, N_BLK):
        group = [(i, i, i - dist) for i in range(dist, N_BLK)]
        group += [(0, N_BLK, 0)] * (-len(group) % STICK_UNROLL)
        for qi, sl, kj in group:
            q_blk.append(qi)
            slot.append(sl)
            k_blk.append(kj)
    return (np.asarray(q_blk, np.int32), np.asarray(slot, np.int32), np.asarray(k_blk, np.int32))


def _attn_c_kernel(qblk_ref, slot_ref, kblk_ref, qkv_ref, suffix_ref, o_ref, later_s, acc_s):
    t_idx = lax.broadcasted_iota(jnp.int32, (BLK, BLK), 0)
    s_idx = lax.broadcasted_iota(jnp.int32, (BLK, BLK), 1)
    strict_lower = s_idx < t_idx

    def cols(part, h):
        return slice(part * C_WIDTH + h * HEAD_DIM, part * C_WIDTH + (h + 1) * HEAD_DIM)

    def rows_of(blk):
        return pl.ds(pl.multiple_of(blk * BLK, BLK), BLK)

    for h in range(C_HEADS):
        later_s[h, N_BLK] = jnp.zeros((BLK, BLK), F32)
        acc_s[h, N_BLK] = jnp.zeros((BLK, HEAD_DIM), F32)

    def diag(n, carry):
        where = [(h, n * STICK_UNROLL + u) for u in range(STICK_UNROLL) for h in range(C_HEADS)]
        parts, laters = _stick_blocks(
            [qkv_ref[rows_of(i), cols(0, h)] for h, i in where],
            [qkv_ref[rows_of(i), cols(1, h)] for h, i in where],
            [qkv_ref[rows_of(i), cols(2, h)] for h, i in where],
            suffix_ref[...], None, strict_lower)
        for (h, i), part, later in zip(where, parts, laters):
            acc_s[h, i] = part
            later_s[h, i] = later
        return carry

    lax.fori_loop(0, N_BLK // STICK_UNROLL, diag, 0)

    def off_diag(n, carry):
        where = []
        for u in range(STICK_UNROLL):
            e = n * STICK_UNROLL + u
            q_rows, k_rows, sl = rows_of(qblk_ref[e]), rows_of(kblk_ref[e]), slot_ref[e]
            where += [(h, sl, q_rows, k_rows) for h in range(C_HEADS)]
        accs = [acc_s[h, sl] for h, sl, _, _ in where]
        parts, laters = _stick_blocks(
            [qkv_ref[q_rows, cols(0, h)] for h, _, q_rows, _ in where],
            [qkv_ref[k_rows, cols(1, h)] for h, _, _, k_rows in where],
            [qkv_ref[k_rows, cols(2, h)] for h, _, _, k_rows in where],
            suffix_ref[...], [later_s[h, sl] for h, sl, _, _ in where], None)
        for (h, sl, _, _), acc, part, later in zip(where, accs, parts, laters):
            acc_s[h, sl] = acc + part
            later_s[h, sl] = later
        return carry

    lax.fori_loop(0, qblk_ref.shape[0] // STICK_UNROLL, off_diag, 0)

    def emit(i, carry):
        o_ref[rows_of(i), :] = jnp.concatenate(
            [acc_s[h, i] for h in range(C_HEADS)], axis=1).astype(BF16)
        return carry

    lax.fori_loop(0, N_BLK, emit, 0)


def _attn_c(qkv):
    suffix = jnp.asarray(_suffix_matrix(), BF16)
    schedule = [jnp.asarray(t) for t in _stick_schedule()]
    smem = pl.BlockSpec(memory_space=pltpu.SMEM)
    return pl.pallas_call(
        _attn_c_kernel,
        grid=(BATCH,),
        in_specs=[
            smem, smem, smem,
            pl.BlockSpec((SEQ, QKV_W), lambda b: (b, 0)),
            pl.BlockSpec((2 * BLK, 2 * BLK), lambda b: (0, 0)),
        ],
        out_specs=pl.BlockSpec((SEQ, C_WIDTH), lambda b: (b, 0)),
        out_shape=jax.ShapeDtypeStruct((ROWS, C_WIDTH), BF16),
        scratch_shapes=[
            pltpu.VMEM((C_HEADS, N_BLK + 1, BLK, BLK), F32),
            pltpu.VMEM((C_HEADS, N_BLK + 1, BLK, HEAD_DIM), F32),
        ],
        compiler_params=_params("arbitrary"),
        name="attn_stick",
    )(*schedule, qkv, suffix)


def _merge_kernel(x_ref, oa_ref, ob_ref, oc_ref, gpre_ref, wg_ref, bg_ref, wa_ref, wb_ref, wc_ref,
                  wo_ref, gpost_ref, y_ref):
    xv = x_ref[...]
    h = _rms(xv, gpre_ref[...]).astype(BF16)
    merged = None
    for n, (o_ref, w_ref) in enumerate(((oa_ref, wa_ref), (ob_ref, wb_ref), (oc_ref, wc_ref))):
        cols = slice(n * D_MODEL, (n + 1) * D_MODEL)
        gate = jax.nn.sigmoid(_dot(h, wg_ref[:, cols]) + bg_ref[:, cols])
        term = gate * _dot(o_ref[...], w_ref[...])
        merged = term if merged is None else merged + term
    y = _dot(merged.astype(BF16), wo_ref[...])
    y_ref[...] = xv + _rms(y, gpost_ref[...])


def _merge(x2, o_a, o_b, o_c, gpre, w_gate, b_gate, w_a, w_b, w_c, w_o, gpost):
    def rows(width):
        return pl.BlockSpec((TM_PROJ, width), lambda i: (i, 0))

    def whole(shape):
        return pl.BlockSpec(shape, lambda i: (0, 0))

    return pl.pallas_call(
        _merge_kernel,
        grid=(ROWS // TM_PROJ,),
        in_specs=[
            rows(D_MODEL), rows(A_WIDTH), rows(B_WIDTH), rows(C_WIDTH),
            whole((1, D_MODEL)), whole((D_MODEL, GATE_COLS)), whole((1, GATE_COLS)),
            whole((A_WIDTH, D_MODEL)), whole((B_WIDTH, D_MODEL)), whole((C_WIDTH, D_MODEL)),
            whole((D_MODEL, D_MODEL)), whole((1, D_MODEL)),
        ],
        out_specs=rows(D_MODEL),
        out_shape=jax.ShapeDtypeStruct((ROWS, D_MODEL), F32),
        compiler_params=_params("arbitrary"),
        name="merge_out_proj",
    )(x2, o_a, o_b, o_c, gpre, w_gate, b_gate, w_a, w_b, w_c, w_o, gpost)


def _gelu_tanh(x):
    return 0.5 * x * (1.0 + jnp.tanh(math.sqrt(2.0 / math.pi) * (x + 0.044715 * (x * x * x))))


def _ffn_kernel(x_ref, gpre_ref, wg_ref, wv_ref, cwg_ref, cwv_ref, cbg_ref, cbv_ref, wd_ref,
                gpost_ref, y_ref, h_s, acc_s, tail_s):
    i = pl.program_id(0)
    j = pl.program_id(1)

    @pl.when(j == 0)
    def _():
        h_s[...] = _rms(x_ref[...], gpre_ref[...]).astype(BF16)
        acc_s[...] = jnp.zeros_like(acc_s)

    seq_start = (i % (SEQ // TM_FFN)) == 0

    def conv(u, slot, cw_ref, cb_ref):
        prev = jnp.where(seq_start, 0.0, tail_s[slot])
        tail_s[slot] = u[TM_FFN - HALO:, :]
        ext = jnp.concatenate([prev, u], axis=0)
        return (cw_ref[2:3, :] * u
                + cw_ref[1:2, :] * ext[HALO - 1:HALO - 1 + TM_FFN, :]
                + cw_ref[0:1, :] * ext[HALO - 2:HALO - 2 + TM_FFN, :]
                + cb_ref[...])

    h = h_s[...]
    gate = conv(_dot(h, wg_ref[...]), 2 * j, cwg_ref, cbg_ref)
    val = conv(_dot(h, wv_ref[...]), 2 * j + 1, cwv_ref, cbv_ref)
    act = (_gelu_tanh(gate) * val).astype(BF16)
    acc_s[...] += _dot(act, wd_ref[...])

    @pl.when(j == pl.num_programs(1) - 1)
    def _():
        y_ref[...] = x_ref[...] + _rms(acc_s[...], gpost_ref[...])


def _ffn(x2, gpre, w_up, conv_w, conv_b, w_down, gpost):
    n_chunks = D_FF // TN_FFN
    return pl.pallas_call(
        _ffn_kernel,
        grid=(ROWS // TM_FFN, n_chunks),
        in_specs=[
            pl.BlockSpec((TM_FFN, D_MODEL), lambda i, j: (i, 0)),
            pl.BlockSpec((1, D_MODEL), lambda i, j: (0, 0)),
            pl.BlockSpec((D_MODEL, TN_FFN), lambda i, j: (0, j)),
            pl.BlockSpec((D_MODEL, TN_FFN), lambda i, j: (0, j + n_chunks)),
            pl.BlockSpec((CONV_WIDTH, TN_FFN), lambda i, j: (0, j)),
            pl.BlockSpec((CONV_WIDTH, TN_FFN), lambda i, j: (0, j + n_chunks)),
            pl.BlockSpec((1, TN_FFN), lambda i, j: (0, j)),
            pl.BlockSpec((1, TN_FFN), lambda i, j: (0, j + n_chunks)),
            pl.BlockSpec((TN_FFN, D_MODEL), lambda i, j: (j, 0)),
            pl.BlockSpec((1, D_MODEL), lambda i, j: (0, 0)),
        ],
        out_specs=pl.BlockSpec((TM_FFN, D_MODEL), lambda i, j: (i, 0)),
        out_shape=jax.ShapeDtypeStruct((ROWS, D_MODEL), F32),
        scratch_shapes=[
            pltpu.VMEM((TM_FFN, D_MODEL), BF16),
            pltpu.VMEM((TM_FFN, D_MODEL), F32),
            pltpu.VMEM((2 * n_chunks, HALO, TN_FFN), F32),
        ],
        compiler_params=_params("arbitrary", "arbitrary"),
        name="conv_ffn",
    )(x2, gpre, w_up, w_up, conv_w, conv_w, conv_b, conv_b, w_down, gpost)


def _qkv_weight(w_in):
    a = w_in[:, :A_QKV_COLS].reshape(D_MODEL, 3, len(A_GROUPS), A_WIDTH)
    a = a * jnp.asarray([SCALE, 1.0, 1.0], F32)[None, :, None, None]
    a = a.transpose(0, 2, 1, 3).reshape(D_MODEL, A_QKV_COLS)
    b = jnp.concatenate([w_in[:, OFF_B_Q:OFF_B_KV] * SCALE, w_in[:, OFF_B_KV:OFF_C]], axis=1)
    c = jnp.concatenate([w_in[:, OFF_C:OFF_C + C_WIDTH] * SCALE, w_in[:, OFF_C + C_WIDTH:OFF_GATE]],
                        axis=1)
    return jnp.concatenate([a, b, c], axis=1).astype(BF16)


def kernel(x, rel_bias, attn_pre_norm, w_in, b_gate, sinks, w_br_a, w_br_b, w_br_c, w_out,
           attn_post_norm, ffn_pre_norm, w_up, conv_w, conv_b, w_down, ffn_post_norm):
    assert x.shape == (BATCH, SEQ, D_MODEL) and x.dtype == F32
    bias = _band_bias(rel_bias)
    bias_a, bias_b = bias[:N_A_GROUP_HEADS], bias[N_A_GROUP_HEADS:]
    x2 = x.reshape(ROWS, D_MODEL)
    for layer in range(DEPTH):
        gpre = attn_pre_norm[layer].reshape(1, D_MODEL)
        a0, a1, a2, bq, cq = _in_proj(x2, gpre, _qkv_weight(w_in[layer]))
        o_a = _attn_a(a0, a1, a2, bias_a)
        o_b = _attn_b(bq, bias_b, sinks[layer])
        o_c = _attn_c(cq)
        x2 = _merge(x2, o_a, o_b, o_c, gpre,
                    w_in[layer][:, OFF_GATE:].astype(BF16), b_gate[layer].reshape(1, GATE_COLS),
                    w_br_a[layer].astype(BF16), w_br_b[layer].astype(BF16),
                    w_br_c[layer].astype(BF16), w_out[layer].astype(BF16),
                    attn_post_norm[layer].reshape(1, D_MODEL))
        x2 = _ffn(x2, ffn_pre_norm[layer].reshape(1, D_MODEL), w_up[layer].astype(BF16),
                  conv_w[layer], conv_b[layer].reshape(1, 2 * D_FF), w_down[layer].astype(BF16),
                  ffn_post_norm[layer].reshape(1, D_MODEL))
    return x2.reshape(BATCH, SEQ, D_MODEL)
```

```python
import math

import numpy as np
import jax
import jax.numpy as jnp
from jax import lax
from jax.experimental import pallas as pl
from jax.experimental.pallas import tpu as pltpu

D_MODEL = 1024
BATCH = 8
SEQ = 2048
DEPTH = 2
HEAD_DIM = 64
BLK = 128
A_GROUPS = ((128, 1), (512, 4), (2048, 16))
A_HEADS = 4
B_Q_HEADS = 8
B_KV_HEADS = 2
B_WINDOW = 128
C_HEADS = 4
N_BRANCH = 3
NUM_BUCKETS = 32
MAX_DISTANCE = 2048
D_FF = 4 * D_MODEL
CONV_WIDTH = 3
EPS = 1e-6
SCALE = HEAD_DIM ** -0.5

A_WIDTH = A_HEADS * HEAD_DIM
B_WIDTH = B_Q_HEADS * HEAD_DIM
C_WIDTH = C_HEADS * HEAD_DIM
B_GROUP = B_Q_HEADS // B_KV_HEADS
N_A_GROUP_HEADS = len(A_GROUPS) * A_HEADS
N_BIAS_HEADS = N_A_GROUP_HEADS + B_Q_HEADS
A_QKV_COLS = 3 * N_A_GROUP_HEADS * HEAD_DIM
OFF_B_Q = A_QKV_COLS
OFF_B_KV = OFF_B_Q + B_WIDTH
OFF_C = OFF_B_KV + 2 * B_KV_HEADS * HEAD_DIM
OFF_GATE = OFF_C + 3 * C_WIDTH
GATE_COLS = N_BRANCH * D_MODEL

ROWS = BATCH * SEQ
N_BLK = SEQ // BLK
QKV_W = 3 * A_WIDTH
LANES = 128
B_COLS = B_WIDTH + 2 * B_KV_HEADS * LANES
NAT_COLS = QKV_W + B_COLS + QKV_W
NEG = -1e30
LOG2_E = 1.4426950408889634
LN_2 = 0.6931471805599453

V7X_VMEM_BYTES = 64 * 1024 * 1024
VMEM_LIMIT = V7X_VMEM_BYTES * 7 // 8

TM_PROJ = 512
TM_FFN = 512
TN_FFN = 512
FFN_AHEAD = 2
HALO = 8
BAND_UNROLL = 2
STICK_UNROLL = 4

BF16 = jnp.bfloat16
F32 = jnp.float32


def _params(*sem):
    return pltpu.CompilerParams(dimension_semantics=sem, vmem_limit_bytes=VMEM_LIMIT)


def _dot(a, b):
    return jnp.dot(a, b, preferred_element_type=F32)


def _dot_nt(a, b):
    return lax.dot_general(a, b, (((1,), (1,)), ((), ())), preferred_element_type=F32)


def _rms(xv, gain):
    y = xv * lax.rsqrt(jnp.mean(xv * xv, axis=-1, keepdims=True) + EPS)
    return y * gain


def _bucket_constants():
    a = np.arange(BLK)[:, None]
    b = np.arange(2 * BLK)[None, :]
    dist = np.maximum(a + BLK - b, 0)
    max_exact = NUM_BUCKETS // 2
    out = []
    for _, d in A_GROUPS:
        n = dist * d
        nf = np.maximum(n, 1).astype(np.float64)
        large = max_exact + (np.log(nf / max_exact) / math.log(MAX_DISTANCE / max_exact)
                             * (NUM_BUCKETS - max_exact)).astype(np.int64)
        large = np.minimum(large, NUM_BUCKETS - 1)
        out.append(np.where(n < max_exact, n, large))
    return np.stack(out).astype(np.int32)


def _bias_kernel(tab_ref, bucket_ref, out_ref):
    hh = pl.program_id(0)
    a = lax.broadcasted_iota(jnp.int32, (BLK, 2 * BLK), 0)
    b = lax.broadcasted_iota(jnp.int32, (BLK, 2 * BLK), 1)
    dist = a + BLK - b
    max_dist = jnp.where(hh < N_A_GROUP_HEADS, BLK, B_WINDOW - 1)
    valid = (dist >= 0) & (dist <= max_dist)
    bk = bucket_ref[...]
    acc = jnp.zeros((BLK, 2 * BLK), F32)
    for k in range(NUM_BUCKETS):
        acc = jnp.where(bk == k, tab_ref[k, hh], acc)
    out_ref[...] = jnp.where(valid, acc, NEG)


def _band_bias(rel_bias):
    buckets = jnp.asarray(_bucket_constants())
    return pl.pallas_call(
        _bias_kernel,
        grid=(N_BIAS_HEADS,),
        in_specs=[
            pl.BlockSpec(memory_space=pltpu.SMEM),
            pl.BlockSpec((None, BLK, 2 * BLK),
                         lambda h: (jnp.where(h < N_A_GROUP_HEADS, h // A_HEADS, 0), 0, 0)),
        ],
        out_specs=pl.BlockSpec((None, BLK, 2 * BLK), lambda h: (h, 0, 0)),
        out_shape=jax.ShapeDtypeStruct((N_BIAS_HEADS, BLK, 2 * BLK), F32),
        compiler_params=_params("arbitrary"),
        name="band_bias",
    )(rel_bias, buckets)


def _in_proj_kernel(x_ref, g_ref, w_ref, a0_ref, a1_ref, a2_ref, b_ref, c_ref, xs_s):
    gain = g_ref[...]
    n_lane_tiles = D_MODEL // LANES

    def proj(xv, col0, width):
        h = _rms(xv, gain).astype(BF16)
        return _dot(h, w_ref[:, col0:col0 + width]).astype(BF16)

    xv = x_ref[...]
    nat = proj(xv, 0, NAT_COLS)
    a0_ref[...] = nat[:, :QKV_W]
    b_ref[...] = nat[:, QKV_W:QKV_W + B_COLS]
    c_ref[...] = nat[:, QKV_W + B_COLS:]
    for c in range(n_lane_tiles):
        xs_s[c] = xv[:, c * LANES:(c + 1) * LANES]
    for ref, (_, d), col0 in ((a1_ref, A_GROUPS[1], NAT_COLS), (a2_ref, A_GROUPS[2], NAT_COLS + QKV_W)):
        per = TM_PROJ // d
        xs = jnp.concatenate(
            [jnp.concatenate([xs_s[c, pl.ds(r, per, stride=d), :] for c in range(n_lane_tiles)], axis=1)
             for r in range(d)], axis=0)
        ref[...] = proj(xs, col0, QKV_W).reshape(d, per, QKV_W)


def _in_proj(x2, gain, w_qkv):
    tiles_per_seq = SEQ // TM_PROJ
    d1, d2 = A_GROUPS[1][1], A_GROUPS[2][1]

    def nat(width):
        return pl.BlockSpec((TM_PROJ, width), lambda i: (i, 0))

    return pl.pallas_call(
        _in_proj_kernel,
        grid=(ROWS // TM_PROJ,),
        in_specs=[
            pl.BlockSpec((TM_PROJ, D_MODEL), lambda i: (i, 0)),
            pl.BlockSpec((1, D_MODEL), lambda i: (0, 0)),
            pl.BlockSpec((D_MODEL, NAT_COLS + 2 * QKV_W), lambda i: (0, 0)),
        ],
        out_specs=[
            nat(QKV_W),
            pl.BlockSpec((None, d1, TM_PROJ // d1, QKV_W),
                         lambda i: (i // tiles_per_seq, 0, i % tiles_per_seq, 0)),
            pl.BlockSpec((None, d2, TM_PROJ // d2, QKV_W),
                         lambda i: (i // tiles_per_seq, 0, i % tiles_per_seq, 0)),
            nat(B_COLS),
            nat(QKV_W),
        ],
        out_shape=[
            jax.ShapeDtypeStruct((ROWS, QKV_W), BF16),
            jax.ShapeDtypeStruct((BATCH, d1, SEQ // d1, QKV_W), BF16),
            jax.ShapeDtypeStruct((BATCH, d2, SEQ // d2, QKV_W), BF16),
            jax.ShapeDtypeStruct((ROWS, B_COLS), BF16),
            jax.ShapeDtypeStruct((ROWS, QKV_W), BF16),
        ],
        scratch_shapes=[pltpu.VMEM((D_MODEL // LANES, TM_PROJ, LANES), F32)],
        compiler_params=_params("arbitrary"),
        name="in_proj",
    )(x2, gain, w_qkv)


def _low_lanes(rows):
    return lax.broadcasted_iota(jnp.int32, (rows, LANES), 1) < HEAD_DIM


def _stack_pair(q2):
    low = _low_lanes(BLK)
    zero = jnp.zeros_like(q2)
    return jnp.concatenate([jnp.where(low, q2, zero), jnp.where(low, zero, q2)], axis=0)


def _unstack_pair(o):
    return jnp.where(_low_lanes(BLK), o[:BLK], o[BLK:])


def _band_blocks(qs, kps, kcs, vps, vcs, bias_ps, bias_cs):
    sps = [_dot_nt(q, k) for q, k in zip(qs, kps)]
    scs = [_dot_nt(q, k) for q, k in zip(qs, kcs)]
    pps, pcs, stats = [], [], []
    for sp, sc, bias_p, bias_c in zip(sps, scs, bias_ps, bias_cs):
        sp = sp + bias_p
        sc = sc + bias_c
        m = jnp.max(jnp.maximum(sp, sc), axis=-1, keepdims=True)
        pp = jnp.exp(sp - m)
        pc = jnp.exp(sc - m)
        stats.append((m, jnp.sum(pp + pc, axis=-1, keepdims=True)))
        pps.append(pp.astype(BF16))
        pcs.append(pc.astype(BF16))
    nums = [_dot(pp, vp) + _dot(pc, vc) for pp, pc, vp, vc in zip(pps, pcs, vps, vcs)]
    return nums, stats


def _attn_a_kernel(a0_ref, a1_ref, a2_ref, bias_ref, o_ref, out_s, lse_s):
    n_pairs = A_HEADS // 2

    def cols(part, p):
        return slice(part * A_WIDTH + p * LANES, part * A_WIDTH + (p + 1) * LANES)

    for g, (ref, (_, d)) in enumerate(zip((a0_ref, a1_ref, a2_ref), A_GROUPS)):
        nb = N_BLK // d

        def step(n, carry, g=g, ref=ref, d=d, nb=nb):
            units, qs, kps, kcs, vps, vcs, bias_ps, bias_cs = [], [], [], [], [], [], [], []
            for u in range(BAND_UNROLL):
                blk = n * BAND_UNROLL + u
                r = blk // nb
                i = blk % nb
                row_c = pl.multiple_of(i * BLK, BLK)
                row_p = pl.multiple_of(jnp.maximum(i - 1, 0) * BLK, BLK)
                no_prev = jnp.where(i > 0, 0.0, NEG)
                if d == 1:
                    rows = pl.ds(row_c, BLK)
                else:
                    rows = pl.ds(i * (BLK * d) + r, BLK, stride=d)
                for p in range(n_pairs):
                    units.append((p, rows))
                    heads = slice(g * A_HEADS + 2 * p, g * A_HEADS + 2 * p + 2)
                    qs.append(_stack_pair(ref[r, pl.ds(row_c, BLK), cols(0, p)]))
                    kps.append(ref[r, pl.ds(row_p, BLK), cols(1, p)])
                    kcs.append(ref[r, pl.ds(row_c, BLK), cols(1, p)])
                    vps.append(ref[r, pl.ds(row_p, BLK), cols(2, p)])
                    vcs.append(ref[r, pl.ds(row_c, BLK), cols(2, p)])
                    bias_ps.append(bias_ref[heads, :, 0:BLK].reshape(2 * BLK, BLK) + no_prev)
                    bias_cs.append(bias_ref[heads, :, BLK:2 * BLK].reshape(2 * BLK, BLK))
            nums, stats = _band_blocks(qs, kps, kcs, vps, vcs, bias_ps, bias_cs)
            for (p, rows), num, (m, l) in zip(units, nums, stats):
                out_s[g, p, rows, :] = _unstack_pair(num / l)
                lse_s[g, p, rows, :] = _unstack_pair(jnp.broadcast_to(m + jnp.log(l), (2 * BLK, LANES)))
            return carry

        lax.fori_loop(0, N_BLK // BAND_UNROLL, step, 0)

    def combine(t, carry):
        rows = pl.ds(pl.multiple_of(t * BLK, BLK), BLK)
        for p in range(n_pairs):
            l0, l1, l2 = lse_s[0, p, rows, :], lse_s[1, p, rows, :], lse_s[2, p, rows, :]
            mx = jnp.maximum(jnp.maximum(l0, l1), l2)
            e0, e1, e2 = jnp.exp(l0 - mx), jnp.exp(l1 - mx), jnp.exp(l2 - mx)
            top = e0 * out_s[0, p, rows, :] + e1 * out_s[1, p, rows, :] + e2 * out_s[2, p, rows, :]
            o_ref[rows, p * LANES:(p + 1) * LANES] = (top / (e0 + e1 + e2)).astype(BF16)
        return carry

    lax.fori_loop(0, N_BLK, combine, 0)


def _attn_a(a0, a1, a2, bias):
    d1, d2 = A_GROUPS[1][1], A_GROUPS[2][1]
    return pl.pallas_call(
        _attn_a_kernel,
        grid=(BATCH,),
        in_specs=[
            pl.BlockSpec((None, 1, SEQ, QKV_W), lambda b: (b, 0, 0, 0)),
            pl.BlockSpec((None, d1, SEQ // d1, QKV_W), lambda b: (b, 0, 0, 0)),
            pl.BlockSpec((None, d2, SEQ // d2, QKV_W), lambda b: (b, 0, 0, 0)),
            pl.BlockSpec((N_A_GROUP_HEADS, BLK, 2 * BLK), lambda b: (0, 0, 0)),
        ],
        out_specs=pl.BlockSpec((SEQ, A_WIDTH), lambda b: (b, 0)),
        out_shape=jax.ShapeDtypeStruct((ROWS, A_WIDTH), BF16),
        scratch_shapes=[
            pltpu.VMEM((len(A_GROUPS), A_HEADS // 2, SEQ, LANES), F32),
            pltpu.VMEM((len(A_GROUPS), A_HEADS // 2, SEQ, LANES), F32),
        ],
        compiler_params=_params("arbitrary"),
        name="attn_dilated",
    )(a0.reshape(BATCH, 1, SEQ, QKV_W), a1, a2, bias)


def _attn_b_kernel(sink_ref, qkv_ref, bias_ref, o_ref):
    def kv_cols(part, kvh):
        start = B_WIDTH + (part * B_KV_HEADS + kvh) * LANES
        return slice(start, start + LANES)

    def step(n, carry):
        blocks, qs, kps, kcs, vps, vcs, bias_ps, bias_cs = [], [], [], [], [], [], [], []
        for u in range(BAND_UNROLL):
            i = n * BAND_UNROLL + u
            row_c = pl.multiple_of(i * BLK, BLK)
            row_p = pl.multiple_of(jnp.maximum(i - 1, 0) * BLK, BLK)
            no_prev = jnp.where(i > 0, 0.0, NEG)
            blocks.append(row_c)
            for kvh in range(B_KV_HEADS):
                heads = slice(kvh * B_GROUP, (kvh + 1) * B_GROUP)
                tiles = range(kvh * B_GROUP // 2, (kvh + 1) * B_GROUP // 2)
                qs.append(jnp.concatenate(
                    [_stack_pair(qkv_ref[pl.ds(row_c, BLK), t * LANES:(t + 1) * LANES]) for t in tiles],
                    axis=0))
                kps.append(qkv_ref[pl.ds(row_p, BLK), kv_cols(0, kvh)])
                kcs.append(qkv_ref[pl.ds(row_c, BLK), kv_cols(0, kvh)])
                vps.append(qkv_ref[pl.ds(row_p, BLK), kv_cols(1, kvh)])
                vcs.append(qkv_ref[pl.ds(row_c, BLK), kv_cols(1, kvh)])
                bias_ps.append(bias_ref[heads, :, 0:BLK].reshape(B_GROUP * BLK, BLK) + no_prev)
                bias_cs.append(bias_ref[heads, :, BLK:2 * BLK].reshape(B_GROUP * BLK, BLK))
        nums, stats = _band_blocks(qs, kps, kcs, vps, vcs, bias_ps, bias_cs)
        for u, row_c in enumerate(blocks):
            for kvh in range(B_KV_HEADS):
                num = nums[u * B_KV_HEADS + kvh]
                m, l = stats[u * B_KV_HEADS + kvh]
                scaled = []
                for gq in range(B_GROUP):
                    rows = slice(gq * BLK, (gq + 1) * BLK)
                    sink = sink_ref[0, kvh * B_GROUP + gq]
                    mx = jnp.maximum(m[rows], sink)
                    c = jnp.exp(m[rows] - mx)
                    scaled.append(num[rows] * (c / (l[rows] * c + jnp.exp(sink - mx))))
                for t in range(B_GROUP // 2):
                    tile = kvh * B_GROUP // 2 + t
                    o_ref[pl.ds(row_c, BLK), tile * LANES:(tile + 1) * LANES] = _unstack_pair(
                        jnp.concatenate(scaled[2 * t:2 * t + 2], axis=0)).astype(BF16)
        return carry

    lax.fori_loop(0, N_BLK // BAND_UNROLL, step, 0)


def _attn_b(qkv, bias, sinks):
    return pl.pallas_call(
        _attn_b_kernel,
        grid=(BATCH,),
        in_specs=[
            pl.BlockSpec(memory_space=pltpu.SMEM),
            pl.BlockSpec((SEQ, B_COLS), lambda b: (b, 0)),
            pl.BlockSpec((B_Q_HEADS, BLK, 2 * BLK), lambda b: (0, 0, 0)),
        ],
        out_specs=pl.BlockSpec((SEQ, B_WIDTH), lambda b: (b, 0)),
        out_shape=jax.ShapeDtypeStruct((ROWS, B_WIDTH), BF16),
        compiler_params=_params("arbitrary"),
        name="attn_window",
    )(sinks.reshape(1, B_Q_HEADS), qkv, bias)


def _suffix_matrix():
    j = np.arange(2 * BLK)[:, None] % BLK
    s = np.arange(2 * BLK)[None, :]
    return ((s >= BLK) | (j > s)).astype(np.float32)


def _stick_blocks(qs, ks, vs, suffix, laters, strict_lower):
    zs = [_dot_nt(q, k) for q, k in zip(qs, ks)]
    log_betas, splits = [], []
    for z in zs:
        neg_soft = jnp.log2(1.0 + jnp.exp2(jnp.abs(z) * -LOG2_E)) * -LN_2
        log_keep = neg_soft - jnp.maximum(z, 0.0)
        log_betas.append(z + log_keep)
        if laters is None:
            log_keep = jnp.where(strict_lower, log_keep, 0.0)
        hi = log_keep.astype(BF16)
        lo = (log_keep - hi.astype(F32)).astype(BF16)
        splits.append(jnp.concatenate([hi, lo], axis=1))
    sums = [_dot(s, suffix) for s in splits]
    ws, new_laters = [], []
    for n, (log_beta, s) in enumerate(zip(log_betas, sums)):
        if laters is None:
            w = jnp.where(strict_lower, jnp.exp(log_beta + s[:, :BLK]), 0.0)
            new_laters.append(s[:, BLK:])
        else:
            w = jnp.exp(log_beta + s[:, :BLK] + laters[n])
            new_laters.append(laters[n] + s[:, BLK:])
        ws.append(w.astype(BF16))
    return [_unstack_pair(_dot(w, v)) for w, v in zip(ws, vs)], new_laters


def _stick_schedule():
    q_blk, slot, k_blk = [], [], []
    for dist in range(1, N_BLK):
        group = [(i, i, i - dist) for i in range(dist, N_BLK)]
        group += [(0, N_BLK, 0)] * (-len(group) % STICK_UNROLL)
        for qi, sl, kj in group:
            q_blk.append(qi)
            slot.append(sl)
            k_blk.append(kj)
    return (np.asarray(q_blk, np.int32), np.asarray(slot, np.int32), np.asarray(k_blk, np.int32))


def _attn_c_kernel(qblk_ref, slot_ref, kblk_ref, qkv_ref, suffix_ref, o_ref, later_s, acc_s):
    n_pairs = C_HEADS // 2
    t_idx = lax.broadcasted_iota(jnp.int32, (2 * BLK, BLK), 0) % BLK
    s_idx = lax.broadcasted_iota(jnp.int32, (2 * BLK, BLK), 1)
    strict_lower = s_idx < t_idx

    def cols(part, p):
        return slice(part * C_WIDTH + p * LANES, part * C_WIDTH + (p + 1) * LANES)

    def rows_of(blk):
        return pl.ds(pl.multiple_of(blk * BLK, BLK), BLK)

    for p in range(n_pairs):
        later_s[p, N_BLK] = jnp.zeros((2 * BLK, BLK), F32)
        acc_s[p, N_BLK] = jnp.zeros((BLK, LANES), F32)

    def diag(n, carry):
        where = [(p, n * STICK_UNROLL + u) for u in range(STICK_UNROLL) for p in range(n_pairs)]
        parts, laters = _stick_blocks(
            [_stack_pair(qkv_ref[rows_of(i), cols(0, p)]) for p, i in where],
            [qkv_ref[rows_of(i), cols(1, p)] for p, i in where],
            [qkv_ref[rows_of(i), cols(2, p)] for p, i in where],
            suffix_ref[...], None, strict_lower)
        for (p, i), part, later in zip(where, parts, laters):
            acc_s[p, i] = part
            later_s[p, i] = later
        return carry

    lax.fori_loop(0, N_BLK // STICK_UNROLL, diag, 0)

    def off_diag(n, carry):
        where = []
        for u in range(STICK_UNROLL):
            e = n * STICK_UNROLL + u
            q_rows, k_rows, sl = rows_of(qblk_ref[e]), rows_of(kblk_ref[e]), slot_ref[e]
            where += [(p, sl, q_rows, k_rows) for p in range(n_pairs)]
        accs = [acc_s[p, sl] for p, sl, _, _ in where]
        parts, laters = _stick_blocks(
            [_stack_pair(qkv_ref[q_rows, cols(0, p)]) for p, _, q_rows, _ in where],
            [qkv_ref[k_rows, cols(1, p)] for p, _, _, k_rows in where],
            [qkv_ref[k_rows, cols(2, p)] for p, _, _, k_rows in where],
            suffix_ref[...], [later_s[p, sl] for p, sl, _, _ in where], None)
        for (p, sl, _, _), acc, part, later in zip(where, accs, parts, laters):
            acc_s[p, sl] = acc + part
            later_s[p, sl] = later
        return carry

    lax.fori_loop(0, qblk_ref.shape[0] // STICK_UNROLL, off_diag, 0)

    def emit(i, carry):
        for p in range(n_pairs):
            o_ref[rows_of(i), p * LANES:(p + 1) * LANES] = acc_s[p, i].astype(BF16)
        return carry

    lax.fori_loop(0, N_BLK, emit, 0)


def _attn_c(qkv):
    suffix = jnp.asarray(_suffix_matrix(), BF16)
    schedule = [jnp.asarray(t) for t in _stick_schedule()]
    smem = pl.BlockSpec(memory_space=pltpu.SMEM)
    return pl.pallas_call(
        _attn_c_kernel,
        grid=(BATCH,),
        in_specs=[
            smem, smem, smem,
            pl.BlockSpec((SEQ, QKV_W), lambda b: (b, 0)),
            pl.BlockSpec((2 * BLK, 2 * BLK), lambda b: (0, 0)),
        ],
        out_specs=pl.BlockSpec((SEQ, C_WIDTH), lambda b: (b, 0)),
        out_shape=jax.ShapeDtypeStruct((ROWS, C_WIDTH), BF16),
        scratch_shapes=[
            pltpu.VMEM((C_HEADS // 2, N_BLK + 1, 2 * BLK, BLK), F32),
            pltpu.VMEM((C_HEADS // 2, N_BLK + 1, BLK, LANES), F32),
        ],
        compiler_params=_params("arbitrary"),
        name="attn_stick",
    )(*schedule, qkv, suffix)


def _merge_kernel(x_ref, oa_ref, ob_ref, oc_ref, gpre_ref, wg_ref, bg_ref, wa_ref, wb_ref, wc_ref,
                  wo_ref, gpost_ref, y_ref):
    xv = x_ref[...]
    h = _rms(xv, gpre_ref[...]).astype(BF16)
    merged = None
    for n, (o_ref, w_ref) in enumerate(((oa_ref, wa_ref), (ob_ref, wb_ref), (oc_ref, wc_ref))):
        cols = slice(n * D_MODEL, (n + 1) * D_MODEL)
        gate = jax.nn.sigmoid(_dot(h, wg_ref[:, cols]) + bg_ref[:, cols])
        term = gate * _dot(o_ref[...], w_ref[...])
        merged = term if merged is None else merged + term
    y = _dot(merged.astype(BF16), wo_ref[...])
    y_ref[...] = xv + _rms(y, gpost_ref[...])


def _merge(x2, o_a, o_b, o_c, gpre, w_gate, b_gate, w_a, w_b, w_c, w_o, gpost):
    def rows(width):
        return pl.BlockSpec((TM_PROJ, width), lambda i: (i, 0))

    def whole(shape):
        return pl.BlockSpec(shape, lambda i: (0, 0))

    return pl.pallas_call(
        _merge_kernel,
        grid=(ROWS // TM_PROJ,),
        in_specs=[
            rows(D_MODEL), rows(A_WIDTH), rows(B_WIDTH), rows(C_WIDTH),
            whole((1, D_MODEL)), whole((D_MODEL, GATE_COLS)), whole((1, GATE_COLS)),
            whole((A_WIDTH, D_MODEL)), whole((B_WIDTH, D_MODEL)), whole((C_WIDTH, D_MODEL)),
            whole((D_MODEL, D_MODEL)), whole((1, D_MODEL)),
        ],
        out_specs=rows(D_MODEL),
        out_shape=jax.ShapeDtypeStruct((ROWS, D_MODEL), F32),
        compiler_params=_params("arbitrary"),
        name="merge_out_proj",
    )(x2, o_a, o_b, o_c, gpre, w_gate, b_gate, w_a, w_b, w_c, w_o, gpost)


def _gelu_tanh(x):
    return 0.5 * x * (1.0 + jnp.tanh(math.sqrt(2.0 / math.pi) * (x + 0.044715 * (x * x * x))))


def _ffn_kernel(x_ref, gpre_ref, wup_ref, cw_ref, cb_ref, wd_ref, gpost_ref, y_ref, tail_s):
    i = pl.program_id(0)
    n_chunks = D_FF // TN_FFN
    xv = x_ref[...]
    h = _rms(xv, gpre_ref[...]).astype(BF16)
    seq_start = (i % (SEQ // TM_FFN)) == 0

    def up(c):
        return [_dot(h, wup_ref[:, half * D_FF + c * TN_FFN:half * D_FF + (c + 1) * TN_FFN])
                for half in range(2)]

    def conv(u, c, half):
        cols = slice(half * D_FF + c * TN_FFN, half * D_FF + (c + 1) * TN_FFN)
        prev = jnp.where(seq_start, 0.0, tail_s[2 * c + half])
        tail_s[2 * c + half] = u[TM_FFN - HALO:, :]
        ext = jnp.concatenate([prev, u], axis=0)
        return (cw_ref[2:3, cols] * u
                + cw_ref[1:2, cols] * ext[HALO - 1:HALO - 1 + TM_FFN, :]
                + cw_ref[0:1, cols] * ext[HALO - 2:HALO - 2 + TM_FFN, :]
                + cb_ref[:, cols])

    acc = None
    ups = [up(c) for c in range(FFN_AHEAD)]
    for c in range(n_chunks):
        if c + FFN_AHEAD < n_chunks:
            ups.append(up(c + FFN_AHEAD))
        gate_u, val_u = ups.pop(0)
        act = (_gelu_tanh(conv(gate_u, c, 0)) * conv(val_u, c, 1)).astype(BF16)
        part = _dot(act, wd_ref[c * TN_FFN:(c + 1) * TN_FFN, :])
        acc = part if acc is None else acc + part
    y_ref[...] = xv + _rms(acc, gpost_ref[...])


def _ffn(x2, gpre, w_up, conv_w, conv_b, w_down, gpost):
    n_chunks = D_FF // TN_FFN

    def whole(shape):
        return pl.BlockSpec(shape, lambda i: (0, 0), pipeline_mode=pl.Buffered(1))

    return pl.pallas_call(
        _ffn_kernel,
        grid=(ROWS // TM_FFN,),
        in_specs=[
            pl.BlockSpec((TM_FFN, D_MODEL), lambda i: (i, 0)),
            whole((1, D_MODEL)),
            whole((D_MODEL, 2 * D_FF)),
            whole((CONV_WIDTH, 2 * D_FF)),
            whole((1, 2 * D_FF)),
            whole((D_FF, D_MODEL)),
            whole((1, D_MODEL)),
        ],
        out_specs=pl.BlockSpec((TM_FFN, D_MODEL), lambda i: (i, 0)),
        out_shape=jax.ShapeDtypeStruct((ROWS, D_MODEL), F32),
        scratch_shapes=[pltpu.VMEM((2 * n_chunks, HALO, TN_FFN), F32)],
        compiler_params=_params("arbitrary"),
        name="conv_ffn",
    )(x2, gpre, w_up, conv_w, conv_b, w_down, gpost)


def _qkv_weight(w_in):
    a = w_in[:, :A_QKV_COLS].reshape(D_MODEL, 3, len(A_GROUPS), A_WIDTH)
    a = a * jnp.asarray([SCALE, 1.0, 1.0], F32)[None, :, None, None]
    a = a.transpose(0, 2, 1, 3).reshape(D_MODEL, len(A_GROUPS), QKV_W)
    kv = w_in[:, OFF_B_KV:OFF_C].reshape(D_MODEL, 2 * B_KV_HEADS, 1, HEAD_DIM)
    kv = jnp.broadcast_to(kv, (D_MODEL, 2 * B_KV_HEADS, 2, HEAD_DIM)).reshape(D_MODEL, -1)
    b = jnp.concatenate([w_in[:, OFF_B_Q:OFF_B_KV] * SCALE, kv], axis=1)
    c = jnp.concatenate([w_in[:, OFF_C:OFF_C + C_WIDTH] * SCALE, w_in[:, OFF_C + C_WIDTH:OFF_GATE]],
                        axis=1)
    return jnp.concatenate([a[:, 0], b, c, a[:, 1], a[:, 2]], axis=1).astype(BF16)


def kernel(x, rel_bias, attn_pre_norm, w_in, b_gate, sinks, w_br_a, w_br_b, w_br_c, w_out,
           attn_post_norm, ffn_pre_norm, w_up, conv_w, conv_b, w_down, ffn_post_norm):
    assert x.shape == (BATCH, SEQ, D_MODEL) and x.dtype == F32
    bias = _band_bias(rel_bias)
    bias_a, bias_b = bias[:N_A_GROUP_HEADS], bias[N_A_GROUP_HEADS:]
    x2 = x.reshape(ROWS, D_MODEL)
    for layer in range(DEPTH):
        gpre = attn_pre_norm[layer].reshape(1, D_MODEL)
        a0, a1, a2, bq, cq = _in_proj(x2, gpre, _qkv_weight(w_in[layer]))
        o_a = _attn_a(a0, a1, a2, bias_a)
        o_b = _attn_b(bq, bias_b, sinks[layer])
        o_c = _attn_c(cq)
        x2 = _merge(x2, o_a, o_b, o_c, gpre,
                    w_in[layer][:, OFF_GATE:].astype(BF16), b_gate[layer].reshape(1, GATE_COLS),
                    w_br_a[layer].astype(BF16), w_br_b[layer].astype(BF16),
                    w_br_c[layer].astype(BF16), w_out[layer].astype(BF16),
                    attn_post_norm[layer].reshape(1, D_MODEL))
        x2 = _ffn(x2, ffn_pre_norm[layer].reshape(1, D_MODEL), w_up[layer].astype(BF16),
                  conv_w[layer], conv_b[layer].reshape(1, 2 * D_FF), w_down[layer].astype(BF16),
                  ffn_post_norm[layer].reshape(1, D_MODEL))
    return x2.reshape(BATCH, SEQ, D_MODEL)
```

```python
import math

import numpy as np
import jax
import jax.numpy as jnp
from jax import lax
from jax.experimental import pallas as pl
from jax.experimental.pallas import tpu as pltpu

D_MODEL = 1024
BATCH = 8
SEQ = 2048
DEPTH = 2
HEAD_DIM = 64
BLK = 128
A_GROUPS = ((128, 1), (512, 4), (2048, 16))
A_HEADS = 4
B_Q_HEADS = 8
B_KV_HEADS = 2
B_WINDOW = 128
C_HEADS = 4
N_BRANCH = 3
NUM_BUCKETS = 32
MAX_DISTANCE = 2048
D_FF = 4 * D_MODEL
CONV_WIDTH = 3
EPS = 1e-6
SCALE = HEAD_DIM ** -0.5

A_WIDTH = A_HEADS * HEAD_DIM
B_WIDTH = B_Q_HEADS * HEAD_DIM
C_WIDTH = C_HEADS * HEAD_DIM
B_GROUP = B_Q_HEADS // B_KV_HEADS
N_A_GROUP_HEADS = len(A_GROUPS) * A_HEADS
N_BIAS_HEADS = N_A_GROUP_HEADS + B_Q_HEADS
A_QKV_COLS = 3 * N_A_GROUP_HEADS * HEAD_DIM
OFF_B_Q = A_QKV_COLS
OFF_B_KV = OFF_B_Q + B_WIDTH
OFF_C = OFF_B_KV + 2 * B_KV_HEADS * HEAD_DIM
OFF_GATE = OFF_C + 3 * C_WIDTH
GATE_COLS = N_BRANCH * D_MODEL

ROWS = BATCH * SEQ
N_BLK = SEQ // BLK
QKV_W = 3 * A_WIDTH
LANES = 128
B_COLS = B_WIDTH + 2 * B_KV_HEADS * LANES
NAT_COLS = QKV_W + B_COLS + QKV_W
NEG = -1e30
STICK_DEAD = -104.0
LOG2_E = 1.4426950408889634
LN_2 = 0.6931471805599453

V7X_VMEM_BYTES = 64 * 1024 * 1024
VMEM_LIMIT = V7X_VMEM_BYTES * 7 // 8

TM_PROJ = 512
TM_FFN = 512
TN_FFN = 512
FFN_AHEAD = 2
HALO = 8
BAND_UNROLL = 2
STICK_UNROLL = 4

BF16 = jnp.bfloat16
F32 = jnp.float32


def _params(*sem):
    return pltpu.CompilerParams(dimension_semantics=sem, vmem_limit_bytes=VMEM_LIMIT)


def _dot(a, b):
    return jnp.dot(a, b, preferred_element_type=F32)


def _dot_nt(a, b):
    return lax.dot_general(a, b, (((1,), (1,)), ((), ())), preferred_element_type=F32)


def _rms(xv, gain):
    y = xv * lax.rsqrt(jnp.mean(xv * xv, axis=-1, keepdims=True) + EPS)
    return y * gain


def _bucket_constants():
    a = np.arange(BLK)[:, None]
    b = np.arange(2 * BLK)[None, :]
    dist = np.maximum(a + BLK - b, 0)
    max_exact = NUM_BUCKETS // 2
    out = []
    for _, d in A_GROUPS:
        n = dist * d
        nf = np.maximum(n, 1).astype(np.float64)
        large = max_exact + (np.log(nf / max_exact) / math.log(MAX_DISTANCE / max_exact)
                             * (NUM_BUCKETS - max_exact)).astype(np.int64)
        large = np.minimum(large, NUM_BUCKETS - 1)
        out.append(np.where(n < max_exact, n, large))
    return np.stack(out).astype(np.int32)


def _bias_kernel(tab_ref, bucket_ref, out_ref):
    hh = pl.program_id(0)
    a = lax.broadcasted_iota(jnp.int32, (BLK, 2 * BLK), 0)
    b = lax.broadcasted_iota(jnp.int32, (BLK, 2 * BLK), 1)
    dist = a + BLK - b
    max_dist = jnp.where(hh < N_A_GROUP_HEADS, BLK, B_WINDOW - 1)
    valid = (dist >= 0) & (dist <= max_dist)
    bk = bucket_ref[...]
    acc = jnp.zeros((BLK, 2 * BLK), F32)
    for k in range(NUM_BUCKETS):
        acc = jnp.where(bk == k, tab_ref[k, hh], acc)
    out_ref[...] = jnp.where(valid, acc, NEG)


def _band_bias(rel_bias):
    buckets = jnp.asarray(_bucket_constants())
    return pl.pallas_call(
        _bias_kernel,
        grid=(N_BIAS_HEADS,),
        in_specs=[
            pl.BlockSpec(memory_space=pltpu.SMEM),
            pl.BlockSpec((None, BLK, 2 * BLK),
                         lambda h: (jnp.where(h < N_A_GROUP_HEADS, h // A_HEADS, 0), 0, 0)),
        ],
        out_specs=pl.BlockSpec((None, BLK, 2 * BLK), lambda h: (h, 0, 0)),
        out_shape=jax.ShapeDtypeStruct((N_BIAS_HEADS, BLK, 2 * BLK), F32),
        compiler_params=_params("arbitrary"),
        name="band_bias",
    )(rel_bias, buckets)


def _in_proj_kernel(x_ref, g_ref, w_ref, a0_ref, a1_ref, a2_ref, b_ref, c_ref, xs_s):
    gain = g_ref[...]
    n_lane_tiles = D_MODEL // LANES

    def proj(xv, col0, width):
        h = _rms(xv, gain).astype(BF16)
        return _dot(h, w_ref[:, col0:col0 + width]).astype(BF16)

    xv = x_ref[...]
    nat = proj(xv, 0, NAT_COLS)
    a0_ref[...] = nat[:, :QKV_W]
    b_ref[...] = nat[:, QKV_W:QKV_W + B_COLS]
    c_ref[...] = nat[:, QKV_W + B_COLS:]
    for c in range(n_lane_tiles):
        xs_s[c] = xv[:, c * LANES:(c + 1) * LANES]
    for ref, (_, d), col0 in ((a1_ref, A_GROUPS[1], NAT_COLS), (a2_ref, A_GROUPS[2], NAT_COLS + QKV_W)):
        per = TM_PROJ // d
        xs = jnp.concatenate(
            [jnp.concatenate([xs_s[c, pl.ds(r, per, stride=d), :] for c in range(n_lane_tiles)], axis=1)
             for r in range(d)], axis=0)
        ref[...] = proj(xs, col0, QKV_W).reshape(d, per, QKV_W)


def _layer_spec(layer, shape):
    return pl.BlockSpec((None,) + shape, lambda i: (layer, 0, 0))


def _in_proj(layer, x2, gain, w_qkv):
    tiles_per_seq = SEQ // TM_PROJ
    d1, d2 = A_GROUPS[1][1], A_GROUPS[2][1]

    def nat(width):
        return pl.BlockSpec((TM_PROJ, width), lambda i: (i, 0))

    return pl.pallas_call(
        _in_proj_kernel,
        grid=(ROWS // TM_PROJ,),
        in_specs=[
            pl.BlockSpec((TM_PROJ, D_MODEL), lambda i: (i, 0)),
            _layer_spec(layer, (1, D_MODEL)),
            _layer_spec(layer, (D_MODEL, NAT_COLS + 2 * QKV_W)),
        ],
        out_specs=[
            nat(QKV_W),
            pl.BlockSpec((None, d1, TM_PROJ // d1, QKV_W),
                         lambda i: (i // tiles_per_seq, 0, i % tiles_per_seq, 0)),
            pl.BlockSpec((None, d2, TM_PROJ // d2, QKV_W),
                         lambda i: (i // tiles_per_seq, 0, i % tiles_per_seq, 0)),
            nat(B_COLS),
            nat(QKV_W),
        ],
        out_shape=[
            jax.ShapeDtypeStruct((ROWS, QKV_W), BF16),
            jax.ShapeDtypeStruct((BATCH, d1, SEQ // d1, QKV_W), BF16),
            jax.ShapeDtypeStruct((BATCH, d2, SEQ // d2, QKV_W), BF16),
            jax.ShapeDtypeStruct((ROWS, B_COLS), BF16),
            jax.ShapeDtypeStruct((ROWS, QKV_W), BF16),
        ],
        scratch_shapes=[pltpu.VMEM((D_MODEL // LANES, TM_PROJ, LANES), F32)],
        compiler_params=_params("arbitrary"),
        name="in_proj",
    )(x2, gain, w_qkv)


def _low_lanes(rows):
    return lax.broadcasted_iota(jnp.int32, (rows, LANES), 1) < HEAD_DIM


def _stack_pair(q2):
    low = _low_lanes(BLK)
    zero = jnp.zeros_like(q2)
    return jnp.concatenate([jnp.where(low, q2, zero), jnp.where(low, zero, q2)], axis=0)


def _unstack_pair(o):
    return jnp.where(_low_lanes(BLK), o[:BLK], o[BLK:])


def _band_blocks(qs, kps, kcs, vps, vcs, bias_ps, bias_cs):
    sps = [_dot_nt(q, k) for q, k in zip(qs, kps)]
    scs = [_dot_nt(q, k) for q, k in zip(qs, kcs)]
    pps, pcs, stats = [], [], []
    for sp, sc, bias_p, bias_c in zip(sps, scs, bias_ps, bias_cs):
        sp = sp + bias_p
        sc = sc + bias_c
        m = jnp.max(jnp.maximum(sp, sc), axis=-1, keepdims=True)
        pp = jnp.exp(sp - m)
        pc = jnp.exp(sc - m)
        stats.append((m, jnp.sum(pp + pc, axis=-1, keepdims=True)))
        pps.append(pp.astype(BF16))
        pcs.append(pc.astype(BF16))
    nums = [_dot(pp, vp) + _dot(pc, vc) for pp, pc, vp, vc in zip(pps, pcs, vps, vcs)]
    return nums, stats


def _attn_a_kernel(a0_ref, a1_ref, a2_ref, bias_ref, o_ref, out_s, lse_s):
    n_pairs = A_HEADS // 2

    def cols(part, p):
        return slice(part * A_WIDTH + p * LANES, part * A_WIDTH + (p + 1) * LANES)

    for g, (ref, (_, d)) in enumerate(zip((a0_ref, a1_ref, a2_ref), A_GROUPS)):
        nb = N_BLK // d

        def step(n, carry, g=g, ref=ref, d=d, nb=nb):
            units, qs, kps, kcs, vps, vcs, bias_ps, bias_cs = [], [], [], [], [], [], [], []
            for u in range(BAND_UNROLL):
                blk = n * BAND_UNROLL + u
                r = blk // nb
                i = blk % nb
                row_c = pl.multiple_of(i * BLK, BLK)
                row_p = pl.multiple_of(jnp.maximum(i - 1, 0) * BLK, BLK)
                no_prev = jnp.where(i > 0, 0.0, NEG)
                if d == 1:
                    rows = pl.ds(row_c, BLK)
                else:
                    rows = pl.ds(i * (BLK * d) + r, BLK, stride=d)
                for p in range(n_pairs):
                    units.append((p, rows))
                    heads = slice(g * A_HEADS + 2 * p, g * A_HEADS + 2 * p + 2)
                    qs.append(_stack_pair(ref[r, pl.ds(row_c, BLK), cols(0, p)]))
                    kps.append(ref[r, pl.ds(row_p, BLK), cols(1, p)])
                    kcs.append(ref[r, pl.ds(row_c, BLK), cols(1, p)])
                    vps.append(ref[r, pl.ds(row_p, BLK), cols(2, p)])
                    vcs.append(ref[r, pl.ds(row_c, BLK), cols(2, p)])
                    bias_ps.append(bias_ref[heads, :, 0:BLK].reshape(2 * BLK, BLK) + no_prev)
                    bias_cs.append(bias_ref[heads, :, BLK:2 * BLK].reshape(2 * BLK, BLK))
            nums, stats = _band_blocks(qs, kps, kcs, vps, vcs, bias_ps, bias_cs)
            for (p, rows), num, (m, l) in zip(units, nums, stats):
                out_s[g, p, rows, :] = _unstack_pair(num / l)
                lse_s[g, p, rows, :] = _unstack_pair(jnp.broadcast_to(m + jnp.log(l), (2 * BLK, LANES)))
            return carry

        lax.fori_loop(0, N_BLK // BAND_UNROLL, step, 0)

    def combine(t, carry):
        rows = pl.ds(pl.multiple_of(t * BLK, BLK), BLK)
        for p in range(n_pairs):
            l0, l1, l2 = lse_s[0, p, rows, :], lse_s[1, p, rows, :], lse_s[2, p, rows, :]
            mx = jnp.maximum(jnp.maximum(l0, l1), l2)
            e0, e1, e2 = jnp.exp(l0 - mx), jnp.exp(l1 - mx), jnp.exp(l2 - mx)
            top = e0 * out_s[0, p, rows, :] + e1 * out_s[1, p, rows, :] + e2 * out_s[2, p, rows, :]
            o_ref[rows, p * LANES:(p + 1) * LANES] = (top / (e0 + e1 + e2)).astype(BF16)
        return carry

    lax.fori_loop(0, N_BLK, combine, 0)


def _attn_a(a0, a1, a2, bias):
    d1, d2 = A_GROUPS[1][1], A_GROUPS[2][1]
    return pl.pallas_call(
        _attn_a_kernel,
        grid=(BATCH,),
        in_specs=[
            pl.BlockSpec((None, 1, SEQ, QKV_W), lambda b: (b, 0, 0, 0)),
            pl.BlockSpec((None, d1, SEQ // d1, QKV_W), lambda b: (b, 0, 0, 0)),
            pl.BlockSpec((None, d2, SEQ // d2, QKV_W), lambda b: (b, 0, 0, 0)),
            pl.BlockSpec((N_A_GROUP_HEADS, BLK, 2 * BLK), lambda b: (0, 0, 0)),
        ],
        out_specs=pl.BlockSpec((SEQ, A_WIDTH), lambda b: (b, 0)),
        out_shape=jax.ShapeDtypeStruct((ROWS, A_WIDTH), BF16),
        scratch_shapes=[
            pltpu.VMEM((len(A_GROUPS), A_HEADS // 2, SEQ, LANES), F32),
            pltpu.VMEM((len(A_GROUPS), A_HEADS // 2, SEQ, LANES), F32),
        ],
        compiler_params=_params("arbitrary"),
        name="attn_dilated",
    )(a0.reshape(BATCH, 1, SEQ, QKV_W), a1, a2, bias)


def _attn_b_kernel(sink_ref, qkv_ref, bias_ref, o_ref):
    def kv_cols(part, kvh):
        start = B_WIDTH + (part * B_KV_HEADS + kvh) * LANES
        return slice(start, start + LANES)

    def step(n, carry):
        blocks, qs, kps, kcs, vps, vcs, bias_ps, bias_cs = [], [], [], [], [], [], [], []
        for u in range(BAND_UNROLL):
            i = n * BAND_UNROLL + u
            row_c = pl.multiple_of(i * BLK, BLK)
            row_p = pl.multiple_of(jnp.maximum(i - 1, 0) * BLK, BLK)
            no_prev = jnp.where(i > 0, 0.0, NEG)
            blocks.append(row_c)
            for kvh in range(B_KV_HEADS):
                heads = slice(kvh * B_GROUP, (kvh + 1) * B_GROUP)
                tiles = range(kvh * B_GROUP // 2, (kvh + 1) * B_GROUP // 2)
                qs.append(jnp.concatenate(
                    [_stack_pair(qkv_ref[pl.ds(row_c, BLK), t * LANES:(t + 1) * LANES]) for t in tiles],
                    axis=0))
                kps.append(qkv_ref[pl.ds(row_p, BLK), kv_cols(0, kvh)])
                kcs.append(qkv_ref[pl.ds(row_c, BLK), kv_cols(0, kvh)])
                vps.append(qkv_ref[pl.ds(row_p, BLK), kv_cols(1, kvh)])
                vcs.append(qkv_ref[pl.ds(row_c, BLK), kv_cols(1, kvh)])
                bias_ps.append(bias_ref[heads, :, 0:BLK].reshape(B_GROUP * BLK, BLK) + no_prev)
                bias_cs.append(bias_ref[heads, :, BLK:2 * BLK].reshape(B_GROUP * BLK, BLK))
        nums, stats = _band_blocks(qs, kps, kcs, vps, vcs, bias_ps, bias_cs)
        for u, row_c in enumerate(blocks):
            for kvh in range(B_KV_HEADS):
                num = nums[u * B_KV_HEADS + kvh]
                m, l = stats[u * B_KV_HEADS + kvh]
                scaled = []
                for gq in range(B_GROUP):
                    rows = slice(gq * BLK, (gq + 1) * BLK)
                    sink = sink_ref[0, kvh * B_GROUP + gq]
                    mx = jnp.maximum(m[rows], sink)
                    c = jnp.exp(m[rows] - mx)
                    scaled.append(num[rows] * (c / (l[rows] * c + jnp.exp(sink - mx))))
                for t in range(B_GROUP // 2):
                    tile = kvh * B_GROUP // 2 + t
                    o_ref[pl.ds(row_c, BLK), tile * LANES:(tile + 1) * LANES] = _unstack_pair(
                        jnp.concatenate(scaled[2 * t:2 * t + 2], axis=0)).astype(BF16)
        return carry

    lax.fori_loop(0, N_BLK // BAND_UNROLL, step, 0)


def _attn_b(qkv, bias, sinks):
    return pl.pallas_call(
        _attn_b_kernel,
        grid=(BATCH,),
        in_specs=[
            pl.BlockSpec(memory_space=pltpu.SMEM),
            pl.BlockSpec((SEQ, B_COLS), lambda b: (b, 0)),
            pl.BlockSpec((B_Q_HEADS, BLK, 2 * BLK), lambda b: (0, 0, 0)),
        ],
        out_specs=pl.BlockSpec((SEQ, B_WIDTH), lambda b: (b, 0)),
        out_shape=jax.ShapeDtypeStruct((ROWS, B_WIDTH), BF16),
        compiler_params=_params("arbitrary"),
        name="attn_window",
    )(sinks.reshape(1, B_Q_HEADS), qkv, bias)


def _suffix_matrix():
    j = np.arange(2 * BLK)[:, None] % BLK
    s = np.arange(2 * BLK)[None, :]
    return ((s >= BLK) | (j > s)).astype(np.float32)


def _stick_blocks(qs, ks, vs, suffix, laters, strict_lower):
    zs = [_dot_nt(q, k) for q, k in zip(qs, ks)]
    log_betas, splits = [], []
    for z in zs:
        neg_soft = jnp.log2(1.0 + jnp.exp2(jnp.abs(z) * -LOG2_E)) * -LN_2
        log_keep = neg_soft - jnp.maximum(z, 0.0)
        log_betas.append(z + log_keep)
        if laters is None:
            log_keep = jnp.where(strict_lower, log_keep, 0.0)
        hi = log_keep.astype(BF16)
        lo = (log_keep - hi.astype(F32)).astype(BF16)
        splits.append(jnp.concatenate([hi, lo], axis=1))
    sums = [_dot(s, suffix) for s in splits]
    ws, new_laters = [], []
    for n, (log_beta, s) in enumerate(zip(log_betas, sums)):
        if laters is None:
            w = jnp.where(strict_lower, jnp.exp(log_beta + s[:, :BLK]), 0.0)
            new_laters.append(s[:, BLK:])
        else:
            w = jnp.exp(log_beta + s[:, :BLK] + laters[n])
            new_laters.append(laters[n] + s[:, BLK:])
        ws.append(w.astype(BF16))
    return [_unstack_pair(_dot(w, v)) for w, v in zip(ws, vs)], new_laters


def _stick_schedule():
    q_blk, slot, k_blk, first_step = [], [], [], [0, 0]
    for dist in range(1, N_BLK):
        group = [(i, i, i - dist) for i in range(dist, N_BLK)]
        group += [(0, N_BLK, 0)] * (-len(group) % STICK_UNROLL)
        for qi, sl, kj in group:
            q_blk.append(qi)
            slot.append(sl)
            k_blk.append(kj)
        first_step.append(len(q_blk) // STICK_UNROLL)
    return tuple(np.asarray(t, np.int32) for t in (q_blk, slot, k_blk, first_step))


def _attn_c_kernel(qblk_ref, slot_ref, kblk_ref, step_ref, qkv_ref, suffix_ref, o_ref, later_s, acc_s):
    n_pairs = C_HEADS // 2
    t_idx = lax.broadcasted_iota(jnp.int32, (2 * BLK, BLK), 0) % BLK
    s_idx = lax.broadcasted_iota(jnp.int32, (2 * BLK, BLK), 1)
    strict_lower = s_idx < t_idx

    def cols(part, p):
        return slice(part * C_WIDTH + p * LANES, part * C_WIDTH + (p + 1) * LANES)

    def rows_of(blk):
        return pl.ds(pl.multiple_of(blk * BLK, BLK), BLK)

    for p in range(n_pairs):
        later_s[p, N_BLK] = jnp.zeros((2 * BLK, BLK), F32)
        acc_s[p, N_BLK] = jnp.zeros((BLK, LANES), F32)

    def diag(n, carry):
        where = [(p, n * STICK_UNROLL + u) for u in range(STICK_UNROLL) for p in range(n_pairs)]
        parts, laters = _stick_blocks(
            [_stack_pair(qkv_ref[rows_of(i), cols(0, p)]) for p, i in where],
            [qkv_ref[rows_of(i), cols(1, p)] for p, i in where],
            [qkv_ref[rows_of(i), cols(2, p)] for p, i in where],
            suffix_ref[...], None, strict_lower)
        for (p, i), part, later in zip(where, parts, laters):
            acc_s[p, i] = part
            later_s[p, i] = later
        return carry

    lax.fori_loop(0, N_BLK // STICK_UNROLL, diag, 0)

    def off_diag(n, carry):
        where = []
        for u in range(STICK_UNROLL):
            e = n * STICK_UNROLL + u
            q_rows, k_rows, sl = rows_of(qblk_ref[e]), rows_of(kblk_ref[e]), slot_ref[e]
            where += [(p, sl, q_rows, k_rows) for p in range(n_pairs)]
        accs = [acc_s[p, sl] for p, sl, _, _ in where]
        parts, laters = _stick_blocks(
            [_stack_pair(qkv_ref[q_rows, cols(0, p)]) for p, _, q_rows, _ in where],
            [qkv_ref[k_rows, cols(1, p)] for p, _, _, k_rows in where],
            [qkv_ref[k_rows, cols(2, p)] for p, _, _, k_rows in where],
            suffix_ref[...], [later_s[p, sl] for p, sl, _, _ in where], None)
        for (p, sl, _, _), acc, part, later in zip(where, accs, parts, laters):
            acc_s[p, sl] = acc + part
            later_s[p, sl] = later
        return carry

    def group(state):
        dist, _ = state
        lax.fori_loop(step_ref[dist], step_ref[dist + 1], off_diag, 0)

        def slot_max(i, mx):
            for p in range(n_pairs):
                mx = jnp.maximum(mx, later_s[p, i])
            return mx

        mx = lax.fori_loop(dist + 1, N_BLK, slot_max, jnp.full((2 * BLK, BLK), -jnp.inf, F32))
        return dist + 1, jnp.logical_not(jnp.max(mx) < STICK_DEAD)

    lax.while_loop(lambda state: (state[0] < N_BLK) & state[1], group, (jnp.int32(1), jnp.bool_(True)))

    def emit(i, carry):
        for p in range(n_pairs):
            o_ref[rows_of(i), p * LANES:(p + 1) * LANES] = acc_s[p, i].astype(BF16)
        return carry

    lax.fori_loop(0, N_BLK, emit, 0)


def _attn_c(qkv):
    suffix = jnp.asarray(_suffix_matrix(), BF16)
    schedule = [jnp.asarray(t) for t in _stick_schedule()]
    smem = pl.BlockSpec(memory_space=pltpu.SMEM)
    return pl.pallas_call(
        _attn_c_kernel,
        grid=(BATCH,),
        in_specs=[
            smem, smem, smem, smem,
            pl.BlockSpec((SEQ, QKV_W), lambda b: (b, 0)),
            pl.BlockSpec((2 * BLK, 2 * BLK), lambda b: (0, 0)),
        ],
        out_specs=pl.BlockSpec((SEQ, C_WIDTH), lambda b: (b, 0)),
        out_shape=jax.ShapeDtypeStruct((ROWS, C_WIDTH), BF16),
        scratch_shapes=[
            pltpu.VMEM((C_HEADS // 2, N_BLK + 1, 2 * BLK, BLK), F32),
            pltpu.VMEM((C_HEADS // 2, N_BLK + 1, BLK, LANES), F32),
        ],
        compiler_params=_params("arbitrary"),
        name="attn_stick",
    )(*schedule, qkv, suffix)


def _merge_kernel(x_ref, oa_ref, ob_ref, oc_ref, gpre_ref, wg_ref, bg_ref, wa_ref, wb_ref, wc_ref,
                  wo_ref, gpost_ref, y_ref):
    xv = x_ref[...]
    h = _rms(xv, gpre_ref[...]).astype(BF16)
    merged = None
    for n, (o_ref, w_ref) in enumerate(((oa_ref, wa_ref), (ob_ref, wb_ref), (oc_ref, wc_ref))):
        cols = slice(n * D_MODEL, (n + 1) * D_MODEL)
        gate = jax.nn.sigmoid(_dot(h, wg_ref[:, cols]) + bg_ref[:, cols])
        term = gate * _dot(o_ref[...], w_ref[...])
        merged = term if merged is None else merged + term
    y = _dot(merged.astype(BF16), wo_ref[...])
    y_ref[...] = xv + _rms(y, gpost_ref[...])


def _merge(layer, x2, o_a, o_b, o_c, gpre, w_gate, b_gate, w_a, w_b, w_c, w_o, gpost):
    def rows(width):
        return pl.BlockSpec((TM_PROJ, width), lambda i: (i, 0))

    def whole(shape):
        return _layer_spec(layer, shape)

    return pl.pallas_call(
        _merge_kernel,
        grid=(ROWS // TM_PROJ,),
        in_specs=[
            rows(D_MODEL), rows(A_WIDTH), rows(B_WIDTH), rows(C_WIDTH),
            whole((1, D_MODEL)), whole((D_MODEL, GATE_COLS)), whole((1, GATE_COLS)),
            whole((A_WIDTH, D_MODEL)), whole((B_WIDTH, D_MODEL)), whole((C_WIDTH, D_MODEL)),
            whole((D_MODEL, D_MODEL)), whole((1, D_MODEL)),
        ],
        out_specs=rows(D_MODEL),
        out_shape=jax.ShapeDtypeStruct((ROWS, D_MODEL), F32),
        compiler_params=_params("arbitrary"),
        name="merge_out_proj",
    )(x2, o_a, o_b, o_c, gpre, w_gate, b_gate, w_a, w_b, w_c, w_o, gpost)


def _gelu_tanh(x):
    return 0.5 * x * (1.0 + jnp.tanh(math.sqrt(2.0 / math.pi) * (x + 0.044715 * (x * x * x))))


def _ffn_kernel(x_ref, gpre_ref, wup_ref, cw_ref, cb_ref, wd_ref, gpost_ref, y_ref, tail_s):
    i = pl.program_id(0)
    n_chunks = D_FF // TN_FFN
    xv = x_ref[...]
    h = _rms(xv, gpre_ref[...]).astype(BF16)
    seq_start = (i % (SEQ // TM_FFN)) == 0

    def up(c):
        return [_dot(h, wup_ref[:, half * D_FF + c * TN_FFN:half * D_FF + (c + 1) * TN_FFN])
                for half in range(2)]

    def conv(u, c, half):
        cols = slice(half * D_FF + c * TN_FFN, half * D_FF + (c + 1) * TN_FFN)
        prev = jnp.where(seq_start, 0.0, tail_s[2 * c + half])
        tail_s[2 * c + half] = u[TM_FFN - HALO:, :]
        ext = jnp.concatenate([prev, u], axis=0)
        return (cw_ref[2:3, cols] * u
                + cw_ref[1:2, cols] * ext[HALO - 1:HALO - 1 + TM_FFN, :]
                + cw_ref[0:1, cols] * ext[HALO - 2:HALO - 2 + TM_FFN, :]
                + cb_ref[:, cols])

    acc = None
    ups = [up(c) for c in range(FFN_AHEAD)]
    for c in range(n_chunks):
        if c + FFN_AHEAD < n_chunks:
            ups.append(up(c + FFN_AHEAD))
        gate_u, val_u = ups.pop(0)
        act = (_gelu_tanh(conv(gate_u, c, 0)) * conv(val_u, c, 1)).astype(BF16)
        part = _dot(act, wd_ref[c * TN_FFN:(c + 1) * TN_FFN, :])
        acc = part if acc is None else acc + part
    y_ref[...] = xv + _rms(acc, gpost_ref[...])


def _ffn(layer, x2, gpre, w_up, conv_w, conv_b, w_down, gpost):
    n_chunks = D_FF // TN_FFN

    def whole(shape):
        return pl.BlockSpec((None,) + shape, lambda i: (layer, 0, 0), pipeline_mode=pl.Buffered(1))

    return pl.pallas_call(
        _ffn_kernel,
        grid=(ROWS // TM_FFN,),
        in_specs=[
            pl.BlockSpec((TM_FFN, D_MODEL), lambda i: (i, 0)),
            whole((1, D_MODEL)),
            whole((D_MODEL, 2 * D_FF)),
            whole((CONV_WIDTH, 2 * D_FF)),
            whole((1, 2 * D_FF)),
            whole((D_FF, D_MODEL)),
            whole((1, D_MODEL)),
        ],
        out_specs=pl.BlockSpec((TM_FFN, D_MODEL), lambda i: (i, 0)),
        out_shape=jax.ShapeDtypeStruct((ROWS, D_MODEL), F32),
        scratch_shapes=[pltpu.VMEM((2 * n_chunks, HALO, TN_FFN), F32)],
        compiler_params=_params("arbitrary"),
        name="conv_ffn",
    )(x2, gpre, w_up, conv_w, conv_b, w_down, gpost)


def _qkv_weight(w_in):
    rows = w_in.shape[:-1]
    a = w_in[..., :A_QKV_COLS].reshape(rows + (3, len(A_GROUPS), A_WIDTH))
    a = a * jnp.asarray([SCALE, 1.0, 1.0], F32)[:, None, None]
    a = jnp.swapaxes(a, -3, -2).reshape(rows + (len(A_GROUPS), QKV_W))
    kv = w_in[..., OFF_B_KV:OFF_C].reshape(rows + (2 * B_KV_HEADS, 1, HEAD_DIM))
    kv = jnp.broadcast_to(kv, rows + (2 * B_KV_HEADS, 2, HEAD_DIM)).reshape(rows + (-1,))
    b = jnp.concatenate([w_in[..., OFF_B_Q:OFF_B_KV] * SCALE, kv], axis=-1)
    c = jnp.concatenate([w_in[..., OFF_C:OFF_C + C_WIDTH] * SCALE, w_in[..., OFF_C + C_WIDTH:OFF_GATE]],
                        axis=-1)
    return jnp.concatenate([a[..., 0, :], b, c, a[..., 1, :], a[..., 2, :]], axis=-1).astype(BF16)


def kernel(x, rel_bias, attn_pre_norm, w_in, b_gate, sinks, w_br_a, w_br_b, w_br_c, w_out,
           attn_post_norm, ffn_pre_norm, w_up, conv_w, conv_b, w_down, ffn_post_norm):
    assert x.shape == (BATCH, SEQ, D_MODEL) and x.dtype == F32
    bias = _band_bias(rel_bias)
    bias_a, bias_b = bias[:N_A_GROUP_HEADS], bias[N_A_GROUP_HEADS:]
    x2 = x.reshape(ROWS, D_MODEL)

    def row_param(p):
        return p.reshape(DEPTH, 1, -1)

    w_qkv = _qkv_weight(w_in)
    w_gate = w_in[..., OFF_GATE:].astype(BF16)
    w_a, w_b, w_c, w_o = (w.astype(BF16) for w in (w_br_a, w_br_b, w_br_c, w_out))
    w_up16, w_down16 = w_up.astype(BF16), w_down.astype(BF16)
    attn_pre, attn_post, ffn_pre, ffn_post, gate_b, conv_bias = (
        row_param(p) for p in (attn_pre_norm, attn_post_norm, ffn_pre_norm, ffn_post_norm, b_gate, conv_b))
    for layer in range(DEPTH):
        a0, a1, a2, bq, cq = _in_proj(layer, x2, attn_pre, w_qkv)
        o_a = _attn_a(a0, a1, a2, bias_a)
        o_b = _attn_b(bq, bias_b, sinks[layer])
        o_c = _attn_c(cq)
        x2 = _merge(layer, x2, o_a, o_b, o_c, attn_pre, w_gate, gate_b, w_a, w_b, w_c, w_o, attn_post)
        x2 = _ffn(layer, x2, ffn_pre, w_up16, conv_w, conv_bias, w_down16, ffn_post)
    return x2.reshape(BATCH, SEQ, D_MODEL)
```

```python
import math

import numpy as np
import jax
import jax.numpy as jnp
from jax import lax
from jax.experimental import pallas as pl
from jax.experimental.pallas import tpu as pltpu

D_MODEL = 1024
BATCH = 8
SEQ = 2048
DEPTH = 2
HEAD_DIM = 64
BLK = 128
A_GROUPS = ((128, 1), (512, 4), (2048, 16))
A_HEADS = 4
B_Q_HEADS = 8
B_KV_HEADS = 2
B_WINDOW = 128
C_HEADS = 4
N_BRANCH = 3
NUM_BUCKETS = 32
MAX_DISTANCE = 2048
D_FF = 4 * D_MODEL
CONV_WIDTH = 3
EPS = 1e-6
SCALE = HEAD_DIM ** -0.5

A_WIDTH = A_HEADS * HEAD_DIM
B_WIDTH = B_Q_HEADS * HEAD_DIM
C_WIDTH = C_HEADS * HEAD_DIM
B_GROUP = B_Q_HEADS // B_KV_HEADS
N_A_GROUP_HEADS = len(A_GROUPS) * A_HEADS
N_BIAS_HEADS = N_A_GROUP_HEADS + B_Q_HEADS
A_QKV_COLS = 3 * N_A_GROUP_HEADS * HEAD_DIM
OFF_B_Q = A_QKV_COLS
OFF_B_KV = OFF_B_Q + B_WIDTH
OFF_C = OFF_B_KV + 2 * B_KV_HEADS * HEAD_DIM
OFF_GATE = OFF_C + 3 * C_WIDTH
GATE_COLS = N_BRANCH * D_MODEL

ROWS = BATCH * SEQ
N_BLK = SEQ // BLK
QKV_W = 3 * A_WIDTH
LANES = 128
B_COLS = B_WIDTH + 2 * B_KV_HEADS * LANES
NAT_COLS = QKV_W + B_COLS + QKV_W
NEG = -1e30
STICK_DEAD = -104.0
LOG2_E = 1.4426950408889634
LN_2 = 0.6931471805599453

V7X_VMEM_BYTES = 64 * 1024 * 1024
VMEM_LIMIT = V7X_VMEM_BYTES * 7 // 8

TM_PROJ = 512
TM_FFN = 512
TN_FFN = 512
FFN_AHEAD = 2
SUBLANES = 8
BAND_UNROLL = 2
STICK_UNROLL = 4

BF16 = jnp.bfloat16
F32 = jnp.float32


def _params(*sem):
    return pltpu.CompilerParams(dimension_semantics=sem, vmem_limit_bytes=VMEM_LIMIT)


def _dot(a, b):
    return jnp.dot(a, b, preferred_element_type=F32)


def _dot_nt(a, b):
    return lax.dot_general(a, b, (((1,), (1,)), ((), ())), preferred_element_type=F32)


def _rms(xv, gain):
    y = xv * lax.rsqrt(jnp.mean(xv * xv, axis=-1, keepdims=True) + EPS)
    return y * gain


def _bucket_constants():
    a = np.arange(BLK)[:, None]
    b = np.arange(2 * BLK)[None, :]
    dist = np.maximum(a + BLK - b, 0)
    max_exact = NUM_BUCKETS // 2
    out = []
    for _, d in A_GROUPS:
        n = dist * d
        nf = np.maximum(n, 1).astype(np.float64)
        large = max_exact + (np.log(nf / max_exact) / math.log(MAX_DISTANCE / max_exact)
                             * (NUM_BUCKETS - max_exact)).astype(np.int64)
        large = np.minimum(large, NUM_BUCKETS - 1)
        out.append(np.where(n < max_exact, n, large))
    return np.stack(out).astype(np.int32)


def _bias_kernel(tab_ref, bucket_ref, out_ref):
    hh = pl.program_id(0)
    a = lax.broadcasted_iota(jnp.int32, (BLK, 2 * BLK), 0)
    b = lax.broadcasted_iota(jnp.int32, (BLK, 2 * BLK), 1)
    dist = a + BLK - b
    max_dist = jnp.where(hh < N_A_GROUP_HEADS, BLK, B_WINDOW - 1)
    valid = (dist >= 0) & (dist <= max_dist)
    bk = bucket_ref[...]
    acc = jnp.zeros((BLK, 2 * BLK), F32)
    for k in range(NUM_BUCKETS):
        acc = jnp.where(bk == k, tab_ref[k, hh], acc)
    out_ref[...] = jnp.where(valid, acc, NEG)


def _band_bias(rel_bias):
    buckets = jnp.asarray(_bucket_constants())
    return pl.pallas_call(
        _bias_kernel,
        grid=(N_BIAS_HEADS,),
        in_specs=[
            pl.BlockSpec(memory_space=pltpu.SMEM),
            pl.BlockSpec((None, BLK, 2 * BLK),
                         lambda h: (jnp.where(h < N_A_GROUP_HEADS, h // A_HEADS, 0), 0, 0)),
        ],
        out_specs=pl.BlockSpec((None, BLK, 2 * BLK), lambda h: (h, 0, 0)),
        out_shape=jax.ShapeDtypeStruct((N_BIAS_HEADS, BLK, 2 * BLK), F32),
        compiler_params=_params("arbitrary"),
        name="band_bias",
    )(rel_bias, buckets)


def _in_proj_kernel(x_ref, g_ref, w_ref, a0_ref, a1_ref, a2_ref, b_ref, c_ref, xs_s):
    gain = g_ref[...]
    n_lane_tiles = D_MODEL // LANES

    def proj(xv, col0, width):
        h = _rms(xv, gain).astype(BF16)
        return _dot(h, w_ref[:, col0:col0 + width]).astype(BF16)

    xv = x_ref[...]
    nat = proj(xv, 0, NAT_COLS)
    a0_ref[...] = nat[:, :QKV_W]
    b_ref[...] = nat[:, QKV_W:QKV_W + B_COLS]
    c_ref[...] = nat[:, QKV_W + B_COLS:]
    for c in range(n_lane_tiles):
        xs_s[c] = xv[:, c * LANES:(c + 1) * LANES]
    for ref, (_, d), col0 in ((a1_ref, A_GROUPS[1], NAT_COLS), (a2_ref, A_GROUPS[2], NAT_COLS + QKV_W)):
        per = TM_PROJ // d
        xs = jnp.concatenate(
            [jnp.concatenate([xs_s[c, pl.ds(r, per, stride=d), :] for c in range(n_lane_tiles)], axis=1)
             for r in range(d)], axis=0)
        ref[...] = proj(xs, col0, QKV_W).reshape(d, per, QKV_W)


def _layer_spec(layer, shape):
    return pl.BlockSpec((None,) + shape, lambda i: (layer, 0, 0))


def _in_proj(layer, x2, gain, w_qkv):
    tiles_per_seq = SEQ // TM_PROJ
    d1, d2 = A_GROUPS[1][1], A_GROUPS[2][1]

    def nat(width):
        return pl.BlockSpec((TM_PROJ, width), lambda i: (i, 0))

    return pl.pallas_call(
        _in_proj_kernel,
        grid=(ROWS // TM_PROJ,),
        in_specs=[
            pl.BlockSpec((TM_PROJ, D_MODEL), lambda i: (i, 0)),
            _layer_spec(layer, (1, D_MODEL)),
            _layer_spec(layer, (D_MODEL, NAT_COLS + 2 * QKV_W)),
        ],
        out_specs=[
            nat(QKV_W),
            pl.BlockSpec((None, d1, TM_PROJ // d1, QKV_W),
                         lambda i: (i // tiles_per_seq, 0, i % tiles_per_seq, 0)),
            pl.BlockSpec((None, d2, TM_PROJ // d2, QKV_W),
                         lambda i: (i // tiles_per_seq, 0, i % tiles_per_seq, 0)),
            nat(B_COLS),
            nat(QKV_W),
        ],
        out_shape=[
            jax.ShapeDtypeStruct((ROWS, QKV_W), BF16),
            jax.ShapeDtypeStruct((BATCH, d1, SEQ // d1, QKV_W), BF16),
            jax.ShapeDtypeStruct((BATCH, d2, SEQ // d2, QKV_W), BF16),
            jax.ShapeDtypeStruct((ROWS, B_COLS), BF16),
            jax.ShapeDtypeStruct((ROWS, QKV_W), BF16),
        ],
        scratch_shapes=[pltpu.VMEM((D_MODEL // LANES, TM_PROJ, LANES), F32)],
        compiler_params=_params("arbitrary"),
        name="in_proj",
    )(x2, gain, w_qkv)


def _low_lanes(rows):
    return lax.broadcasted_iota(jnp.int32, (rows, LANES), 1) < HEAD_DIM


def _stack_pair(q2):
    low = _low_lanes(BLK)
    zero = jnp.zeros_like(q2)
    return jnp.concatenate([jnp.where(low, q2, zero), jnp.where(low, zero, q2)], axis=0)


def _unstack_pair(o):
    return jnp.where(_low_lanes(BLK), o[:BLK], o[BLK:])


def _band_blocks(qs, kps, kcs, vps, vcs, bias_ps, bias_cs):
    sps = [_dot_nt(q, k) for q, k in zip(qs, kps)]
    scs = [_dot_nt(q, k) for q, k in zip(qs, kcs)]
    pps, pcs, stats = [], [], []
    for sp, sc, bias_p, bias_c in zip(sps, scs, bias_ps, bias_cs):
        sp = sp + bias_p
        sc = sc + bias_c
        m = jnp.max(jnp.maximum(sp, sc), axis=-1, keepdims=True)
        pp = jnp.exp(sp - m)
        pc = jnp.exp(sc - m)
        stats.append((m, jnp.sum(pp + pc, axis=-1, keepdims=True)))
        pps.append(pp.astype(BF16))
        pcs.append(pc.astype(BF16))
    nums = [_dot(pp, vp) + _dot(pc, vc) for pp, pc, vp, vc in zip(pps, pcs, vps, vcs)]
    return nums, stats


def _attn_a_kernel(a0_ref, a1_ref, a2_ref, bias_ref, o_ref, out_s, lse_s):
    n_pairs = A_HEADS // 2

    def cols(part, p):
        return slice(part * A_WIDTH + p * LANES, part * A_WIDTH + (p + 1) * LANES)

    for g, (ref, (_, d)) in enumerate(zip((a0_ref, a1_ref, a2_ref), A_GROUPS)):
        nb = N_BLK // d

        def step(n, carry, g=g, ref=ref, d=d, nb=nb):
            units, qs, kps, kcs, vps, vcs, bias_ps, bias_cs = [], [], [], [], [], [], [], []
            for u in range(BAND_UNROLL):
                blk = n * BAND_UNROLL + u
                r = blk // nb
                i = blk % nb
                row_c = pl.multiple_of(i * BLK, BLK)
                row_p = pl.multiple_of(jnp.maximum(i - 1, 0) * BLK, BLK)
                no_prev = jnp.where(i > 0, 0.0, NEG)
                if d == 1:
                    rows = pl.ds(row_c, BLK)
                else:
                    rows = pl.ds(i * (BLK * d) + r, BLK, stride=d)
                for p in range(n_pairs):
                    units.append((p, rows))
                    heads = slice(g * A_HEADS + 2 * p, g * A_HEADS + 2 * p + 2)
                    qs.append(_stack_pair(ref[r, pl.ds(row_c, BLK), cols(0, p)]))
                    kps.append(ref[r, pl.ds(row_p, BLK), cols(1, p)])
                    kcs.append(ref[r, pl.ds(row_c, BLK), cols(1, p)])
                    vps.append(ref[r, pl.ds(row_p, BLK), cols(2, p)])
                    vcs.append(ref[r, pl.ds(row_c, BLK), cols(2, p)])
                    bias_ps.append(bias_ref[heads, :, 0:BLK].reshape(2 * BLK, BLK) + no_prev)
                    bias_cs.append(bias_ref[heads, :, BLK:2 * BLK].reshape(2 * BLK, BLK))
            nums, stats = _band_blocks(qs, kps, kcs, vps, vcs, bias_ps, bias_cs)
            for (p, rows), num, (m, l) in zip(units, nums, stats):
                out_s[g, p, rows, :] = _unstack_pair(num / l)
                lse_s[g, p, rows, :] = _unstack_pair(jnp.broadcast_to(m + jnp.log(l), (2 * BLK, LANES)))
            return carry

        lax.fori_loop(0, N_BLK // BAND_UNROLL, step, 0)

    def combine(t, carry):
        rows = pl.ds(pl.multiple_of(t * BLK, BLK), BLK)
        for p in range(n_pairs):
            l0, l1, l2 = lse_s[0, p, rows, :], lse_s[1, p, rows, :], lse_s[2, p, rows, :]
            mx = jnp.maximum(jnp.maximum(l0, l1), l2)
            e0, e1, e2 = jnp.exp(l0 - mx), jnp.exp(l1 - mx), jnp.exp(l2 - mx)
            top = e0 * out_s[0, p, rows, :] + e1 * out_s[1, p, rows, :] + e2 * out_s[2, p, rows, :]
            o_ref[rows, p * LANES:(p + 1) * LANES] = (top / (e0 + e1 + e2)).astype(BF16)
        return carry

    lax.fori_loop(0, N_BLK, combine, 0)


def _attn_a(a0, a1, a2, bias):
    d1, d2 = A_GROUPS[1][1], A_GROUPS[2][1]
    return pl.pallas_call(
        _attn_a_kernel,
        grid=(BATCH,),
        in_specs=[
            pl.BlockSpec((None, 1, SEQ, QKV_W), lambda b: (b, 0, 0, 0)),
            pl.BlockSpec((None, d1, SEQ // d1, QKV_W), lambda b: (b, 0, 0, 0)),
            pl.BlockSpec((None, d2, SEQ // d2, QKV_W), lambda b: (b, 0, 0, 0)),
            pl.BlockSpec((N_A_GROUP_HEADS, BLK, 2 * BLK), lambda b: (0, 0, 0)),
        ],
        out_specs=pl.BlockSpec((SEQ, A_WIDTH), lambda b: (b, 0)),
        out_shape=jax.ShapeDtypeStruct((ROWS, A_WIDTH), BF16),
        scratch_shapes=[
            pltpu.VMEM((len(A_GROUPS), A_HEADS // 2, SEQ, LANES), F32),
            pltpu.VMEM((len(A_GROUPS), A_HEADS // 2, SEQ, LANES), F32),
        ],
        compiler_params=_params("arbitrary"),
        name="attn_dilated",
    )(a0.reshape(BATCH, 1, SEQ, QKV_W), a1, a2, bias)


def _attn_b_kernel(sink_ref, qkv_ref, bias_ref, o_ref):
    def kv_cols(part, kvh):
        start = B_WIDTH + (part * B_KV_HEADS + kvh) * LANES
        return slice(start, start + LANES)

    def step(n, carry):
        blocks, qs, kps, kcs, vps, vcs, bias_ps, bias_cs = [], [], [], [], [], [], [], []
        for u in range(BAND_UNROLL):
            i = n * BAND_UNROLL + u
            row_c = pl.multiple_of(i * BLK, BLK)
            row_p = pl.multiple_of(jnp.maximum(i - 1, 0) * BLK, BLK)
            no_prev = jnp.where(i > 0, 0.0, NEG)
            blocks.append(row_c)
            for kvh in range(B_KV_HEADS):
                heads = slice(kvh * B_GROUP, (kvh + 1) * B_GROUP)
                tiles = range(kvh * B_GROUP // 2, (kvh + 1) * B_GROUP // 2)
                qs.append(jnp.concatenate(
                    [_stack_pair(qkv_ref[pl.ds(row_c, BLK), t * LANES:(t + 1) * LANES]) for t in tiles],
                    axis=0))
                kps.append(qkv_ref[pl.ds(row_p, BLK), kv_cols(0, kvh)])
                kcs.append(qkv_ref[pl.ds(row_c, BLK), kv_cols(0, kvh)])
                vps.append(qkv_ref[pl.ds(row_p, BLK), kv_cols(1, kvh)])
                vcs.append(qkv_ref[pl.ds(row_c, BLK), kv_cols(1, kvh)])
                bias_ps.append(bias_ref[heads, :, 0:BLK].reshape(B_GROUP * BLK, BLK) + no_prev)
                bias_cs.append(bias_ref[heads, :, BLK:2 * BLK].reshape(B_GROUP * BLK, BLK))
        nums, stats = _band_blocks(qs, kps, kcs, vps, vcs, bias_ps, bias_cs)
        for u, row_c in enumerate(blocks):
            for kvh in range(B_KV_HEADS):
                num = nums[u * B_KV_HEADS + kvh]
                m, l = stats[u * B_KV_HEADS + kvh]
                scaled = []
                for gq in range(B_GROUP):
                    rows = slice(gq * BLK, (gq + 1) * BLK)
                    sink = sink_ref[0, kvh * B_GROUP + gq]
                    mx = jnp.maximum(m[rows], sink)
                    c = jnp.exp(m[rows] - mx)
                    scaled.append(num[rows] * (c / (l[rows] * c + jnp.exp(sink - mx))))
                for t in range(B_GROUP // 2):
                    tile = kvh * B_GROUP // 2 + t
                    o_ref[pl.ds(row_c, BLK), tile * LANES:(tile + 1) * LANES] = _unstack_pair(
                        jnp.concatenate(scaled[2 * t:2 * t + 2], axis=0)).astype(BF16)
        return carry

    lax.fori_loop(0, N_BLK // BAND_UNROLL, step, 0)


def _attn_b(qkv, bias, sinks):
    return pl.pallas_call(
        _attn_b_kernel,
        grid=(BATCH,),
        in_specs=[
            pl.BlockSpec(memory_space=pltpu.SMEM),
            pl.BlockSpec((SEQ, B_COLS), lambda b: (b, 0)),
            pl.BlockSpec((B_Q_HEADS, BLK, 2 * BLK), lambda b: (0, 0, 0)),
        ],
        out_specs=pl.BlockSpec((SEQ, B_WIDTH), lambda b: (b, 0)),
        out_shape=jax.ShapeDtypeStruct((ROWS, B_WIDTH), BF16),
        compiler_params=_params("arbitrary"),
        name="attn_window",
    )(sinks.reshape(1, B_Q_HEADS), qkv, bias)


def _suffix_matrix():
    j = np.arange(2 * BLK)[:, None] % BLK
    s = np.arange(2 * BLK)[None, :]
    return ((s >= BLK) | (j > s)).astype(np.float32)


def _stick_blocks(qs, ks, vs, suffix, laters, strict_lower):
    zs = [_dot_nt(q, k) for q, k in zip(qs, ks)]
    log_betas, splits = [], []
    for z in zs:
        neg_soft = jnp.log2(1.0 + jnp.exp2(jnp.abs(z) * -LOG2_E)) * -LN_2
        log_keep = neg_soft - jnp.maximum(z, 0.0)
        log_betas.append(z + log_keep)
        if laters is None:
            log_keep = jnp.where(strict_lower, log_keep, 0.0)
        hi = log_keep.astype(BF16)
        lo = (log_keep - hi.astype(F32)).astype(BF16)
        splits.append(jnp.concatenate([hi, lo], axis=1))
    sums = [_dot(s, suffix) for s in splits]
    ws, new_laters = [], []
    for n, (log_beta, s) in enumerate(zip(log_betas, sums)):
        if laters is None:
            w = jnp.where(strict_lower, jnp.exp(log_beta + s[:, :BLK]), 0.0)
            new_laters.append(s[:, BLK:])
        else:
            w = jnp.exp(log_beta + s[:, :BLK] + laters[n])
            new_laters.append(laters[n] + s[:, BLK:])
        ws.append(w.astype(BF16))
    return [_unstack_pair(_dot(w, v)) for w, v in zip(ws, vs)], new_laters


def _stick_schedule():
    q_blk, slot, k_blk, first_step = [], [], [], [0, 0]
    for dist in range(1, N_BLK):
        group = [(i, i, i - dist) for i in range(dist, N_BLK)]
        group += [(0, N_BLK, 0)] * (-len(group) % STICK_UNROLL)
        for qi, sl, kj in group:
            q_blk.append(qi)
            slot.append(sl)
            k_blk.append(kj)
        first_step.append(len(q_blk) // STICK_UNROLL)
    return tuple(np.asarray(t, np.int32) for t in (q_blk, slot, k_blk, first_step))


def _attn_c_kernel(qblk_ref, slot_ref, kblk_ref, step_ref, qkv_ref, suffix_ref, o_ref, later_s, acc_s):
    n_pairs = C_HEADS // 2
    t_idx = lax.broadcasted_iota(jnp.int32, (2 * BLK, BLK), 0) % BLK
    s_idx = lax.broadcasted_iota(jnp.int32, (2 * BLK, BLK), 1)
    strict_lower = s_idx < t_idx

    def cols(part, p):
        return slice(part * C_WIDTH + p * LANES, part * C_WIDTH + (p + 1) * LANES)

    def rows_of(blk):
        return pl.ds(pl.multiple_of(blk * BLK, BLK), BLK)

    for p in range(n_pairs):
        later_s[p, N_BLK] = jnp.zeros((2 * BLK, BLK), F32)
        acc_s[p, N_BLK] = jnp.zeros((BLK, LANES), F32)

    def diag(n, carry):
        where = [(p, n * STICK_UNROLL + u) for u in range(STICK_UNROLL) for p in range(n_pairs)]
        parts, laters = _stick_blocks(
            [_stack_pair(qkv_ref[rows_of(i), cols(0, p)]) for p, i in where],
            [qkv_ref[rows_of(i), cols(1, p)] for p, i in where],
            [qkv_ref[rows_of(i), cols(2, p)] for p, i in where],
            suffix_ref[...], None, strict_lower)
        for (p, i), part, later in zip(where, parts, laters):
            acc_s[p, i] = part
            later_s[p, i] = later
        return carry

    lax.fori_loop(0, N_BLK // STICK_UNROLL, diag, 0)

    def off_diag(n, carry):
        where = []
        for u in range(STICK_UNROLL):
            e = n * STICK_UNROLL + u
            q_rows, k_rows, sl = rows_of(qblk_ref[e]), rows_of(kblk_ref[e]), slot_ref[e]
            where += [(p, sl, q_rows, k_rows) for p in range(n_pairs)]
        accs = [acc_s[p, sl] for p, sl, _, _ in where]
        parts, laters = _stick_blocks(
            [_stack_pair(qkv_ref[q_rows, cols(0, p)]) for p, _, q_rows, _ in where],
            [qkv_ref[k_rows, cols(1, p)] for p, _, _, k_rows in where],
            [qkv_ref[k_rows, cols(2, p)] for p, _, _, k_rows in where],
            suffix_ref[...], [later_s[p, sl] for p, sl, _, _ in where], None)
        for (p, sl, _, _), acc, part, later in zip(where, accs, parts, laters):
            acc_s[p, sl] = acc + part
            later_s[p, sl] = later
        return carry

    def group(state):
        dist, _ = state
        lax.fori_loop(step_ref[dist], step_ref[dist + 1], off_diag, 0)

        def slot_max(i, mx):
            for p in range(n_pairs):
                mx = jnp.maximum(mx, later_s[p, i])
            return mx

        mx = lax.fori_loop(dist + 1, N_BLK, slot_max, jnp.full((2 * BLK, BLK), -jnp.inf, F32))
        return dist + 1, jnp.logical_not(jnp.max(mx) < STICK_DEAD)

    lax.while_loop(lambda state: (state[0] < N_BLK) & state[1], group, (jnp.int32(1), jnp.bool_(True)))

    def emit(i, carry):
        for p in range(n_pairs):
            o_ref[rows_of(i), p * LANES:(p + 1) * LANES] = acc_s[p, i].astype(BF16)
        return carry

    lax.fori_loop(0, N_BLK, emit, 0)


def _attn_c(qkv):
    suffix = jnp.asarray(_suffix_matrix(), BF16)
    schedule = [jnp.asarray(t) for t in _stick_schedule()]
    smem = pl.BlockSpec(memory_space=pltpu.SMEM)
    return pl.pallas_call(
        _attn_c_kernel,
        grid=(BATCH,),
        in_specs=[
            smem, smem, smem, smem,
            pl.BlockSpec((SEQ, QKV_W), lambda b: (b, 0)),
            pl.BlockSpec((2 * BLK, 2 * BLK), lambda b: (0, 0)),
        ],
        out_specs=pl.BlockSpec((SEQ, C_WIDTH), lambda b: (b, 0)),
        out_shape=jax.ShapeDtypeStruct((ROWS, C_WIDTH), BF16),
        scratch_shapes=[
            pltpu.VMEM((C_HEADS // 2, N_BLK + 1, 2 * BLK, BLK), F32),
            pltpu.VMEM((C_HEADS // 2, N_BLK + 1, BLK, LANES), F32),
        ],
        compiler_params=_params("arbitrary"),
        name="attn_stick",
    )(*schedule, qkv, suffix)


def _merge_kernel(x_ref, oa_ref, ob_ref, oc_ref, gpre_ref, wg_ref, bg_ref, wa_ref, wb_ref, wc_ref,
                  wo_ref, gpost_ref, y_ref):
    xv = x_ref[...]
    h = _rms(xv, gpre_ref[...]).astype(BF16)
    merged = None
    for n, (o_ref, w_ref) in enumerate(((oa_ref, wa_ref), (ob_ref, wb_ref), (oc_ref, wc_ref))):
        cols = slice(n * D_MODEL, (n + 1) * D_MODEL)
        gate = jax.nn.sigmoid(_dot(h, wg_ref[:, cols]) + bg_ref[:, cols])
        term = gate * _dot(o_ref[...], w_ref[...])
        merged = term if merged is None else merged + term
    y = _dot(merged.astype(BF16), wo_ref[...])
    y_ref[...] = xv + _rms(y, gpost_ref[...])


def _merge(layer, x2, o_a, o_b, o_c, gpre, w_gate, b_gate, w_a, w_b, w_c, w_o, gpost):
    def rows(width):
        return pl.BlockSpec((TM_PROJ, width), lambda i: (i, 0))

    def whole(shape):
        return _layer_spec(layer, shape)

    return pl.pallas_call(
        _merge_kernel,
        grid=(ROWS // TM_PROJ,),
        in_specs=[
            rows(D_MODEL), rows(A_WIDTH), rows(B_WIDTH), rows(C_WIDTH),
            whole((1, D_MODEL)), whole((D_MODEL, GATE_COLS)), whole((1, GATE_COLS)),
            whole((A_WIDTH, D_MODEL)), whole((B_WIDTH, D_MODEL)), whole((C_WIDTH, D_MODEL)),
            whole((D_MODEL, D_MODEL)), whole((1, D_MODEL)),
        ],
        out_specs=rows(D_MODEL),
        out_shape=jax.ShapeDtypeStruct((ROWS, D_MODEL), F32),
        compiler_params=_params("arbitrary"),
        name="merge_out_proj",
    )(x2, o_a, o_b, o_c, gpre, w_gate, b_gate, w_a, w_b, w_c, w_o, gpost)


def _gelu_tanh_doubled(x):
    k = math.sqrt(2.0 / math.pi)
    return x * (1.0 + jnp.tanh(x * (k + (k * 0.044715) * (x * x))))


def _ffn_kernel(x_ref, gpre_ref, wup_ref, cw_ref, cb_ref, wd_ref, gpost_ref, y_ref, xs_s, ys_s, tail_s):
    i = pl.program_id(0)
    n_chunks = D_FF // TN_FFN
    steps = TM_FFN // SUBLANES
    n_lane_tiles = D_MODEL // LANES
    def strided_rows(v):
        s, j0 = divmod(v * SUBLANES, steps)
        return pl.ds(j0 * SUBLANES + s, SUBLANES, stride=SUBLANES)

    for lt in range(n_lane_tiles):
        for v in range(steps):
            xs_s[lt, strided_rows(v), :] = x_ref[v * SUBLANES:(v + 1) * SUBLANES, lt * LANES:(lt + 1) * LANES]
    xv = jnp.concatenate([xs_s[lt] for lt in range(n_lane_tiles)], axis=1)
    h = _rms(xv, gpre_ref[...]).astype(BF16)
    seq_start = (i % (SEQ // TM_FFN)) == 0

    def up(c):
        return [_dot(h, wup_ref[:, half * D_FF + c * TN_FFN:half * D_FF + (c + 1) * TN_FFN])
                for half in range(2)]

    def conv(u, c, half):
        cols = slice(half * D_FF + c * TN_FFN, half * D_FF + (c + 1) * TN_FFN)
        last2 = u[TM_FFN - 2 * SUBLANES:, :]
        prev2 = jnp.where(seq_start, 0.0, tail_s[2 * c + half])
        tail_s[2 * c + half] = last2
        wrapped = [jnp.concatenate([prev2[g * SUBLANES + SUBLANES - 1:(g + 1) * SUBLANES],
                                    last2[g * SUBLANES:(g + 1) * SUBLANES - 1]], axis=0) for g in range(2)]
        back1 = jnp.concatenate([wrapped[1], u[:TM_FFN - SUBLANES]], axis=0)
        back2 = jnp.concatenate([wrapped[0], wrapped[1], u[:TM_FFN - 2 * SUBLANES]], axis=0)
        return cw_ref[2:3, cols] * u + cw_ref[1:2, cols] * back1 + cw_ref[0:1, cols] * back2 + cb_ref[:, cols]

    acc = None
    ups = [up(c) for c in range(FFN_AHEAD)]
    for c in range(n_chunks):
        if c + FFN_AHEAD < n_chunks:
            ups.append(up(c + FFN_AHEAD))
        gate_u, val_u = ups.pop(0)
        act = (_gelu_tanh_doubled(conv(gate_u, c, 0)) * conv(val_u, c, 1)).astype(BF16)
        part = _dot(act, wd_ref[c * TN_FFN:(c + 1) * TN_FFN, :])
        acc = part if acc is None else acc + part
    y = xv + _rms(acc, gpost_ref[...])
    for lt in range(n_lane_tiles):
        ys_s[lt] = y[:, lt * LANES:(lt + 1) * LANES]
        for v in range(steps):
            y_ref[v * SUBLANES:(v + 1) * SUBLANES, lt * LANES:(lt + 1) * LANES] = ys_s[lt, strided_rows(v), :]


def _ffn(layer, x2, gpre, w_up, conv_w, conv_b, w_down, gpost):
    n_chunks = D_FF // TN_FFN

    def whole(shape):
        return pl.BlockSpec((None,) + shape, lambda i: (layer, 0, 0), pipeline_mode=pl.Buffered(1))

    return pl.pallas_call(
        _ffn_kernel,
        grid=(ROWS // TM_FFN,),
        in_specs=[
            pl.BlockSpec((TM_FFN, D_MODEL), lambda i: (i, 0)),
            whole((1, D_MODEL)),
            whole((D_MODEL, 2 * D_FF)),
            whole((CONV_WIDTH, 2 * D_FF)),
            whole((1, 2 * D_FF)),
            whole((D_FF, D_MODEL)),
            whole((1, D_MODEL)),
        ],
        out_specs=pl.BlockSpec((TM_FFN, D_MODEL), lambda i: (i, 0)),
        out_shape=jax.ShapeDtypeStruct((ROWS, D_MODEL), F32),
        scratch_shapes=[
            pltpu.VMEM((D_MODEL // LANES, TM_FFN, LANES), F32),
            pltpu.VMEM((D_MODEL // LANES, TM_FFN, LANES), F32),
            pltpu.VMEM((2 * n_chunks, 2 * SUBLANES, TN_FFN), F32),
        ],
        compiler_params=_params("arbitrary"),
        name="conv_ffn",
    )(x2, gpre, w_up, conv_w, conv_b, w_down, gpost)


def _qkv_weight(w_in):
    rows = w_in.shape[:-1]
    a = w_in[..., :A_QKV_COLS].reshape(rows + (3, len(A_GROUPS), A_WIDTH))
    a = a * jnp.asarray([SCALE, 1.0, 1.0], F32)[:, None, None]
    a = jnp.swapaxes(a, -3, -2).reshape(rows + (len(A_GROUPS), QKV_W))
    kv = w_in[..., OFF_B_KV:OFF_C].reshape(rows + (2 * B_KV_HEADS, 1, HEAD_DIM))
    kv = jnp.broadcast_to(kv, rows + (2 * B_KV_HEADS, 2, HEAD_DIM)).reshape(rows + (-1,))
    b = jnp.concatenate([w_in[..., OFF_B_Q:OFF_B_KV] * SCALE, kv], axis=-1)
    c = jnp.concatenate([w_in[..., OFF_C:OFF_C + C_WIDTH] * SCALE, w_in[..., OFF_C + C_WIDTH:OFF_GATE]],
                        axis=-1)
    return jnp.concatenate([a[..., 0, :], b, c, a[..., 1, :], a[..., 2, :]], axis=-1).astype(BF16)


def kernel(x, rel_bias, attn_pre_norm, w_in, b_gate, sinks, w_br_a, w_br_b, w_br_c, w_out,
           attn_post_norm, ffn_pre_norm, w_up, conv_w, conv_b, w_down, ffn_post_norm):
    assert x.shape == (BATCH, SEQ, D_MODEL) and x.dtype == F32
    bias = _band_bias(rel_bias)
    bias_a, bias_b = bias[:N_A_GROUP_HEADS], bias[N_A_GROUP_HEADS:]
    x2 = x.reshape(ROWS, D_MODEL)

    def row_param(p):
        return p.reshape(DEPTH, 1, -1)

    w_qkv = _qkv_weight(w_in)
    w_gate = w_in[..., OFF_GATE:].astype(BF16)
    w_a, w_b, w_c, w_o = (w.astype(BF16) for w in (w_br_a, w_br_b, w_br_c, w_out))
    w_up16, w_down16 = w_up.astype(BF16), (w_down * 0.5).astype(BF16)
    attn_pre, attn_post, ffn_pre, ffn_post, gate_b, conv_bias = (
        row_param(p) for p in (attn_pre_norm, attn_post_norm, ffn_pre_norm, ffn_post_norm, b_gate, conv_b))
    for layer in range(DEPTH):
        a0, a1, a2, bq, cq = _in_proj(layer, x2, attn_pre, w_qkv)
        o_a = _attn_a(a0, a1, a2, bias_a)
        o_b = _attn_b(bq, bias_b, sinks[layer])
        o_c = _attn_c(cq)
        x2 = _merge(layer, x2, o_a, o_b, o_c, attn_pre, w_gate, gate_b, w_a, w_b, w_c, w_o, attn_post)
        x2 = _ffn(layer, x2, ffn_pre, w_up16, conv_w, conv_bias, w_down16, ffn_post)
    return x2.reshape(BATCH, SEQ, D_MODEL)
```

```python
import math

import numpy as np
import jax
import jax.numpy as jnp
from jax import lax
from jax.experimental import pallas as pl
from jax.experimental.pallas import tpu as pltpu

D_MODEL = 1024
BATCH = 8
SEQ = 2048
DEPTH = 2
HEAD_DIM = 64
BLK = 128
A_GROUPS = ((128, 1), (512, 4), (2048, 16))
A_HEADS = 4
B_Q_HEADS = 8
B_KV_HEADS = 2
B_WINDOW = 128
C_HEADS = 4
N_BRANCH = 3
NUM_BUCKETS = 32
MAX_DISTANCE = 2048
D_FF = 4 * D_MODEL
CONV_WIDTH = 3
EPS = 1e-6
SCALE = HEAD_DIM ** -0.5

A_WIDTH = A_HEADS * HEAD_DIM
B_WIDTH = B_Q_HEADS * HEAD_DIM
C_WIDTH = C_HEADS * HEAD_DIM
B_GROUP = B_Q_HEADS // B_KV_HEADS
N_A_GROUP_HEADS = len(A_GROUPS) * A_HEADS
N_BIAS_HEADS = N_A_GROUP_HEADS + B_Q_HEADS
A_QKV_COLS = 3 * N_A_GROUP_HEADS * HEAD_DIM
OFF_B_Q = A_QKV_COLS
OFF_B_KV = OFF_B_Q + B_WIDTH
OFF_C = OFF_B_KV + 2 * B_KV_HEADS * HEAD_DIM
OFF_GATE = OFF_C + 3 * C_WIDTH
GATE_COLS = N_BRANCH * D_MODEL

ROWS = BATCH * SEQ
N_BLK = SEQ // BLK
QKV_W = 3 * A_WIDTH
LANES = 128
B_COLS = B_WIDTH + 2 * B_KV_HEADS * LANES
NAT_COLS = QKV_W + B_COLS + QKV_W
NEG = -1e30
STICK_DEAD = -104.0
LOG2_E = 1.4426950408889634
LN_2 = 0.6931471805599453

V7X_VMEM_BYTES = 64 * 1024 * 1024
VMEM_LIMIT = V7X_VMEM_BYTES * 7 // 8

TM_PROJ = 512
TM_FFN = 512
TN_FFN = 512
FFN_AHEAD = 2
SUBLANES = 8
BAND_UNROLL = 4
STICK_UNROLL = 4

BF16 = jnp.bfloat16
F32 = jnp.float32


def _params(*sem):
    return pltpu.CompilerParams(dimension_semantics=sem, vmem_limit_bytes=VMEM_LIMIT)


def _dot(a, b):
    return jnp.dot(a, b, preferred_element_type=F32)


def _dot_nt(a, b):
    return lax.dot_general(a, b, (((1,), (1,)), ((), ())), preferred_element_type=F32)


def _rms(xv, gain):
    y = xv * lax.rsqrt(jnp.mean(xv * xv, axis=-1, keepdims=True) + EPS)
    return y * gain


def _bucket_constants():
    a = np.arange(BLK)[:, None]
    b = np.arange(2 * BLK)[None, :]
    dist = np.maximum(a + BLK - b, 0)
    max_exact = NUM_BUCKETS // 2
    out = []
    for _, d in A_GROUPS:
        n = dist * d
        nf = np.maximum(n, 1).astype(np.float64)
        large = max_exact + (np.log(nf / max_exact) / math.log(MAX_DISTANCE / max_exact)
                             * (NUM_BUCKETS - max_exact)).astype(np.int64)
        large = np.minimum(large, NUM_BUCKETS - 1)
        out.append(np.where(n < max_exact, n, large))
    return np.stack(out).astype(np.int32)


def _bias_kernel(tab_ref, bucket_ref, out_ref):
    hh = pl.program_id(0)
    a = lax.broadcasted_iota(jnp.int32, (BLK, 2 * BLK), 0)
    b = lax.broadcasted_iota(jnp.int32, (BLK, 2 * BLK), 1)
    dist = a + BLK - b
    max_dist = jnp.where(hh < N_A_GROUP_HEADS, BLK, B_WINDOW - 1)
    valid = (dist >= 0) & (dist <= max_dist)
    bk = bucket_ref[...]
    acc = jnp.zeros((BLK, 2 * BLK), F32)
    for k in range(NUM_BUCKETS):
        acc = jnp.where(bk == k, tab_ref[k, hh], acc)
    out_ref[...] = jnp.where(valid, acc, NEG)


def _band_bias(rel_bias):
    buckets = jnp.asarray(_bucket_constants())
    return pl.pallas_call(
        _bias_kernel,
        grid=(N_BIAS_HEADS,),
        in_specs=[
            pl.BlockSpec(memory_space=pltpu.SMEM),
            pl.BlockSpec((None, BLK, 2 * BLK),
                         lambda h: (jnp.where(h < N_A_GROUP_HEADS, h // A_HEADS, 0), 0, 0)),
        ],
        out_specs=pl.BlockSpec((None, BLK, 2 * BLK), lambda h: (h, 0, 0)),
        out_shape=jax.ShapeDtypeStruct((N_BIAS_HEADS, BLK, 2 * BLK), F32),
        compiler_params=_params("arbitrary"),
        name="band_bias",
    )(rel_bias, buckets)


def _in_proj_kernel(x_ref, g_ref, w_ref, a0_ref, a1_ref, a2_ref, b_ref, c_ref, ys_s):
    n_lane_tiles = D_MODEL // LANES

    def proj(h, col0, width):
        return _dot(h.astype(BF16), w_ref[:, col0:col0 + width]).astype(BF16)

    y = _rms(x_ref[...], g_ref[...])
    nat = proj(y, 0, NAT_COLS)
    a0_ref[...] = nat[:, :QKV_W]
    b_ref[...] = nat[:, QKV_W:QKV_W + B_COLS]
    c_ref[...] = nat[:, QKV_W + B_COLS:]
    for c in range(n_lane_tiles):
        ys_s[c] = y[:, c * LANES:(c + 1) * LANES]
    for ref, (_, d), col0 in ((a1_ref, A_GROUPS[1], NAT_COLS), (a2_ref, A_GROUPS[2], NAT_COLS + QKV_W)):
        per = TM_PROJ // d
        regrouped = jnp.concatenate(
            [jnp.concatenate([ys_s[c, pl.ds(r, per, stride=d), :] for c in range(n_lane_tiles)], axis=1)
             for r in range(d)], axis=0)
        ref[...] = proj(regrouped, col0, QKV_W).reshape(d, per, QKV_W)


def _layer_spec(layer, shape):
    return pl.BlockSpec((None,) + shape, lambda i: (layer, 0, 0))


def _in_proj(layer, x2, gain, w_qkv):
    tiles_per_seq = SEQ // TM_PROJ
    d1, d2 = A_GROUPS[1][1], A_GROUPS[2][1]

    def nat(width):
        return pl.BlockSpec((TM_PROJ, width), lambda i: (i, 0))

    return pl.pallas_call(
        _in_proj_kernel,
        grid=(ROWS // TM_PROJ,),
        in_specs=[
            pl.BlockSpec((TM_PROJ, D_MODEL), lambda i: (i, 0)),
            _layer_spec(layer, (1, D_MODEL)),
            _layer_spec(layer, (D_MODEL, NAT_COLS + 2 * QKV_W)),
        ],
        out_specs=[
            nat(QKV_W),
            pl.BlockSpec((None, d1, TM_PROJ // d1, QKV_W),
                         lambda i: (i // tiles_per_seq, 0, i % tiles_per_seq, 0)),
            pl.BlockSpec((None, d2, TM_PROJ // d2, QKV_W),
                         lambda i: (i // tiles_per_seq, 0, i % tiles_per_seq, 0)),
            nat(B_COLS),
            nat(QKV_W),
        ],
        out_shape=[
            jax.ShapeDtypeStruct((ROWS, QKV_W), BF16),
            jax.ShapeDtypeStruct((BATCH, d1, SEQ // d1, QKV_W), BF16),
            jax.ShapeDtypeStruct((BATCH, d2, SEQ // d2, QKV_W), BF16),
            jax.ShapeDtypeStruct((ROWS, B_COLS), BF16),
            jax.ShapeDtypeStruct((ROWS, QKV_W), BF16),
        ],
        scratch_shapes=[pltpu.VMEM((D_MODEL // LANES, TM_PROJ, LANES), F32)],
        compiler_params=_params("arbitrary"),
        name="in_proj",
    )(x2, gain, w_qkv)


def _low_lanes(rows):
    return lax.broadcasted_iota(jnp.int32, (rows, LANES), 1) < HEAD_DIM


def _stack_pair(q2):
    low = _low_lanes(BLK)
    zero = jnp.zeros_like(q2)
    return jnp.concatenate([jnp.where(low, q2, zero), jnp.where(low, zero, q2)], axis=0)


def _unstack_pair(o):
    return jnp.where(_low_lanes(BLK), o[:BLK], o[BLK:])


def _band_blocks(qs, kps, kcs, vps, vcs, bias_ps, bias_cs):
    ss = [_dot_nt(q, jnp.concatenate([kp, kc], axis=0)) for q, kp, kc in zip(qs, kps, kcs)]
    ps, stats = [], []
    for s, bias_p, bias_c in zip(ss, bias_ps, bias_cs):
        s = s + jnp.concatenate([bias_p, bias_c], axis=1)
        m = jnp.max(s, axis=-1, keepdims=True)
        p = jnp.exp(s - m)
        stats.append((m, jnp.sum(p, axis=-1, keepdims=True)))
        ps.append(p.astype(BF16))
    nums = [_dot(p, jnp.concatenate([vp, vc], axis=0)) for p, vp, vc in zip(ps, vps, vcs)]
    return nums, stats


def _attn_a_kernel(a0_ref, a1_ref, a2_ref, bias_ref, o_ref, out_s, lse_s):
    n_pairs = A_HEADS // 2

    def cols(part, p):
        return slice(part * A_WIDTH + p * LANES, part * A_WIDTH + (p + 1) * LANES)

    for g, (ref, (_, d)) in enumerate(zip((a0_ref, a1_ref, a2_ref), A_GROUPS)):
        nb = N_BLK // d

        def step(n, carry, g=g, ref=ref, d=d, nb=nb):
            units, qs, kps, kcs, vps, vcs, bias_ps, bias_cs = [], [], [], [], [], [], [], []
            for u in range(BAND_UNROLL):
                blk = n * BAND_UNROLL + u
                r = blk // nb
                i = blk % nb
                row_c = pl.multiple_of(i * BLK, BLK)
                row_p = pl.multiple_of(jnp.maximum(i - 1, 0) * BLK, BLK)
                no_prev = jnp.where(i > 0, 0.0, NEG)
                if d == 1:
                    rows = pl.ds(row_c, BLK)
                else:
                    rows = pl.ds(i * (BLK * d) + r, BLK, stride=d)
                for p in range(n_pairs):
                    units.append((p, rows))
                    heads = slice(g * A_HEADS + 2 * p, g * A_HEADS + 2 * p + 2)
                    qs.append(_stack_pair(ref[r, pl.ds(row_c, BLK), cols(0, p)]))
                    kps.append(ref[r, pl.ds(row_p, BLK), cols(1, p)])
                    kcs.append(ref[r, pl.ds(row_c, BLK), cols(1, p)])
                    vps.append(ref[r, pl.ds(row_p, BLK), cols(2, p)])
                    vcs.append(ref[r, pl.ds(row_c, BLK), cols(2, p)])
                    bias_ps.append(bias_ref[heads, :, 0:BLK].reshape(2 * BLK, BLK) + no_prev)
                    bias_cs.append(bias_ref[heads, :, BLK:2 * BLK].reshape(2 * BLK, BLK))
            nums, stats = _band_blocks(qs, kps, kcs, vps, vcs, bias_ps, bias_cs)
            for (p, rows), num, (m, l) in zip(units, nums, stats):
                out_s[g, p, rows, :] = _unstack_pair(num / l)
                lse_s[g, p, rows, :] = _unstack_pair(jnp.broadcast_to(m + jnp.log(l), (2 * BLK, LANES)))
            return carry

        lax.fori_loop(0, N_BLK // BAND_UNROLL, step, 0)

    def combine(t, carry):
        rows = pl.ds(pl.multiple_of(t * BLK, BLK), BLK)
        for p in range(n_pairs):
            l0, l1, l2 = lse_s[0, p, rows, :], lse_s[1, p, rows, :], lse_s[2, p, rows, :]
            mx = jnp.maximum(jnp.maximum(l0, l1), l2)
            e0, e1, e2 = jnp.exp(l0 - mx), jnp.exp(l1 - mx), jnp.exp(l2 - mx)
            top = e0 * out_s[0, p, rows, :] + e1 * out_s[1, p, rows, :] + e2 * out_s[2, p, rows, :]
            o_ref[rows, p * LANES:(p + 1) * LANES] = (top / (e0 + e1 + e2)).astype(BF16)
        return carry

    lax.fori_loop(0, N_BLK, combine, 0)


def _attn_a(a0, a1, a2, bias):
    d1, d2 = A_GROUPS[1][1], A_GROUPS[2][1]
    return pl.pallas_call(
        _attn_a_kernel,
        grid=(BATCH,),
        in_specs=[
            pl.BlockSpec((None, 1, SEQ, QKV_W), lambda b: (b, 0, 0, 0)),
            pl.BlockSpec((None, d1, SEQ // d1, QKV_W), lambda b: (b, 0, 0, 0)),
            pl.BlockSpec((None, d2, SEQ // d2, QKV_W), lambda b: (b, 0, 0, 0)),
            pl.BlockSpec((N_A_GROUP_HEADS, BLK, 2 * BLK), lambda b: (0, 0, 0)),
        ],
        out_specs=pl.BlockSpec((SEQ, A_WIDTH), lambda b: (b, 0)),
        out_shape=jax.ShapeDtypeStruct((ROWS, A_WIDTH), BF16),
        scratch_shapes=[
            pltpu.VMEM((len(A_GROUPS), A_HEADS // 2, SEQ, LANES), F32),
            pltpu.VMEM((len(A_GROUPS), A_HEADS // 2, SEQ, LANES), F32),
        ],
        compiler_params=_params("arbitrary"),
        name="attn_dilated",
    )(a0.reshape(BATCH, 1, SEQ, QKV_W), a1, a2, bias)


def _attn_b_kernel(sink_ref, qkv_ref, bias_ref, o_ref):
    def kv_cols(part, kvh):
        start = B_WIDTH + (part * B_KV_HEADS + kvh) * LANES
        return slice(start, start + LANES)

    def step(n, carry):
        blocks, qs, kps, kcs, vps, vcs, bias_ps, bias_cs = [], [], [], [], [], [], [], []
        for u in range(BAND_UNROLL):
            i = n * BAND_UNROLL + u
            row_c = pl.multiple_of(i * BLK, BLK)
            row_p = pl.multiple_of(jnp.maximum(i - 1, 0) * BLK, BLK)
            no_prev = jnp.where(i > 0, 0.0, NEG)
            blocks.append(row_c)
            for kvh in range(B_KV_HEADS):
                heads = slice(kvh * B_GROUP, (kvh + 1) * B_GROUP)
                tiles = range(kvh * B_GROUP // 2, (kvh + 1) * B_GROUP // 2)
                qs.append(jnp.concatenate(
                    [_stack_pair(qkv_ref[pl.ds(row_c, BLK), t * LANES:(t + 1) * LANES]) for t in tiles],
                    axis=0))
                kps.append(qkv_ref[pl.ds(row_p, BLK), kv_cols(0, kvh)])
                kcs.append(qkv_ref[pl.ds(row_c, BLK), kv_cols(0, kvh)])
                vps.append(qkv_ref[pl.ds(row_p, BLK), kv_cols(1, kvh)])
                vcs.append(qkv_ref[pl.ds(row_c, BLK), kv_cols(1, kvh)])
                bias_ps.append(bias_ref[heads, :, 0:BLK].reshape(B_GROUP * BLK, BLK) + no_prev)
                bias_cs.append(bias_ref[heads, :, BLK:2 * BLK].reshape(B_GROUP * BLK, BLK))
        nums, stats = _band_blocks(qs, kps, kcs, vps, vcs, bias_ps, bias_cs)
        for u, row_c in enumerate(blocks):
            for kvh in range(B_KV_HEADS):
                num = nums[u * B_KV_HEADS + kvh]
                m, l = stats[u * B_KV_HEADS + kvh]
                scaled = []
                for gq in range(B_GROUP):
                    rows = slice(gq * BLK, (gq + 1) * BLK)
                    sink = sink_ref[0, kvh * B_GROUP + gq]
                    scaled.append(num[rows] / (l[rows] + jnp.exp(sink - m[rows])))
                for t in range(B_GROUP // 2):
                    tile = kvh * B_GROUP // 2 + t
                    o_ref[pl.ds(row_c, BLK), tile * LANES:(tile + 1) * LANES] = _unstack_pair(
                        jnp.concatenate(scaled[2 * t:2 * t + 2], axis=0)).astype(BF16)
        return carry

    lax.fori_loop(0, N_BLK // BAND_UNROLL, step, 0)


def _attn_b(qkv, bias, sinks):
    return pl.pallas_call(
        _attn_b_kernel,
        grid=(BATCH,),
        in_specs=[
            pl.BlockSpec(memory_space=pltpu.SMEM),
            pl.BlockSpec((SEQ, B_COLS), lambda b: (b, 0)),
            pl.BlockSpec((B_Q_HEADS, BLK, 2 * BLK), lambda b: (0, 0, 0)),
        ],
        out_specs=pl.BlockSpec((SEQ, B_WIDTH), lambda b: (b, 0)),
        out_shape=jax.ShapeDtypeStruct((ROWS, B_WIDTH), BF16),
        compiler_params=_params("arbitrary"),
        name="attn_window",
    )(sinks.reshape(1, B_Q_HEADS), qkv, bias)


def _suffix_matrix():
    j = np.arange(2 * BLK)[:, None] % BLK
    s = np.arange(2 * BLK)[None, :]
    return ((s >= BLK) | (j > s)).astype(np.float32)


def _stick_blocks(qs, ks, vs, suffix, laters, strict_lower):
    zs = [_dot_nt(q, k) for q, k in zip(qs, ks)]
    log_betas, splits = [], []
    for z in zs:
        neg_soft = jnp.log2(1.0 + jnp.exp2(jnp.abs(z) * -LOG2_E)) * -LN_2
        log_keep = neg_soft - jnp.maximum(z, 0.0)
        log_betas.append(z + log_keep)
        if laters is None:
            log_keep = jnp.where(strict_lower, log_keep, 0.0)
        hi = log_keep.astype(BF16)
        lo = (log_keep - hi.astype(F32)).astype(BF16)
        splits.append(jnp.concatenate([hi, lo], axis=1))
    sums = [_dot(s, suffix) for s in splits]
    ws, new_laters = [], []
    for n, (log_beta, s) in enumerate(zip(log_betas, sums)):
        if laters is None:
            w = jnp.where(strict_lower, jnp.exp(log_beta + s[:, :BLK]), 0.0)
            new_laters.append(s[:, BLK:])
        else:
            w = jnp.exp(log_beta + s[:, :BLK] + laters[n])
            new_laters.append(laters[n] + s[:, BLK:])
        ws.append(w.astype(BF16))
    return [_unstack_pair(_dot(w, v)) for w, v in zip(ws, vs)], new_laters


def _stick_schedule():
    q_blk, slot, k_blk, first_step = [], [], [], [0, 0]
    for dist in range(1, N_BLK):
        group = [(i, i, i - dist) for i in range(dist, N_BLK)]
        group += [(0, N_BLK, 0)] * (-len(group) % STICK_UNROLL)
        for qi, sl, kj in group:
            q_blk.append(qi)
            slot.append(sl)
            k_blk.append(kj)
        first_step.append(len(q_blk) // STICK_UNROLL)
    return tuple(np.asarray(t, np.int32) for t in (q_blk, slot, k_blk, first_step))


def _attn_c_kernel(qblk_ref, slot_ref, kblk_ref, step_ref, qkv_ref, suffix_ref, o_ref, later_s, acc_s):
    n_pairs = C_HEADS // 2
    t_idx = lax.broadcasted_iota(jnp.int32, (2 * BLK, BLK), 0) % BLK
    s_idx = lax.broadcasted_iota(jnp.int32, (2 * BLK, BLK), 1)
    strict_lower = s_idx < t_idx

    def cols(part, p):
        return slice(part * C_WIDTH + p * LANES, part * C_WIDTH + (p + 1) * LANES)

    def rows_of(blk):
        return pl.ds(pl.multiple_of(blk * BLK, BLK), BLK)

    for p in range(n_pairs):
        later_s[p, N_BLK] = jnp.zeros((2 * BLK, BLK), F32)
        acc_s[p, N_BLK] = jnp.zeros((BLK, LANES), F32)

    def diag(n, carry):
        where = [(p, n * STICK_UNROLL + u) for u in range(STICK_UNROLL) for p in range(n_pairs)]
        parts, laters = _stick_blocks(
            [_stack_pair(qkv_ref[rows_of(i), cols(0, p)]) for p, i in where],
            [qkv_ref[rows_of(i), cols(1, p)] for p, i in where],
            [qkv_ref[rows_of(i), cols(2, p)] for p, i in where],
            suffix_ref[...], None, strict_lower)
        for (p, i), part, later in zip(where, parts, laters):
            acc_s[p, i] = part
            later_s[p, i] = later
        return carry

    lax.fori_loop(0, N_BLK // STICK_UNROLL, diag, 0)

    def off_diag(n, carry):
        where = []
        for u in range(STICK_UNROLL):
            e = n * STICK_UNROLL + u
            q_rows, k_rows, sl = rows_of(qblk_ref[e]), rows_of(kblk_ref[e]), slot_ref[e]
            where += [(p, sl, q_rows, k_rows) for p in range(n_pairs)]
        accs = [acc_s[p, sl] for p, sl, _, _ in where]
        parts, laters = _stick_blocks(
            [_stack_pair(qkv_ref[q_rows, cols(0, p)]) for p, _, q_rows, _ in where],
            [qkv_ref[k_rows, cols(1, p)] for p, _, _, k_rows in where],
            [qkv_ref[k_rows, cols(2, p)] for p, _, _, k_rows in where],
            suffix_ref[...], [later_s[p, sl] for p, sl, _, _ in where], None)
        for (p, sl, _, _), acc, part, later in zip(where, accs, parts, laters):
            acc_s[p, sl] = acc + part
            later_s[p, sl] = later
        return carry

    def group(state):
        dist, _ = state
        lax.fori_loop(step_ref[dist], step_ref[dist + 1], off_diag, 0)

        def slot_max(i, mx):
            for p in range(n_pairs):
                mx = jnp.maximum(mx, later_s[p, i])
            return mx

        mx = lax.fori_loop(dist + 1, N_BLK, slot_max, jnp.full((2 * BLK, BLK), -jnp.inf, F32))
        return dist + 1, jnp.logical_not(jnp.max(mx) < STICK_DEAD)

    lax.while_loop(lambda state: (state[0] < N_BLK) & state[1], group, (jnp.int32(1), jnp.bool_(True)))

    def emit(i, carry):
        for p in range(n_pairs):
            o_ref[rows_of(i), p * LANES:(p + 1) * LANES] = acc_s[p, i].astype(BF16)
        return carry

    lax.fori_loop(0, N_BLK, emit, 0)


def _attn_c(qkv):
    suffix = jnp.asarray(_suffix_matrix(), BF16)
    schedule = [jnp.asarray(t) for t in _stick_schedule()]
    smem = pl.BlockSpec(memory_space=pltpu.SMEM)
    return pl.pallas_call(
        _attn_c_kernel,
        grid=(BATCH,),
        in_specs=[
            smem, smem, smem, smem,
            pl.BlockSpec((SEQ, QKV_W), lambda b: (b, 0)),
            pl.BlockSpec((2 * BLK, 2 * BLK), lambda b: (0, 0)),
        ],
        out_specs=pl.BlockSpec((SEQ, C_WIDTH), lambda b: (b, 0)),
        out_shape=jax.ShapeDtypeStruct((ROWS, C_WIDTH), BF16),
        scratch_shapes=[
            pltpu.VMEM((C_HEADS // 2, N_BLK + 1, 2 * BLK, BLK), F32),
            pltpu.VMEM((C_HEADS // 2, N_BLK + 1, BLK, LANES), F32),
        ],
        compiler_params=_params("arbitrary"),
        name="attn_stick",
    )(*schedule, qkv, suffix)


def _merge_kernel(x_ref, oa_ref, ob_ref, oc_ref, gpre_ref, wg_ref, bg_ref, wa_ref, wb_ref, wc_ref,
                  wo_ref, gpost_ref, y_ref):
    xv = x_ref[...]
    branches = [_dot(o_ref[...], w_ref[...])
                for o_ref, w_ref in ((oa_ref, wa_ref), (ob_ref, wb_ref), (oc_ref, wc_ref))]
    h = _rms(xv, gpre_ref[...]).astype(BF16)
    merged = None
    for n, branch in enumerate(branches):
        cols = slice(n * D_MODEL, (n + 1) * D_MODEL)
        term = jax.nn.sigmoid(_dot(h, wg_ref[:, cols]) + bg_ref[:, cols]) * branch
        merged = term if merged is None else merged + term
    y = _dot(merged.astype(BF16), wo_ref[...])
    y_ref[...] = xv + _rms(y, gpost_ref[...])


def _merge(layer, x2, o_a, o_b, o_c, gpre, w_gate, b_gate, w_a, w_b, w_c, w_o, gpost):
    def rows(width):
        return pl.BlockSpec((TM_PROJ, width), lambda i: (i, 0))

    def whole(shape):
        return _layer_spec(layer, shape)

    return pl.pallas_call(
        _merge_kernel,
        grid=(ROWS // TM_PROJ,),
        in_specs=[
            rows(D_MODEL), rows(A_WIDTH), rows(B_WIDTH), rows(C_WIDTH),
            whole((1, D_MODEL)), whole((D_MODEL, GATE_COLS)), whole((1, GATE_COLS)),
            whole((A_WIDTH, D_MODEL)), whole((B_WIDTH, D_MODEL)), whole((C_WIDTH, D_MODEL)),
            whole((D_MODEL, D_MODEL)), whole((1, D_MODEL)),
        ],
        out_specs=rows(D_MODEL),
        out_shape=jax.ShapeDtypeStruct((ROWS, D_MODEL), F32),
        compiler_params=_params("arbitrary"),
        name="merge_out_proj",
    )(x2, o_a, o_b, o_c, gpre, w_gate, b_gate, w_a, w_b, w_c, w_o, gpost)


def _gelu_tanh_doubled(x):
    k = math.sqrt(2.0 / math.pi)
    return x * (1.0 + jnp.tanh(x * (k + (k * 0.044715) * (x * x))))


def _ffn_kernel(x_ref, gpre_ref, wup_ref, cw_ref, cb_ref, wd_ref, gpost_ref, y_ref, xs_s, ys_s, tail_s):
    i = pl.program_id(0)
    n_chunks = D_FF // TN_FFN
    steps = TM_FFN // SUBLANES
    n_lane_tiles = D_MODEL // LANES
    def strided_rows(v):
        s, j0 = divmod(v * SUBLANES, steps)
        return pl.ds(j0 * SUBLANES + s, SUBLANES, stride=SUBLANES)

    for lt in range(n_lane_tiles):
        for v in range(steps):
            xs_s[lt, strided_rows(v), :] = x_ref[v * SUBLANES:(v + 1) * SUBLANES, lt * LANES:(lt + 1) * LANES]
    xv = jnp.concatenate([xs_s[lt] for lt in range(n_lane_tiles)], axis=1)
    h = _rms(xv, gpre_ref[...]).astype(BF16)
    seq_start = (i % (SEQ // TM_FFN)) == 0

    def up(c):
        return [_dot(h, wup_ref[:, half * D_FF + c * TN_FFN:half * D_FF + (c + 1) * TN_FFN])
                for half in range(2)]

    def conv(u, c, half):
        cols = slice(half * D_FF + c * TN_FFN, half * D_FF + (c + 1) * TN_FFN)
        last2 = u[TM_FFN - 2 * SUBLANES:, :]
        prev2 = jnp.where(seq_start, 0.0, tail_s[2 * c + half])
        tail_s[2 * c + half] = last2
        wrapped = [jnp.concatenate([prev2[g * SUBLANES + SUBLANES - 1:(g + 1) * SUBLANES],
                                    last2[g * SUBLANES:(g + 1) * SUBLANES - 1]], axis=0) for g in range(2)]
        back1 = jnp.concatenate([wrapped[1], u[:TM_FFN - SUBLANES]], axis=0)
        back2 = jnp.concatenate([wrapped[0], wrapped[1], u[:TM_FFN - 2 * SUBLANES]], axis=0)
        return cw_ref[2:3, cols] * u + cw_ref[1:2, cols] * back1 + cw_ref[0:1, cols] * back2 + cb_ref[:, cols]

    acc = None
    ups = [up(c) for c in range(FFN_AHEAD)]
    for c in range(n_chunks):
        if c + FFN_AHEAD < n_chunks:
            ups.append(up(c + FFN_AHEAD))
        gate_u, val_u = ups.pop(0)
        act = (_gelu_tanh_doubled(conv(gate_u, c, 0)) * conv(val_u, c, 1)).astype(BF16)
        part = _dot(act, wd_ref[c * TN_FFN:(c + 1) * TN_FFN, :])
        acc = part if acc is None else acc + part
    y = xv + _rms(acc, gpost_ref[...])
    for lt in range(n_lane_tiles):
        ys_s[lt] = y[:, lt * LANES:(lt + 1) * LANES]
        for v in range(steps):
            y_ref[v * SUBLANES:(v + 1) * SUBLANES, lt * LANES:(lt + 1) * LANES] = ys_s[lt, strided_rows(v), :]


def _ffn(layer, x2, gpre, w_up, conv_w, conv_b, w_down, gpost):
    n_chunks = D_FF // TN_FFN

    def whole(shape):
        return pl.BlockSpec((None,) + shape, lambda i: (layer, 0, 0), pipeline_mode=pl.Buffered(1))

    return pl.pallas_call(
        _ffn_kernel,
        grid=(ROWS // TM_FFN,),
        in_specs=[
            pl.BlockSpec((TM_FFN, D_MODEL), lambda i: (i, 0)),
            whole((1, D_MODEL)),
            whole((D_MODEL, 2 * D_FF)),
            whole((CONV_WIDTH, 2 * D_FF)),
            whole((1, 2 * D_FF)),
            whole((D_FF, D_MODEL)),
            whole((1, D_MODEL)),
        ],
        out_specs=pl.BlockSpec((TM_FFN, D_MODEL), lambda i: (i, 0)),
        out_shape=jax.ShapeDtypeStruct((ROWS, D_MODEL), F32),
        scratch_shapes=[
            pltpu.VMEM((D_MODEL // LANES, TM_FFN, LANES), F32),
            pltpu.VMEM((D_MODEL // LANES, TM_FFN, LANES), F32),
            pltpu.VMEM((2 * n_chunks, 2 * SUBLANES, TN_FFN), F32),
        ],
        compiler_params=_params("arbitrary"),
        name="conv_ffn",
    )(x2, gpre, w_up, conv_w, conv_b, w_down, gpost)


def _qkv_weight(w_in):
    rows = w_in.shape[:-1]
    a = w_in[..., :A_QKV_COLS].reshape(rows + (3, len(A_GROUPS), A_WIDTH))
    a = a * jnp.asarray([SCALE, 1.0, 1.0], F32)[:, None, None]
    a = jnp.swapaxes(a, -3, -2).reshape(rows + (len(A_GROUPS), QKV_W))
    kv = w_in[..., OFF_B_KV:OFF_C].reshape(rows + (2 * B_KV_HEADS, 1, HEAD_DIM))
    kv = jnp.broadcast_to(kv, rows + (2 * B_KV_HEADS, 2, HEAD_DIM)).reshape(rows + (-1,))
    b = jnp.concatenate([w_in[..., OFF_B_Q:OFF_B_KV] * SCALE, kv], axis=-1)
    c = jnp.concatenate([w_in[..., OFF_C:OFF_C + C_WIDTH] * SCALE, w_in[..., OFF_C + C_WIDTH:OFF_GATE]],
                        axis=-1)
    return jnp.concatenate([a[..., 0, :], b, c, a[..., 1, :], a[..., 2, :]], axis=-1).astype(BF16)


def kernel(x, rel_bias, attn_pre_norm, w_in, b_gate, sinks, w_br_a, w_br_b, w_br_c, w_out,
           attn_post_norm, ffn_pre_norm, w_up, conv_w, conv_b, w_down, ffn_post_norm):
    assert x.shape == (BATCH, SEQ, D_MODEL) and x.dtype == F32
    bias = _band_bias(rel_bias)
    bias_a, bias_b = bias[:N_A_GROUP_HEADS], bias[N_A_GROUP_HEADS:]
    x2 = x.reshape(ROWS, D_MODEL)

    def row_param(p):
        return p.reshape(DEPTH, 1, -1)

    w_qkv = _qkv_weight(w_in)
    w_gate = w_in[..., OFF_GATE:].astype(BF16)
    w_a, w_b, w_c, w_o = (w.astype(BF16) for w in (w_br_a, w_br_b, w_br_c, w_out))
    w_up16, w_down16 = w_up.astype(BF16), (w_down * 0.5).astype(BF16)
    attn_pre, attn_post, ffn_pre, ffn_post, gate_b, conv_bias = (
        row_param(p) for p in (attn_pre_norm, attn_post_norm, ffn_pre_norm, ffn_post_norm, b_gate, conv_b))
    for layer in range(DEPTH):
        a0, a1, a2, bq, cq = _in_proj(layer, x2, attn_pre, w_qkv)
        o_a = _attn_a(a0, a1, a2, bias_a)
        o_b = _attn_b(bq, bias_b, sinks[layer])
        o_c = _attn_c(cq)
        x2 = _merge(layer, x2, o_a, o_b, o_c, attn_pre, w_gate, gate_b, w_a, w_b, w_c, w_o, attn_post)
        x2 = _ffn(layer, x2, ffn_pre, w_up16, conv_w, conv_bias, w_down16, ffn_post)
    return x2.reshape(BATCH, SEQ, D_MODEL)
```

```python
import functools
import math

import numpy as np
import jax
import jax.numpy as jnp
from jax import lax
from jax.experimental import pallas as pl
from jax.experimental.pallas import tpu as pltpu

D_MODEL = 1024
BATCH = 8
SEQ = 2048
DEPTH = 2
HEAD_DIM = 64
BLK = 128
A_GROUPS = ((128, 1), (512, 4), (2048, 16))
A_HEADS = 4
B_Q_HEADS = 8
B_KV_HEADS = 2
B_WINDOW = 128
C_HEADS = 4
N_BRANCH = 3
NUM_BUCKETS = 32
MAX_DISTANCE = 2048
D_FF = 4 * D_MODEL
CONV_WIDTH = 3
EPS = 1e-6
SCALE = HEAD_DIM ** -0.5

A_WIDTH = A_HEADS * HEAD_DIM
B_WIDTH = B_Q_HEADS * HEAD_DIM
C_WIDTH = C_HEADS * HEAD_DIM
B_GROUP = B_Q_HEADS // B_KV_HEADS
N_A_GROUP_HEADS = len(A_GROUPS) * A_HEADS
N_BIAS_HEADS = N_A_GROUP_HEADS + B_Q_HEADS
A_QKV_COLS = 3 * N_A_GROUP_HEADS * HEAD_DIM
OFF_B_Q = A_QKV_COLS
OFF_B_KV = OFF_B_Q + B_WIDTH
OFF_C = OFF_B_KV + 2 * B_KV_HEADS * HEAD_DIM
OFF_GATE = OFF_C + 3 * C_WIDTH
GATE_COLS = N_BRANCH * D_MODEL

ROWS = BATCH * SEQ
N_BLK = SEQ // BLK
QKV_W = 3 * A_WIDTH
LANES = 128
B_COLS = B_WIDTH + 2 * B_KV_HEADS * LANES
NAT_COLS = QKV_W + B_COLS + QKV_W
NEG = -1e30
STICK_DEAD = -104.0
LOG2_E = 1.4426950408889634
LN_2 = 0.6931471805599453

V7X_VMEM_BYTES = 64 * 1024 * 1024
VMEM_LIMIT = V7X_VMEM_BYTES * 7 // 8

TM_PROJ = 512
TM_FFN = 512
TN_FFN = 512
FFN_AHEAD = 2
SUBLANES = 8
BAND_UNROLL = 4
STICK_UNROLL = 4

BF16 = jnp.bfloat16
F32 = jnp.float32


def _params(*sem):
    return pltpu.CompilerParams(dimension_semantics=sem, vmem_limit_bytes=VMEM_LIMIT)


def _dot(a, b):
    return jnp.dot(a, b, preferred_element_type=F32)


def _dot_nt(a, b):
    return lax.dot_general(a, b, (((1,), (1,)), ((), ())), preferred_element_type=F32)


def _rms(xv, gain):
    y = xv * lax.rsqrt(jnp.mean(xv * xv, axis=-1, keepdims=True) + EPS)
    return y * gain


def _bucket_constants():
    a = np.arange(BLK)[:, None]
    b = np.arange(2 * BLK)[None, :]
    dist = np.maximum(a + BLK - b, 0)
    max_exact = NUM_BUCKETS // 2
    out = []
    for _, d in A_GROUPS:
        n = dist * d
        nf = np.maximum(n, 1).astype(np.float64)
        large = max_exact + (np.log(nf / max_exact) / math.log(MAX_DISTANCE / max_exact)
                             * (NUM_BUCKETS - max_exact)).astype(np.int64)
        large = np.minimum(large, NUM_BUCKETS - 1)
        out.append(np.where(n < max_exact, n, large))
    return np.stack(out).astype(np.int32)


def _bias_kernel(tab_ref, bucket_ref, out_ref, *, head0, max_dist):
    step = pl.program_id(0)
    a = lax.broadcasted_iota(jnp.int32, (BLK, 2 * BLK), 0)
    b = lax.broadcasted_iota(jnp.int32, (BLK, 2 * BLK), 1)
    dist = a + BLK - b
    valid = (dist >= 0) & (dist <= max_dist)
    bk = bucket_ref[...]
    for n in range(A_HEADS):
        acc = jnp.zeros((BLK, 2 * BLK), F32)
        for k in range(NUM_BUCKETS):
            acc = jnp.where(bk == k, tab_ref[k, head0 + step * A_HEADS + n], acc)
        out_ref[0, n] = jnp.where(valid, acc, NEG)
        out_ref[1, n] = jnp.where(valid & (b >= BLK), acc, NEG)


def _band_bias(rel_bias, head0, n_heads, max_dist, dilated):
    buckets = jnp.asarray(_bucket_constants())
    return pl.pallas_call(
        functools.partial(_bias_kernel, head0=head0, max_dist=max_dist),
        grid=(n_heads // A_HEADS,),
        in_specs=[
            pl.BlockSpec(memory_space=pltpu.SMEM),
            pl.BlockSpec((None, BLK, 2 * BLK), lambda s: (s if dilated else 0, 0, 0)),
        ],
        out_specs=pl.BlockSpec((2, A_HEADS, BLK, 2 * BLK), lambda s: (0, s, 0, 0)),
        out_shape=jax.ShapeDtypeStruct((2, n_heads, BLK, 2 * BLK), F32),
        compiler_params=_params("arbitrary"),
        name="band_bias",
    )(rel_bias, buckets)


def _in_proj_kernel(x_ref, g_ref, w_ref, a0_ref, a1_ref, a2_ref, b_ref, c_ref, ys_s):
    n_lane_tiles = D_MODEL // LANES

    def proj(h, col0, width):
        return _dot(h.astype(BF16), w_ref[:, col0:col0 + width]).astype(BF16)

    y = _rms(x_ref[...], g_ref[...])
    nat = proj(y, 0, NAT_COLS)
    a0_ref[...] = nat[:, :QKV_W]
    b_ref[...] = nat[:, QKV_W:QKV_W + B_COLS]
    c_ref[...] = nat[:, QKV_W + B_COLS:]
    for c in range(n_lane_tiles):
        ys_s[c] = y[:, c * LANES:(c + 1) * LANES]
    for ref, (_, d), col0 in ((a1_ref, A_GROUPS[1], NAT_COLS), (a2_ref, A_GROUPS[2], NAT_COLS + QKV_W)):
        per = TM_PROJ // d
        regrouped = jnp.concatenate(
            [jnp.concatenate([ys_s[c, pl.ds(r, per, stride=d), :] for c in range(n_lane_tiles)], axis=1)
             for r in range(d)], axis=0)
        ref[...] = proj(regrouped, col0, QKV_W).reshape(d, per, QKV_W)


def _layer_spec(layer, shape):
    return pl.BlockSpec((None,) + shape, lambda i: (layer, 0, 0))


def _in_proj(layer, x2, gain, w_qkv):
    tiles_per_seq = SEQ // TM_PROJ
    d1, d2 = A_GROUPS[1][1], A_GROUPS[2][1]

    def nat(width):
        return pl.BlockSpec((TM_PROJ, width), lambda i: (i, 0))

    return pl.pallas_call(
        _in_proj_kernel,
        grid=(ROWS // TM_PROJ,),
        in_specs=[
            pl.BlockSpec((TM_PROJ, D_MODEL), lambda i: (i, 0)),
            _layer_spec(layer, (1, D_MODEL)),
            _layer_spec(layer, (D_MODEL, NAT_COLS + 2 * QKV_W)),
        ],
        out_specs=[
            nat(QKV_W),
            pl.BlockSpec((None, d1, TM_PROJ // d1, QKV_W),
                         lambda i: (i // tiles_per_seq, 0, i % tiles_per_seq, 0)),
            pl.BlockSpec((None, d2, TM_PROJ // d2, QKV_W),
                         lambda i: (i // tiles_per_seq, 0, i % tiles_per_seq, 0)),
            nat(B_COLS),
            nat(QKV_W),
        ],
        out_shape=[
            jax.ShapeDtypeStruct((ROWS, QKV_W), BF16),
            jax.ShapeDtypeStruct((BATCH, d1, SEQ // d1, QKV_W), BF16),
            jax.ShapeDtypeStruct((BATCH, d2, SEQ // d2, QKV_W), BF16),
            jax.ShapeDtypeStruct((ROWS, B_COLS), BF16),
            jax.ShapeDtypeStruct((ROWS, QKV_W), BF16),
        ],
        scratch_shapes=[pltpu.VMEM((D_MODEL // LANES, TM_PROJ, LANES), F32)],
        compiler_params=_params("arbitrary"),
        name="in_proj",
    )(x2, gain, w_qkv)


def _low_lanes(rows):
    return lax.broadcasted_iota(jnp.int32, (rows, LANES), 1) < HEAD_DIM


def _stack_pair(q2):
    low = _low_lanes(BLK)
    zero = jnp.zeros_like(q2)
    return jnp.concatenate([jnp.where(low, q2, zero), jnp.where(low, zero, q2)], axis=0)


def _unstack_pair(o):
    return jnp.where(_low_lanes(BLK), o[:BLK], o[BLK:])


def _band_blocks(qs, kps, kcs, vps, vcs, biases):
    ss = [_dot_nt(q, jnp.concatenate([kp, kc], axis=0)) for q, kp, kc in zip(qs, kps, kcs)]
    ps, stats = [], []
    for s, bias in zip(ss, biases):
        s = s + bias
        m = jnp.max(s, axis=-1, keepdims=True)
        p = jnp.exp(s - m)
        stats.append((m, jnp.sum(p, axis=-1, keepdims=True)))
        ps.append(p.astype(BF16))
    nums = [_dot(p, jnp.concatenate([vp, vc], axis=0)) for p, vp, vc in zip(ps, vps, vcs)]
    return nums, stats


def _attn_a_kernel(a0_ref, a1_ref, a2_ref, bias_ref, o_ref, out_s, lse_s):
    n_pairs = A_HEADS // 2

    def cols(part, p):
        return slice(part * A_WIDTH + p * LANES, part * A_WIDTH + (p + 1) * LANES)

    for g, (ref, (_, d)) in enumerate(zip((a0_ref, a1_ref, a2_ref), A_GROUPS)):
        nb = N_BLK // d

        def step(n, carry, g=g, ref=ref, d=d, nb=nb):
            units, qs, kps, kcs, vps, vcs, biases = [], [], [], [], [], [], []
            for u in range(BAND_UNROLL):
                blk = n * BAND_UNROLL + u
                r = blk // nb
                i = blk % nb
                row_c = pl.multiple_of(i * BLK, BLK)
                row_p = pl.multiple_of(jnp.maximum(i - 1, 0) * BLK, BLK)
                first = jnp.where(i > 0, 0, 1)
                if d == 1:
                    rows = pl.ds(row_c, BLK)
                else:
                    rows = pl.ds(i * (BLK * d) + r, BLK, stride=d)
                for p in range(n_pairs):
                    units.append((p, rows))
                    heads = slice(g * A_HEADS + 2 * p, g * A_HEADS + 2 * p + 2)
                    qs.append(_stack_pair(ref[r, pl.ds(row_c, BLK), cols(0, p)]))
                    kps.append(ref[r, pl.ds(row_p, BLK), cols(1, p)])
                    kcs.append(ref[r, pl.ds(row_c, BLK), cols(1, p)])
                    vps.append(ref[r, pl.ds(row_p, BLK), cols(2, p)])
                    vcs.append(ref[r, pl.ds(row_c, BLK), cols(2, p)])
                    biases.append(bias_ref[first, heads].reshape(2 * BLK, 2 * BLK))
            nums, stats = _band_blocks(qs, kps, kcs, vps, vcs, biases)
            for (p, rows), num, (m, l) in zip(units, nums, stats):
                out_s[g, p, rows, :] = _unstack_pair(num / l)
                lse_s[g, p, rows, :] = _unstack_pair(jnp.broadcast_to(m + jnp.log(l), (2 * BLK, LANES)))
            return carry

        lax.fori_loop(0, N_BLK // BAND_UNROLL, step, 0)

    def combine(t, carry):
        rows = pl.ds(pl.multiple_of(t * BLK, BLK), BLK)
        for p in range(n_pairs):
            l0, l1, l2 = lse_s[0, p, rows, :], lse_s[1, p, rows, :], lse_s[2, p, rows, :]
            mx = jnp.maximum(jnp.maximum(l0, l1), l2)
            e0, e1, e2 = jnp.exp(l0 - mx), jnp.exp(l1 - mx), jnp.exp(l2 - mx)
            top = e0 * out_s[0, p, rows, :] + e1 * out_s[1, p, rows, :] + e2 * out_s[2, p, rows, :]
            o_ref[rows, p * LANES:(p + 1) * LANES] = (top / (e0 + e1 + e2)).astype(BF16)
        return carry

    lax.fori_loop(0, N_BLK, combine, 0)


def _attn_a(a0, a1, a2, bias):
    d1, d2 = A_GROUPS[1][1], A_GROUPS[2][1]
    return pl.pallas_call(
        _attn_a_kernel,
        grid=(BATCH,),
        in_specs=[
            pl.BlockSpec((None, 1, SEQ, QKV_W), lambda b: (b, 0, 0, 0)),
            pl.BlockSpec((None, d1, SEQ // d1, QKV_W), lambda b: (b, 0, 0, 0)),
            pl.BlockSpec((None, d2, SEQ // d2, QKV_W), lambda b: (b, 0, 0, 0)),
            pl.BlockSpec((2, N_A_GROUP_HEADS, BLK, 2 * BLK), lambda b: (0, 0, 0, 0)),
        ],
        out_specs=pl.BlockSpec((SEQ, A_WIDTH), lambda b: (b, 0)),
        out_shape=jax.ShapeDtypeStruct((ROWS, A_WIDTH), BF16),
        scratch_shapes=[
            pltpu.VMEM((len(A_GROUPS), A_HEADS // 2, SEQ, LANES), F32),
            pltpu.VMEM((len(A_GROUPS), A_HEADS // 2, SEQ, LANES), F32),
        ],
        compiler_params=_params("arbitrary"),
        name="attn_dilated",
    )(a0.reshape(BATCH, 1, SEQ, QKV_W), a1, a2, bias)


def _attn_b_kernel(sink_ref, qkv_ref, bias_ref, o_ref):
    def kv_cols(part, kvh):
        start = B_WIDTH + (part * B_KV_HEADS + kvh) * LANES
        return slice(start, start + LANES)

    def step(n, carry):
        blocks, qs, kps, kcs, vps, vcs, biases = [], [], [], [], [], [], []
        for u in range(BAND_UNROLL):
            i = n * BAND_UNROLL + u
            row_c = pl.multiple_of(i * BLK, BLK)
            row_p = pl.multiple_of(jnp.maximum(i - 1, 0) * BLK, BLK)
            first = jnp.where(i > 0, 0, 1)
            blocks.append(row_c)
            for kvh in range(B_KV_HEADS):
                heads = slice(kvh * B_GROUP, (kvh + 1) * B_GROUP)
                tiles = range(kvh * B_GROUP // 2, (kvh + 1) * B_GROUP // 2)
                qs.append(jnp.concatenate(
                    [_stack_pair(qkv_ref[pl.ds(row_c, BLK), t * LANES:(t + 1) * LANES]) for t in tiles],
                    axis=0))
                kps.append(qkv_ref[pl.ds(row_p, BLK), kv_cols(0, kvh)])
                kcs.append(qkv_ref[pl.ds(row_c, BLK), kv_cols(0, kvh)])
                vps.append(qkv_ref[pl.ds(row_p, BLK), kv_cols(1, kvh)])
                vcs.append(qkv_ref[pl.ds(row_c, BLK), kv_cols(1, kvh)])
                biases.append(bias_ref[first, heads].reshape(B_GROUP * BLK, 2 * BLK))
        nums, stats = _band_blocks(qs, kps, kcs, vps, vcs, biases)
        for u, row_c in enumerate(blocks):
            for kvh in range(B_KV_HEADS):
                num = nums[u * B_KV_HEADS + kvh]
                m, l = stats[u * B_KV_HEADS + kvh]
                scaled = []
                for gq in range(B_GROUP):
                    rows = slice(gq * BLK, (gq + 1) * BLK)
                    sink = sink_ref[0, kvh * B_GROUP + gq]
                    scaled.append(num[rows] / (l[rows] + jnp.exp(sink - m[rows])))
                for t in range(B_GROUP // 2):
                    tile = kvh * B_GROUP // 2 + t
                    o_ref[pl.ds(row_c, BLK), tile * LANES:(tile + 1) * LANES] = _unstack_pair(
                        jnp.concatenate(scaled[2 * t:2 * t + 2], axis=0)).astype(BF16)
        return carry

    lax.fori_loop(0, N_BLK // BAND_UNROLL, step, 0)


def _attn_b(qkv, bias, sinks):
    return pl.pallas_call(
        _attn_b_kernel,
        grid=(BATCH,),
        in_specs=[
            pl.BlockSpec(memory_space=pltpu.SMEM),
            pl.BlockSpec((SEQ, B_COLS), lambda b: (b, 0)),
            pl.BlockSpec((2, B_Q_HEADS, BLK, 2 * BLK), lambda b: (0, 0, 0, 0)),
        ],
        out_specs=pl.BlockSpec((SEQ, B_WIDTH), lambda b: (b, 0)),
        out_shape=jax.ShapeDtypeStruct((ROWS, B_WIDTH), BF16),
        compiler_params=_params("arbitrary"),
        name="attn_window",
    )(sinks.reshape(1, B_Q_HEADS), qkv, bias)


def _suffix_matrix():
    j = np.arange(2 * BLK)[:, None] % BLK
    s = np.arange(2 * BLK)[None, :]
    return ((s >= BLK) | (j > s)).astype(np.float32)


def _stick_blocks(qs, ks, vs, suffix, laters, strict_lower):
    zs = [_dot_nt(q, k) for q, k in zip(qs, ks)]
    log_betas, splits = [], []
    for z in zs:
        neg_soft = jnp.log2(1.0 + jnp.exp2(jnp.abs(z) * -LOG2_E)) * -LN_2
        log_keep = neg_soft - jnp.maximum(z, 0.0)
        log_betas.append(z + log_keep)
        if laters is None:
            log_keep = jnp.where(strict_lower, log_keep, 0.0)
        hi = log_keep.astype(BF16)
        lo = (log_keep - hi.astype(F32)).astype(BF16)
        splits.append(jnp.concatenate([hi, lo], axis=1))
    sums = [_dot(s, suffix) for s in splits]
    ws, new_laters = [], []
    for n, (log_beta, s) in enumerate(zip(log_betas, sums)):
        if laters is None:
            w = jnp.where(strict_lower, jnp.exp(log_beta + s[:, :BLK]), 0.0)
            new_laters.append(s[:, BLK:])
        else:
            w = jnp.exp(log_beta + s[:, :BLK] + laters[n])
            new_laters.append(laters[n] + s[:, BLK:])
        ws.append(w.astype(BF16))
    return [_unstack_pair(_dot(w, v)) for w, v in zip(ws, vs)], new_laters


def _stick_schedule():
    q_blk, slot, k_blk, first_step = [], [], [], [0, 0]
    for dist in range(1, N_BLK):
        group = [(i, i, i - dist) for i in range(dist, N_BLK)]
        group += [(0, N_BLK, 0)] * (-len(group) % STICK_UNROLL)
        for qi, sl, kj in group:
            q_blk.append(qi)
            slot.append(sl)
            k_blk.append(kj)
        first_step.append(len(q_blk) // STICK_UNROLL)
    return tuple(np.asarray(t, np.int32) for t in (q_blk, slot, k_blk, first_step))


def _attn_c_kernel(qblk_ref, slot_ref, kblk_ref, step_ref, qkv_ref, suffix_ref, o_ref, later_s, acc_s):
    n_pairs = C_HEADS // 2
    t_idx = lax.broadcasted_iota(jnp.int32, (2 * BLK, BLK), 0) % BLK
    s_idx = lax.broadcasted_iota(jnp.int32, (2 * BLK, BLK), 1)
    strict_lower = s_idx < t_idx

    def cols(part, p):
        return slice(part * C_WIDTH + p * LANES, part * C_WIDTH + (p + 1) * LANES)

    def rows_of(blk):
        return pl.ds(pl.multiple_of(blk * BLK, BLK), BLK)

    for p in range(n_pairs):
        later_s[p, N_BLK] = jnp.zeros((2 * BLK, BLK), F32)
        acc_s[p, N_BLK] = jnp.zeros((BLK, LANES), F32)

    def diag(n, carry):
        where = [(p, n * STICK_UNROLL + u) for u in range(STICK_UNROLL) for p in range(n_pairs)]
        parts, laters = _stick_blocks(
            [_stack_pair(qkv_ref[rows_of(i), cols(0, p)]) for p, i in where],
            [qkv_ref[rows_of(i), cols(1, p)] for p, i in where],
            [qkv_ref[rows_of(i), cols(2, p)] for p, i in where],
            suffix_ref[...], None, strict_lower)
        for (p, i), part, later in zip(where, parts, laters):
            acc_s[p, i] = part
            later_s[p, i] = later
        return carry

    lax.fori_loop(0, N_BLK // STICK_UNROLL, diag, 0)

    def off_diag(n, carry):
        where = []
        for u in range(STICK_UNROLL):
            e = n * STICK_UNROLL + u
            q_rows, k_rows, sl = rows_of(qblk_ref[e]), rows_of(kblk_ref[e]), slot_ref[e]
            where += [(p, sl, q_rows, k_rows) for p in range(n_pairs)]
        accs = [acc_s[p, sl] for p, sl, _, _ in where]
        parts, laters = _stick_blocks(
            [_stack_pair(qkv_ref[q_rows, cols(0, p)]) for p, _, q_rows, _ in where],
            [qkv_ref[k_rows, cols(1, p)] for p, _, _, k_rows in where],
            [qkv_ref[k_rows, cols(2, p)] for p, _, _, k_rows in where],
            suffix_ref[...], [later_s[p, sl] for p, sl, _, _ in where], None)
        for (p, sl, _, _), acc, part, later in zip(where, accs, parts, laters):
            acc_s[p, sl] = acc + part
            later_s[p, sl] = later
        return carry

    def group(state):
        dist, _ = state
        lax.fori_loop(step_ref[dist], step_ref[dist + 1], off_diag, 0)

        def slot_max(i, mx):
            for p in range(n_pairs):
                mx = jnp.maximum(mx, later_s[p, i])
            return mx

        mx = lax.fori_loop(dist + 1, N_BLK, slot_max, jnp.full((2 * BLK, BLK), -jnp.inf, F32))
        return dist + 1, jnp.logical_not(jnp.max(mx) < STICK_DEAD)

    lax.while_loop(lambda state: (state[0] < N_BLK) & state[1], group, (jnp.int32(1), jnp.bool_(True)))

    def emit(i, carry):
        for p in range(n_pairs):
            o_ref[rows_of(i), p * LANES:(p + 1) * LANES] = acc_s[p, i].astype(BF16)
        return carry

    lax.fori_loop(0, N_BLK, emit, 0)


def _attn_c(qkv):
    suffix = jnp.asarray(_suffix_matrix(), BF16)
    schedule = [jnp.asarray(t) for t in _stick_schedule()]
    smem = pl.BlockSpec(memory_space=pltpu.SMEM)
    return pl.pallas_call(
        _attn_c_kernel,
        grid=(BATCH,),
        in_specs=[
            smem, smem, smem, smem,
            pl.BlockSpec((SEQ, QKV_W), lambda b: (b, 0)),
            pl.BlockSpec((2 * BLK, 2 * BLK), lambda b: (0, 0)),
        ],
        out_specs=pl.BlockSpec((SEQ, C_WIDTH), lambda b: (b, 0)),
        out_shape=jax.ShapeDtypeStruct((ROWS, C_WIDTH), BF16),
        scratch_shapes=[
            pltpu.VMEM((C_HEADS // 2, N_BLK + 1, 2 * BLK, BLK), F32),
            pltpu.VMEM((C_HEADS // 2, N_BLK + 1, BLK, LANES), F32),
        ],
        compiler_params=_params("arbitrary"),
        name="attn_stick",
    )(*schedule, qkv, suffix)


def _merge_kernel(x_ref, oa_ref, ob_ref, oc_ref, gpre_ref, wg_ref, bg_ref, wa_ref, wb_ref, wc_ref,
                  wo_ref, gpost_ref, y_ref):
    xv = x_ref[...]
    h = _rms(xv, gpre_ref[...]).astype(BF16)
    merged = None
    for n, (o_ref, w_ref) in enumerate(((oa_ref, wa_ref), (ob_ref, wb_ref), (oc_ref, wc_ref))):
        cols = slice(n * D_MODEL, (n + 1) * D_MODEL)
        gate = jax.nn.sigmoid(_dot(h, wg_ref[:, cols]) + bg_ref[:, cols])
        term = gate * _dot(o_ref[...], w_ref[...])
        merged = term if merged is None else merged + term
    y = _dot(merged.astype(BF16), wo_ref[...])
    y_ref[...] = xv + _rms(y, gpost_ref[...])


def _merge(layer, x2, o_a, o_b, o_c, gpre, w_gate, b_gate, w_a, w_b, w_c, w_o, gpost):
    def rows(width):
        return pl.BlockSpec((TM_PROJ, width), lambda i: (i, 0))

    def whole(shape):
        return _layer_spec(layer, shape)

    return pl.pallas_call(
        _merge_kernel,
        grid=(ROWS // TM_PROJ,),
        in_specs=[
            rows(D_MODEL), rows(A_WIDTH), rows(B_WIDTH), rows(C_WIDTH),
            whole((1, D_MODEL)), whole((D_MODEL, GATE_COLS)), whole((1, GATE_COLS)),
            whole((A_WIDTH, D_MODEL)), whole((B_WIDTH, D_MODEL)), whole((C_WIDTH, D_MODEL)),
            whole((D_MODEL, D_MODEL)), whole((1, D_MODEL)),
        ],
        out_specs=rows(D_MODEL),
        out_shape=jax.ShapeDtypeStruct((ROWS, D_MODEL), F32),
        compiler_params=_params("arbitrary"),
        name="merge_out_proj",
    )(x2, o_a, o_b, o_c, gpre, w_gate, b_gate, w_a, w_b, w_c, w_o, gpost)


def _gelu_tanh_doubled(x):
    k = math.sqrt(2.0 / math.pi)
    return x * (1.0 + jnp.tanh(x * (k + (k * 0.044715) * (x * x))))


def _ffn_kernel(x_ref, gpre_ref, wup_ref, cw_ref, cb_ref, wd_ref, gpost_ref, y_ref, xs_s, ys_s, tail_s):
    i = pl.program_id(0)
    n_chunks = D_FF // TN_FFN
    steps = TM_FFN // SUBLANES
    n_lane_tiles = D_MODEL // LANES

    def strided_rows(v):
        s, j0 = divmod(v * SUBLANES, steps)
        return pl.ds(j0 * SUBLANES + s, SUBLANES, stride=SUBLANES)

    for lt in range(n_lane_tiles):
        for v in range(steps):
            xs_s[lt, strided_rows(v), :] = x_ref[v * SUBLANES:(v + 1) * SUBLANES, lt * LANES:(lt + 1) * LANES]
    xv = jnp.concatenate([xs_s[lt] for lt in range(n_lane_tiles)], axis=1)
    h = _rms(xv, gpre_ref[...]).astype(BF16)
    seq_start = (i % (SEQ // TM_FFN)) == 0

    def up(c):
        return [_dot(h, wup_ref[:, half * D_FF + c * TN_FFN:half * D_FF + (c + 1) * TN_FFN])
                for half in range(2)]

    def conv(u, c, half):
        cols = slice(half * D_FF + c * TN_FFN, half * D_FF + (c + 1) * TN_FFN)
        last2 = u[TM_FFN - 2 * SUBLANES:, :]
        prev2 = jnp.where(seq_start, 0.0, tail_s[2 * c + half])
        tail_s[2 * c + half] = last2
        wrapped = [jnp.concatenate([prev2[g * SUBLANES + SUBLANES - 1:(g + 1) * SUBLANES],
                                    last2[g * SUBLANES:(g + 1) * SUBLANES - 1]], axis=0) for g in range(2)]
        back1 = jnp.concatenate([wrapped[1], u[:TM_FFN - SUBLANES]], axis=0)
        back2 = jnp.concatenate([wrapped[0], wrapped[1], u[:TM_FFN - 2 * SUBLANES]], axis=0)
        return cw_ref[2:3, cols] * u + cw_ref[1:2, cols] * back1 + cw_ref[0:1, cols] * back2 + cb_ref[:, cols]

    acc = None
    ups = [up(c) for c in range(FFN_AHEAD)]
    for c in range(n_chunks):
        if c + FFN_AHEAD < n_chunks:
            ups.append(up(c + FFN_AHEAD))
        gate_u, val_u = ups.pop(0)
        act = (_gelu_tanh_doubled(conv(gate_u, c, 0)) * conv(val_u, c, 1)).astype(BF16)
        part = _dot(act, wd_ref[c * TN_FFN:(c + 1) * TN_FFN, :])
        acc = part if acc is None else acc + part
    y = xv + _rms(acc, gpost_ref[...])
    for lt in range(n_lane_tiles):
        ys_s[lt] = y[:, lt * LANES:(lt + 1) * LANES]
        for v in range(steps):
            y_ref[v * SUBLANES:(v + 1) * SUBLANES, lt * LANES:(lt + 1) * LANES] = ys_s[lt, strided_rows(v), :]


def _ffn(layer, x2, gpre, w_up, conv_w, conv_b, w_down, gpost):
    n_chunks = D_FF // TN_FFN

    def whole(shape):
        return pl.BlockSpec((None,) + shape, lambda i: (layer, 0, 0), pipeline_mode=pl.Buffered(1))

    return pl.pallas_call(
        _ffn_kernel,
        grid=(ROWS // TM_FFN,),
        in_specs=[
            pl.BlockSpec((TM_FFN, D_MODEL), lambda i: (i, 0)),
            whole((1, D_MODEL)),
            whole((D_MODEL, 2 * D_FF)),
            whole((CONV_WIDTH, 2 * D_FF)),
            whole((1, 2 * D_FF)),
            whole((D_FF, D_MODEL)),
            whole((1, D_MODEL)),
        ],
        out_specs=pl.BlockSpec((TM_FFN, D_MODEL), lambda i: (i, 0)),
        out_shape=jax.ShapeDtypeStruct((ROWS, D_MODEL), F32),
        scratch_shapes=[
            pltpu.VMEM((D_MODEL // LANES, TM_FFN, LANES), F32),
            pltpu.VMEM((D_MODEL // LANES, TM_FFN, LANES), F32),
            pltpu.VMEM((2 * n_chunks, 2 * SUBLANES, TN_FFN), F32),
        ],
        compiler_params=_params("arbitrary"),
        name="conv_ffn",
    )(x2, gpre, w_up, conv_w, conv_b, w_down, gpost)


def _qkv_weight(w_in):
    rows = w_in.shape[:-1]
    a = w_in[..., :A_QKV_COLS].reshape(rows + (3, len(A_GROUPS), A_WIDTH))
    a = a * jnp.asarray([SCALE, 1.0, 1.0], w_in.dtype)[:, None, None]
    a = jnp.swapaxes(a, -3, -2).reshape(rows + (len(A_GROUPS), QKV_W))
    kv = w_in[..., OFF_B_KV:OFF_C].reshape(rows + (2 * B_KV_HEADS, 1, HEAD_DIM))
    kv = jnp.broadcast_to(kv, rows + (2 * B_KV_HEADS, 2, HEAD_DIM)).reshape(rows + (-1,))
    b = jnp.concatenate([w_in[..., OFF_B_Q:OFF_B_KV] * SCALE, kv], axis=-1)
    c = jnp.concatenate([w_in[..., OFF_C:OFF_C + C_WIDTH] * SCALE, w_in[..., OFF_C + C_WIDTH:OFF_GATE]],
                        axis=-1)
    return jnp.concatenate([a[..., 0, :], b, c, a[..., 1, :], a[..., 2, :]], axis=-1).astype(BF16)


def kernel(x, rel_bias, attn_pre_norm, w_in, b_gate, sinks, w_br_a, w_br_b, w_br_c, w_out,
           attn_post_norm, ffn_pre_norm, w_up, conv_w, conv_b, w_down, ffn_post_norm):
    assert x.shape == (BATCH, SEQ, D_MODEL) and x.dtype == F32
    bias_a = _band_bias(rel_bias, 0, N_A_GROUP_HEADS, BLK, True)
    bias_b = _band_bias(rel_bias, N_A_GROUP_HEADS, B_Q_HEADS, B_WINDOW - 1, False)
    x2 = x.reshape(ROWS, D_MODEL)

    def row_param(p):
        return p.reshape(DEPTH, 1, -1)

    w_in16 = w_in.astype(BF16)
    w_qkv = _qkv_weight(w_in16)
    w_gate = w_in16[..., OFF_GATE:]
    w_a, w_b, w_c, w_o = (w.astype(BF16) for w in (w_br_a, w_br_b, w_br_c, w_out))
    w_up16, w_down16 = w_up.astype(BF16), (w_down * 0.5).astype(BF16)
    attn_pre, attn_post, ffn_pre, ffn_post, gate_b, conv_bias = (
        row_param(p) for p in (attn_pre_norm, attn_post_norm, ffn_pre_norm, ffn_post_norm, b_gate, conv_b))
    for layer in range(DEPTH):
        a0, a1, a2, bq, cq = _in_proj(layer, x2, attn_pre, w_qkv)
        o_a = _attn_a(a0, a1, a2, bias_a)
        o_b = _attn_b(bq, bias_b, sinks[layer])
        o_c = _attn_c(cq)
        x2 = _merge(layer, x2, o_a, o_b, o_c, attn_pre, w_gate, gate_b, w_a, w_b, w_c, w_o, attn_post)
        x2 = _ffn(layer, x2, ffn_pre, w_up16, conv_w, conv_bias, w_down16, ffn_post)
    return x2.reshape(BATCH, SEQ, D_MODEL)
```

```python
import functools
import math

import numpy as np
import jax
import jax.numpy as jnp
from jax import lax
from jax.experimental import pallas as pl
from jax.experimental.pallas import tpu as pltpu

D_MODEL = 1024
BATCH = 8
SEQ = 2048
DEPTH = 2
HEAD_DIM = 64
BLK = 128
A_GROUPS = ((128, 1), (512, 4), (2048, 16))
A_HEADS = 4
B_Q_HEADS = 8
B_KV_HEADS = 2
B_WINDOW = 128
C_HEADS = 4
N_BRANCH = 3
NUM_BUCKETS = 32
MAX_DISTANCE = 2048
D_FF = 4 * D_MODEL
CONV_WIDTH = 3
EPS = 1e-6
SCALE = HEAD_DIM ** -0.5

A_WIDTH = A_HEADS * HEAD_DIM
B_WIDTH = B_Q_HEADS * HEAD_DIM
C_WIDTH = C_HEADS * HEAD_DIM
B_GROUP = B_Q_HEADS // B_KV_HEADS
N_A_GROUP_HEADS = len(A_GROUPS) * A_HEADS
N_BIAS_HEADS = N_A_GROUP_HEADS + B_Q_HEADS
A_QKV_COLS = 3 * N_A_GROUP_HEADS * HEAD_DIM
OFF_B_Q = A_QKV_COLS
OFF_B_KV = OFF_B_Q + B_WIDTH
OFF_C = OFF_B_KV + 2 * B_KV_HEADS * HEAD_DIM
OFF_GATE = OFF_C + 3 * C_WIDTH
GATE_COLS = N_BRANCH * D_MODEL

ROWS = BATCH * SEQ
N_BLK = SEQ // BLK
QKV_W = 3 * A_WIDTH
LANES = 128
B_COLS = B_WIDTH + 2 * B_KV_HEADS * LANES
NAT_COLS = QKV_W + B_COLS + QKV_W
NEG = -1e30
STICK_DEAD = -104.0
LOG2_E = 1.4426950408889634
LN_2 = 0.6931471805599453

V7X_VMEM_BYTES = 64 * 1024 * 1024
VMEM_LIMIT = V7X_VMEM_BYTES * 7 // 8

TM_PROJ = 512
TILES_PER_STEP = 2
TM_FFN = 512
TN_FFN = 512
FFN_AHEAD = 2
FFN_TILES = 1
SUBLANES = 8
BAND_UNROLL = 4
STICK_UNROLL = 4

BF16 = jnp.bfloat16
F32 = jnp.float32


def _params(*sem):
    return pltpu.CompilerParams(dimension_semantics=sem, vmem_limit_bytes=VMEM_LIMIT)


def _dot(a, b):
    return jnp.dot(a, b, preferred_element_type=F32)


def _dot_nt(a, b):
    return lax.dot_general(a, b, (((1,), (1,)), ((), ())), preferred_element_type=F32)


def _rms(xv, gain):
    y = xv * lax.rsqrt(jnp.mean(xv * xv, axis=-1, keepdims=True) + EPS)
    return y * gain


def _bucket_constants():
    a = np.arange(BLK)[:, None]
    b = np.arange(2 * BLK)[None, :]
    dist = np.maximum(a + BLK - b, 0)
    max_exact = NUM_BUCKETS // 2
    out = []
    for _, d in A_GROUPS:
        n = dist * d
        nf = np.maximum(n, 1).astype(np.float64)
        large = max_exact + (np.log(nf / max_exact) / math.log(MAX_DISTANCE / max_exact)
                             * (NUM_BUCKETS - max_exact)).astype(np.int64)
        large = np.minimum(large, NUM_BUCKETS - 1)
        out.append(np.where(n < max_exact, n, large))
    return np.stack(out).astype(np.int32)


def _bias_kernel(tab_ref, bucket_ref, out_ref, *, head0, max_dist):
    step = pl.program_id(0)
    a = lax.broadcasted_iota(jnp.int32, (BLK, 2 * BLK), 0)
    b = lax.broadcasted_iota(jnp.int32, (BLK, 2 * BLK), 1)
    dist = a + BLK - b
    valid = (dist >= 0) & (dist <= max_dist)
    bk = bucket_ref[...]
    for n in range(A_HEADS):
        acc = jnp.zeros((BLK, 2 * BLK), F32)
        for k in range(NUM_BUCKETS):
            acc = jnp.where(bk == k, tab_ref[k, head0 + step * A_HEADS + n], acc)
        out_ref[0, n] = jnp.where(valid, acc, NEG)
        out_ref[1, n] = jnp.where(valid & (b >= BLK), acc, NEG)


def _band_bias(rel_bias, head0, n_heads, max_dist, dilated):
    buckets = jnp.asarray(_bucket_constants())
    return pl.pallas_call(
        functools.partial(_bias_kernel, head0=head0, max_dist=max_dist),
        grid=(n_heads // A_HEADS,),
        in_specs=[
            pl.BlockSpec(memory_space=pltpu.SMEM),
            pl.BlockSpec((None, BLK, 2 * BLK), lambda s: (s if dilated else 0, 0, 0)),
        ],
        out_specs=pl.BlockSpec((2, A_HEADS, BLK, 2 * BLK), lambda s: (0, s, 0, 0)),
        out_shape=jax.ShapeDtypeStruct((2, n_heads, BLK, 2 * BLK), F32),
        compiler_params=_params("arbitrary"),
        name="band_bias",
    )(rel_bias, buckets)


def _in_proj_kernel(x_ref, g_ref, w_ref, a0_ref, a1_ref, a2_ref, b_ref, c_ref, ys_s):
    n_lane_tiles = D_MODEL // LANES

    def proj(h, col0, width):
        return _dot(h.astype(BF16), w_ref[:, col0:col0 + width]).astype(BF16)

    y = _rms(x_ref[...], g_ref[...])
    nat = proj(y, 0, NAT_COLS)
    a0_ref[...] = nat[:, :QKV_W]
    b_ref[...] = nat[:, QKV_W:QKV_W + B_COLS]
    c_ref[...] = nat[:, QKV_W + B_COLS:]
    for c in range(n_lane_tiles):
        ys_s[c] = y[:, c * LANES:(c + 1) * LANES]
    for ref, (_, d), col0 in ((a1_ref, A_GROUPS[1], NAT_COLS), (a2_ref, A_GROUPS[2], NAT_COLS + QKV_W)):
        per = TM_PROJ // d
        regrouped = jnp.concatenate(
            [jnp.concatenate([ys_s[c, pl.ds(r, per, stride=d), :] for c in range(n_lane_tiles)], axis=1)
             for r in range(d)], axis=0)
        ref[...] = proj(regrouped, col0, QKV_W).reshape(d, per, QKV_W)


def _layer_spec(layer, shape):
    return pl.BlockSpec((None,) + shape, lambda i: (layer, 0, 0))


def _in_proj(layer, x2, gain, w_qkv):
    tiles_per_seq = SEQ // TM_PROJ
    d1, d2 = A_GROUPS[1][1], A_GROUPS[2][1]

    def nat(width):
        return pl.BlockSpec((TM_PROJ, width), lambda i: (i, 0))

    return pl.pallas_call(
        _in_proj_kernel,
        grid=(ROWS // TM_PROJ,),
        in_specs=[
            pl.BlockSpec((TM_PROJ, D_MODEL), lambda i: (i, 0)),
            _layer_spec(layer, (1, D_MODEL)),
            _layer_spec(layer, (D_MODEL, NAT_COLS + 2 * QKV_W)),
        ],
        out_specs=[
            nat(QKV_W),
            pl.BlockSpec((None, d1, TM_PROJ // d1, QKV_W),
                         lambda i: (i // tiles_per_seq, 0, i % tiles_per_seq, 0)),
            pl.BlockSpec((None, d2, TM_PROJ // d2, QKV_W),
                         lambda i: (i // tiles_per_seq, 0, i % tiles_per_seq, 0)),
            nat(B_COLS),
            nat(QKV_W),
        ],
        out_shape=[
            jax.ShapeDtypeStruct((ROWS, QKV_W), BF16),
            jax.ShapeDtypeStruct((BATCH, d1, SEQ // d1, QKV_W), BF16),
            jax.ShapeDtypeStruct((BATCH, d2, SEQ // d2, QKV_W), BF16),
            jax.ShapeDtypeStruct((ROWS, B_COLS), BF16),
            jax.ShapeDtypeStruct((ROWS, QKV_W), BF16),
        ],
        scratch_shapes=[pltpu.VMEM((D_MODEL // LANES, TM_PROJ, LANES), F32)],
        compiler_params=_params("arbitrary"),
        name="in_proj",
    )(x2, gain, w_qkv)


def _low_lanes(rows):
    return lax.broadcasted_iota(jnp.int32, (rows, LANES), 1) < HEAD_DIM


def _stack_pair(q2):
    low = _low_lanes(BLK)
    zero = jnp.zeros_like(q2)
    return jnp.concatenate([jnp.where(low, q2, zero), jnp.where(low, zero, q2)], axis=0)


def _unstack_pair(o):
    return jnp.where(_low_lanes(BLK), o[:BLK], o[BLK:])


def _band_blocks(qs, kps, kcs, vps, vcs, biases):
    ss = [_dot_nt(q, jnp.concatenate([kp, kc], axis=0)) for q, kp, kc in zip(qs, kps, kcs)]
    ps, stats = [], []
    for s, bias in zip(ss, biases):
        s = s + bias
        m = jnp.max(s, axis=-1, keepdims=True)
        p = jnp.exp(s - m)
        stats.append((m, jnp.sum(p, axis=-1, keepdims=True)))
        ps.append(p.astype(BF16))
    nums = [_dot(p, jnp.concatenate([vp, vc], axis=0)) for p, vp, vc in zip(ps, vps, vcs)]
    return nums, stats


def _attn_a_kernel(a0_ref, a1_ref, a2_ref, bias_ref, o_ref, out_s, lse_s):
    n_pairs = A_HEADS // 2

    def cols(part, p):
        return slice(part * A_WIDTH + p * LANES, part * A_WIDTH + (p + 1) * LANES)

    for g, (ref, (_, d)) in enumerate(zip((a0_ref, a1_ref, a2_ref), A_GROUPS)):
        nb = N_BLK // d

        def step(n, carry, g=g, ref=ref, d=d, nb=nb):
            units, qs, kps, kcs, vps, vcs, biases = [], [], [], [], [], [], []
            for u in range(BAND_UNROLL):
                blk = n * BAND_UNROLL + u
                r = blk // nb
                i = blk % nb
                row_c = pl.multiple_of(i * BLK, BLK)
                row_p = pl.multiple_of(jnp.maximum(i - 1, 0) * BLK, BLK)
                first = jnp.where(i > 0, 0, 1)
                if d == 1:
                    rows = pl.ds(row_c, BLK)
                else:
                    rows = pl.ds(i * (BLK * d) + r, BLK, stride=d)
                for p in range(n_pairs):
                    units.append((p, rows))
                    heads = slice(g * A_HEADS + 2 * p, g * A_HEADS + 2 * p + 2)
                    qs.append(_stack_pair(ref[r, pl.ds(row_c, BLK), cols(0, p)]))
                    kps.append(ref[r, pl.ds(row_p, BLK), cols(1, p)])
                    kcs.append(ref[r, pl.ds(row_c, BLK), cols(1, p)])
                    vps.append(ref[r, pl.ds(row_p, BLK), cols(2, p)])
                    vcs.append(ref[r, pl.ds(row_c, BLK), cols(2, p)])
                    biases.append(bias_ref[first, heads].reshape(2 * BLK, 2 * BLK))
            nums, stats = _band_blocks(qs, kps, kcs, vps, vcs, biases)
            for (p, rows), num, (m, l) in zip(units, nums, stats):
                out_s[g, p, rows, :] = _unstack_pair(num / l)
                lse_s[g, p, rows, :] = _unstack_pair(jnp.broadcast_to(m + jnp.log(l), (2 * BLK, LANES)))
            return carry

        lax.fori_loop(0, N_BLK // BAND_UNROLL, step, 0)

    def combine(t, carry):
        rows = pl.ds(pl.multiple_of(t * BLK, BLK), BLK)
        for p in range(n_pairs):
            l0, l1, l2 = lse_s[0, p, rows, :], lse_s[1, p, rows, :], lse_s[2, p, rows, :]
            mx = jnp.maximum(jnp.maximum(l0, l1), l2)
            e0, e1, e2 = jnp.exp(l0 - mx), jnp.exp(l1 - mx), jnp.exp(l2 - mx)
            top = e0 * out_s[0, p, rows, :] + e1 * out_s[1, p, rows, :] + e2 * out_s[2, p, rows, :]
            o_ref[rows, p * LANES:(p + 1) * LANES] = (top / (e0 + e1 + e2)).astype(BF16)
        return carry

    lax.fori_loop(0, N_BLK, combine, 0)


def _attn_a(a0, a1, a2, bias):
    d1, d2 = A_GROUPS[1][1], A_GROUPS[2][1]
    return pl.pallas_call(
        _attn_a_kernel,
        grid=(BATCH,),
        in_specs=[
            pl.BlockSpec((None, 1, SEQ, QKV_W), lambda b: (b, 0, 0, 0)),
            pl.BlockSpec((None, d1, SEQ // d1, QKV_W), lambda b: (b, 0, 0, 0)),
            pl.BlockSpec((None, d2, SEQ // d2, QKV_W), lambda b: (b, 0, 0, 0)),
            pl.BlockSpec((2, N_A_GROUP_HEADS, BLK, 2 * BLK), lambda b: (0, 0, 0, 0)),
        ],
        out_specs=pl.BlockSpec((SEQ, A_WIDTH), lambda b: (b, 0)),
        out_shape=jax.ShapeDtypeStruct((ROWS, A_WIDTH), BF16),
        scratch_shapes=[
            pltpu.VMEM((len(A_GROUPS), A_HEADS // 2, SEQ, LANES), F32),
            pltpu.VMEM((len(A_GROUPS), A_HEADS // 2, SEQ, LANES), F32),
        ],
        compiler_params=_params("arbitrary"),
        name="attn_dilated",
    )(a0.reshape(BATCH, 1, SEQ, QKV_W), a1, a2, bias)


def _attn_b_kernel(sink_ref, qkv_ref, bias_ref, o_ref):
    def kv_cols(part, kvh):
        start = B_WIDTH + (part * B_KV_HEADS + kvh) * LANES
        return slice(start, start + LANES)

    def step(n, carry):
        blocks, qs, kps, kcs, vps, vcs, biases = [], [], [], [], [], [], []
        for u in range(BAND_UNROLL):
            i = n * BAND_UNROLL + u
            row_c = pl.multiple_of(i * BLK, BLK)
            row_p = pl.multiple_of(jnp.maximum(i - 1, 0) * BLK, BLK)
            first = jnp.where(i > 0, 0, 1)
            blocks.append(row_c)
            for kvh in range(B_KV_HEADS):
                heads = slice(kvh * B_GROUP, (kvh + 1) * B_GROUP)
                tiles = range(kvh * B_GROUP // 2, (kvh + 1) * B_GROUP // 2)
                qs.append(jnp.concatenate(
                    [_stack_pair(qkv_ref[pl.ds(row_c, BLK), t * LANES:(t + 1) * LANES]) for t in tiles],
                    axis=0))
                kps.append(qkv_ref[pl.ds(row_p, BLK), kv_cols(0, kvh)])
                kcs.append(qkv_ref[pl.ds(row_c, BLK), kv_cols(0, kvh)])
                vps.append(qkv_ref[pl.ds(row_p, BLK), kv_cols(1, kvh)])
                vcs.append(qkv_ref[pl.ds(row_c, BLK), kv_cols(1, kvh)])
                biases.append(bias_ref[first, heads].reshape(B_GROUP * BLK, 2 * BLK))
        nums, stats = _band_blocks(qs, kps, kcs, vps, vcs, biases)
        for u, row_c in enumerate(blocks):
            for kvh in range(B_KV_HEADS):
                num = nums[u * B_KV_HEADS + kvh]
                m, l = stats[u * B_KV_HEADS + kvh]
                scaled = []
                for gq in range(B_GROUP):
                    rows = slice(gq * BLK, (gq + 1) * BLK)
                    sink = sink_ref[0, kvh * B_GROUP + gq]
                    scaled.append(num[rows] / (l[rows] + jnp.exp(sink - m[rows])))
                for t in range(B_GROUP // 2):
                    tile = kvh * B_GROUP // 2 + t
                    o_ref[pl.ds(row_c, BLK), tile * LANES:(tile + 1) * LANES] = _unstack_pair(
                        jnp.concatenate(scaled[2 * t:2 * t + 2], axis=0)).astype(BF16)
        return carry

    lax.fori_loop(0, N_BLK // BAND_UNROLL, step, 0)


def _attn_b(qkv, bias, sinks):
    return pl.pallas_call(
        _attn_b_kernel,
        grid=(BATCH,),
        in_specs=[
            pl.BlockSpec(memory_space=pltpu.SMEM),
            pl.BlockSpec((SEQ, B_COLS), lambda b: (b, 0)),
            pl.BlockSpec((2, B_Q_HEADS, BLK, 2 * BLK), lambda b: (0, 0, 0, 0)),
        ],
        out_specs=pl.BlockSpec((SEQ, B_WIDTH), lambda b: (b, 0)),
        out_shape=jax.ShapeDtypeStruct((ROWS, B_WIDTH), BF16),
        compiler_params=_params("arbitrary"),
        name="attn_window",
    )(sinks.reshape(1, B_Q_HEADS), qkv, bias)


def _suffix_matrix():
    j = np.arange(2 * BLK)[:, None] % BLK
    s = np.arange(2 * BLK)[None, :]
    return ((s >= BLK) | (j > s)).astype(np.float32)


def _stick_blocks(qs, ks, vs, suffix, laters, strict_lower):
    zs = [_dot_nt(q, k) for q, k in zip(qs, ks)]
    log_betas, splits = [], []
    for z in zs:
        neg_soft = jnp.log2(1.0 + jnp.exp2(jnp.abs(z) * -LOG2_E)) * -LN_2
        log_keep = neg_soft - jnp.maximum(z, 0.0)
        log_betas.append(z + log_keep)
        if laters is None:
            log_keep = jnp.where(strict_lower, log_keep, 0.0)
        hi = log_keep.astype(BF16)
        lo = (log_keep - hi.astype(F32)).astype(BF16)
        splits.append(jnp.concatenate([hi, lo], axis=1))
    sums = [_dot(s, suffix) for s in splits]
    ws, new_laters = [], []
    for n, (log_beta, s) in enumerate(zip(log_betas, sums)):
        if laters is None:
            w = jnp.where(strict_lower, jnp.exp(log_beta + s[:, :BLK]), 0.0)
            new_laters.append(s[:, BLK:])
        else:
            w = jnp.exp(log_beta + s[:, :BLK] + laters[n])
            new_laters.append(laters[n] + s[:, BLK:])
        ws.append(w.astype(BF16))
    return [_unstack_pair(_dot(w, v)) for w, v in zip(ws, vs)], new_laters


def _stick_schedule():
    q_blk, slot, k_blk, first_step = [], [], [], [0, 0]
    for dist in range(1, N_BLK):
        group = [(i, i, i - dist) for i in range(dist, N_BLK)]
        group += [(0, N_BLK, 0)] * (-len(group) % STICK_UNROLL)
        for qi, sl, kj in group:
            q_blk.append(qi)
            slot.append(sl)
            k_blk.append(kj)
        first_step.append(len(q_blk) // STICK_UNROLL)
    return tuple(np.asarray(t, np.int32) for t in (q_blk, slot, k_blk, first_step))


def _attn_c_kernel(qblk_ref, slot_ref, kblk_ref, step_ref, qkv_ref, suffix_ref, o_ref, later_s, acc_s):
    n_pairs = C_HEADS // 2
    t_idx = lax.broadcasted_iota(jnp.int32, (2 * BLK, BLK), 0) % BLK
    s_idx = lax.broadcasted_iota(jnp.int32, (2 * BLK, BLK), 1)
    strict_lower = s_idx < t_idx

    def cols(part, p):
        return slice(part * C_WIDTH + p * LANES, part * C_WIDTH + (p + 1) * LANES)

    def rows_of(blk):
        return pl.ds(pl.multiple_of(blk * BLK, BLK), BLK)

    for p in range(n_pairs):
        later_s[p, N_BLK] = jnp.zeros((2 * BLK, BLK), F32)
        acc_s[p, N_BLK] = jnp.zeros((BLK, LANES), F32)

    def diag(n, carry):
        where = [(p, n * STICK_UNROLL + u) for u in range(STICK_UNROLL) for p in range(n_pairs)]
        parts, laters = _stick_blocks(
            [_stack_pair(qkv_ref[rows_of(i), cols(0, p)]) for p, i in where],
            [qkv_ref[rows_of(i), cols(1, p)] for p, i in where],
            [qkv_ref[rows_of(i), cols(2, p)] for p, i in where],
            suffix_ref[...], None, strict_lower)
        for (p, i), part, later in zip(where, parts, laters):
            acc_s[p, i] = part
            later_s[p, i] = later
        return carry

    lax.fori_loop(0, N_BLK // STICK_UNROLL, diag, 0)

    def off_diag(n, carry):
        where = []
        for u in range(STICK_UNROLL):
            e = n * STICK_UNROLL + u
            q_rows, k_rows, sl = rows_of(qblk_ref[e]), rows_of(kblk_ref[e]), slot_ref[e]
            where += [(p, sl, q_rows, k_rows) for p in range(n_pairs)]
        accs = [acc_s[p, sl] for p, sl, _, _ in where]
        parts, laters = _stick_blocks(
            [_stack_pair(qkv_ref[q_rows, cols(0, p)]) for p, _, q_rows, _ in where],
            [qkv_ref[k_rows, cols(1, p)] for p, _, _, k_rows in where],
            [qkv_ref[k_rows, cols(2, p)] for p, _, _, k_rows in where],
            suffix_ref[...], [later_s[p, sl] for p, sl, _, _ in where], None)
        for (p, sl, _, _), acc, part, later in zip(where, accs, parts, laters):
            acc_s[p, sl] = acc + part
            later_s[p, sl] = later
        return carry

    def group(state):
        dist, _ = state
        lax.fori_loop(step_ref[dist], step_ref[dist + 1], off_diag, 0)

        def slot_max(i, mx):
            for p in range(n_pairs):
                mx = jnp.maximum(mx, later_s[p, i])
            return mx

        mx = lax.fori_loop(dist + 1, N_BLK, slot_max, jnp.full((2 * BLK, BLK), -jnp.inf, F32))
        return dist + 1, jnp.logical_not(jnp.max(mx) < STICK_DEAD)

    lax.while_loop(lambda state: (state[0] < N_BLK) & state[1], group, (jnp.int32(1), jnp.bool_(True)))

    def emit(i, carry):
        for p in range(n_pairs):
            o_ref[rows_of(i), p * LANES:(p + 1) * LANES] = acc_s[p, i].astype(BF16)
        return carry

    lax.fori_loop(0, N_BLK, emit, 0)


def _attn_c(qkv):
    suffix = jnp.asarray(_suffix_matrix(), BF16)
    schedule = [jnp.asarray(t) for t in _stick_schedule()]
    smem = pl.BlockSpec(memory_space=pltpu.SMEM)
    return pl.pallas_call(
        _attn_c_kernel,
        grid=(BATCH,),
        in_specs=[
            smem, smem, smem, smem,
            pl.BlockSpec((SEQ, QKV_W), lambda b: (b, 0)),
            pl.BlockSpec((2 * BLK, 2 * BLK), lambda b: (0, 0)),
        ],
        out_specs=pl.BlockSpec((SEQ, C_WIDTH), lambda b: (b, 0)),
        out_shape=jax.ShapeDtypeStruct((ROWS, C_WIDTH), BF16),
        scratch_shapes=[
            pltpu.VMEM((C_HEADS // 2, N_BLK + 1, 2 * BLK, BLK), F32),
            pltpu.VMEM((C_HEADS // 2, N_BLK + 1, BLK, LANES), F32),
        ],
        compiler_params=_params("arbitrary"),
        name="attn_stick",
    )(*schedule, qkv, suffix)


def _merge_kernel(x_ref, oa_ref, ob_ref, oc_ref, gpre_ref, win_ref, bg_ref, wa_ref, wb_ref, wc_ref,
                  wo_ref, gpost_ref, y_ref):
    for t in range(TILES_PER_STEP):
        rows = slice(t * TM_PROJ, (t + 1) * TM_PROJ)
        xv = x_ref[rows, :]
        h = _rms(xv, gpre_ref[...]).astype(BF16)
        merged = None
        for n, (o_ref, w_ref) in enumerate(((oa_ref, wa_ref), (ob_ref, wb_ref), (oc_ref, wc_ref))):
            cols = slice(n * D_MODEL, (n + 1) * D_MODEL)
            gate_w = win_ref[:, OFF_GATE + n * D_MODEL:OFF_GATE + (n + 1) * D_MODEL]
            gate = jax.nn.sigmoid(_dot(h, gate_w) + bg_ref[:, cols])
            term = gate * _dot(o_ref[rows, :], w_ref[...])
            merged = term if merged is None else merged + term
        y = _dot(merged.astype(BF16), wo_ref[...])
        y_ref[rows, :] = xv + _rms(y, gpost_ref[...])


def _merge(layer, x2, o_a, o_b, o_c, gpre, w_in16, b_gate, w_a, w_b, w_c, w_o, gpost):
    def rows(width):
        return pl.BlockSpec((TILES_PER_STEP * TM_PROJ, width), lambda i: (i, 0))

    def whole(shape):
        return _layer_spec(layer, shape)

    return pl.pallas_call(
        _merge_kernel,
        grid=(ROWS // (TILES_PER_STEP * TM_PROJ),),
        in_specs=[
            rows(D_MODEL), rows(A_WIDTH), rows(B_WIDTH), rows(C_WIDTH),
            whole((1, D_MODEL)),
            pl.BlockSpec((None, D_MODEL, OFF_GATE + GATE_COLS), lambda i: (layer, 0, 0),
                         pipeline_mode=pl.Buffered(1)),
            whole((1, GATE_COLS)),
            whole((A_WIDTH, D_MODEL)), whole((B_WIDTH, D_MODEL)), whole((C_WIDTH, D_MODEL)),
            whole((D_MODEL, D_MODEL)), whole((1, D_MODEL)),
        ],
        out_specs=rows(D_MODEL),
        out_shape=jax.ShapeDtypeStruct((ROWS, D_MODEL), F32),
        compiler_params=_params("arbitrary"),
        name="merge_out_proj",
    )(x2, o_a, o_b, o_c, gpre, w_in16, b_gate, w_a, w_b, w_c, w_o, gpost)


def _gelu_tanh_doubled(x):
    k = math.sqrt(2.0 / math.pi)
    return x * (1.0 + jnp.tanh(x * (k + (k * 0.044715) * (x * x))))


def _ffn_kernel(x_ref, gpre_ref, wup_ref, cw_ref, cb_ref, wd_ref, gpost_ref, y_ref, xs_s, ys_s, tail_s):
    i = pl.program_id(0)
    n_chunks = D_FF // TN_FFN
    steps = TM_FFN // SUBLANES
    n_lane_tiles = D_MODEL // LANES

    def strided_rows(v):
        s, j0 = divmod(v * SUBLANES, steps)
        return pl.ds(j0 * SUBLANES + s, SUBLANES, stride=SUBLANES)

    for t in range(FFN_TILES):
        row0 = t * TM_FFN
        for lt in range(n_lane_tiles):
            for v in range(steps):
                xs_s[t, lt, strided_rows(v), :] = x_ref[row0 + v * SUBLANES:row0 + (v + 1) * SUBLANES,
                                                        lt * LANES:(lt + 1) * LANES]
        xv = jnp.concatenate([xs_s[t, lt] for lt in range(n_lane_tiles)], axis=1)
        h = _rms(xv, gpre_ref[...]).astype(BF16)
        seq_start = ((i * FFN_TILES + t) % (SEQ // TM_FFN)) == 0

        def up(c, h=h):
            return [_dot(h, wup_ref[:, half * D_FF + c * TN_FFN:half * D_FF + (c + 1) * TN_FFN])
                    for half in range(2)]

        def conv(u, c, half, seq_start=seq_start):
            cols = slice(half * D_FF + c * TN_FFN, half * D_FF + (c + 1) * TN_FFN)
            last2 = u[TM_FFN - 2 * SUBLANES:, :]
            prev2 = jnp.where(seq_start, 0.0, tail_s[2 * c + half])
            tail_s[2 * c + half] = last2
            wrapped = [jnp.concatenate([prev2[g * SUBLANES + SUBLANES - 1:(g + 1) * SUBLANES],
                                        last2[g * SUBLANES:(g + 1) * SUBLANES - 1]], axis=0)
                       for g in range(2)]
            back1 = jnp.concatenate([wrapped[1], u[:TM_FFN - SUBLANES]], axis=0)
            back2 = jnp.concatenate([wrapped[0], wrapped[1], u[:TM_FFN - 2 * SUBLANES]], axis=0)
            return (cw_ref[2:3, cols] * u + cw_ref[1:2, cols] * back1 + cw_ref[0:1, cols] * back2
                    + cb_ref[:, cols])

        acc = None
        ups = [up(c) for c in range(FFN_AHEAD)]
        for c in range(n_chunks):
            if c + FFN_AHEAD < n_chunks:
                ups.append(up(c + FFN_AHEAD))
            gate_u, val_u = ups.pop(0)
            act = (_gelu_tanh_doubled(conv(gate_u, c, 0)) * conv(val_u, c, 1)).astype(BF16)
            part = _dot(act, wd_ref[c * TN_FFN:(c + 1) * TN_FFN, :])
            acc = part if acc is None else acc + part
        y = xv + _rms(acc, gpost_ref[...])
        for lt in range(n_lane_tiles):
            ys_s[lt] = y[:, lt * LANES:(lt + 1) * LANES]
            for v in range(steps):
                y_ref[row0 + v * SUBLANES:row0 + (v + 1) * SUBLANES, lt * LANES:(lt + 1) * LANES] = (
                    ys_s[lt, strided_rows(v), :])


def _ffn(layer, x2, gpre, w_up, conv_w, conv_b, w_down, gpost):
    n_chunks = D_FF // TN_FFN

    def whole(shape):
        return pl.BlockSpec((None,) + shape, lambda i: (layer, 0, 0), pipeline_mode=pl.Buffered(1))

    return pl.pallas_call(
        _ffn_kernel,
        grid=(ROWS // (FFN_TILES * TM_FFN),),
        in_specs=[
            pl.BlockSpec((FFN_TILES * TM_FFN, D_MODEL), lambda i: (i, 0)),
            whole((1, D_MODEL)),
            whole((D_MODEL, 2 * D_FF)),
            whole((CONV_WIDTH, 2 * D_FF)),
            whole((1, 2 * D_FF)),
            whole((D_FF, D_MODEL)),
            whole((1, D_MODEL)),
        ],
        out_specs=pl.BlockSpec((FFN_TILES * TM_FFN, D_MODEL), lambda i: (i, 0)),
        out_shape=jax.ShapeDtypeStruct((ROWS, D_MODEL), F32),
        scratch_shapes=[
            pltpu.VMEM((FFN_TILES, D_MODEL // LANES, TM_FFN, LANES), F32),
            pltpu.VMEM((D_MODEL // LANES, TM_FFN, LANES), F32),
            pltpu.VMEM((2 * n_chunks, 2 * SUBLANES, TN_FFN), F32),
        ],
        compiler_params=_params("arbitrary"),
        name="conv_ffn",
    )(x2, gpre, w_up, conv_w, conv_b, w_down, gpost)


def _qkv_weight(w_in):
    rows = w_in.shape[:-1]
    a = w_in[..., :A_QKV_COLS].reshape(rows + (3, len(A_GROUPS), A_WIDTH))
    a = a * jnp.asarray([SCALE, 1.0, 1.0], w_in.dtype)[:, None, None]
    a = jnp.swapaxes(a, -3, -2).reshape(rows + (len(A_GROUPS), QKV_W))
    kv = w_in[..., OFF_B_KV:OFF_C].reshape(rows + (2 * B_KV_HEADS, 1, HEAD_DIM))
    kv = jnp.broadcast_to(kv, rows + (2 * B_KV_HEADS, 2, HEAD_DIM)).reshape(rows + (-1,))
    b = jnp.concatenate([w_in[..., OFF_B_Q:OFF_B_KV] * SCALE, kv], axis=-1)
    c = jnp.concatenate([w_in[..., OFF_C:OFF_C + C_WIDTH] * SCALE, w_in[..., OFF_C + C_WIDTH:OFF_GATE]],
                        axis=-1)
    return jnp.concatenate([a[..., 0, :], b, c, a[..., 1, :], a[..., 2, :]], axis=-1).astype(BF16)


def kernel(x, rel_bias, attn_pre_norm, w_in, b_gate, sinks, w_br_a, w_br_b, w_br_c, w_out,
           attn_post_norm, ffn_pre_norm, w_up, conv_w, conv_b, w_down, ffn_post_norm):
    assert x.shape == (BATCH, SEQ, D_MODEL) and x.dtype == F32
    bias_a = _band_bias(rel_bias, 0, N_A_GROUP_HEADS, BLK, True)
    bias_b = _band_bias(rel_bias, N_A_GROUP_HEADS, B_Q_HEADS, B_WINDOW - 1, False)
    x2 = x.reshape(ROWS, D_MODEL)

    def row_param(p):
        return p.reshape(DEPTH, 1, -1)

    w_in16 = w_in.astype(BF16)
    w_qkv = _qkv_weight(w_in16)
    w_a, w_b, w_c, w_o = (w.astype(BF16) for w in (w_br_a, w_br_b, w_br_c, w_out))
    w_up16, w_down16 = w_up.astype(BF16), (w_down * 0.5).astype(BF16)
    attn_pre, attn_post, ffn_pre, ffn_post, gate_b, conv_bias = (
        row_param(p) for p in (attn_pre_norm, attn_post_norm, ffn_pre_norm, ffn_post_norm, b_gate, conv_b))
    for layer in range(DEPTH):
        a0, a1, a2, bq, cq = _in_proj(layer, x2, attn_pre, w_qkv)
        o_a = _attn_a(a0, a1, a2, bias_a)
        o_b = _attn_b(bq, bias_b, sinks[layer])
        o_c = _attn_c(cq)
        x2 = _merge(layer, x2, o_a, o_b, o_c, attn_pre, w_in16, gate_b, w_a, w_b, w_c, w_o, attn_post)
        x2 = _ffn(layer, x2, ffn_pre, w_up16, conv_w, conv_bias, w_down16, ffn_post)
    return x2.reshape(BATCH, SEQ, D_MODEL)
```

```python
import functools
import math

import numpy as np
import jax
import jax.numpy as jnp
from jax import lax
from jax.experimental import pallas as pl
from jax.experimental.pallas import tpu as pltpu

D_MODEL = 1024
BATCH = 8
SEQ = 2048
DEPTH = 2
HEAD_DIM = 64
BLK = 128
A_GROUPS = ((128, 1), (512, 4), (2048, 16))
A_HEADS = 4
B_Q_HEADS = 8
B_KV_HEADS = 2
B_WINDOW = 128
C_HEADS = 4
N_BRANCH = 3
NUM_BUCKETS = 32
MAX_DISTANCE = 2048
D_FF = 4 * D_MODEL
CONV_WIDTH = 3
EPS = 1e-6
SCALE = HEAD_DIM ** -0.5

A_WIDTH = A_HEADS * HEAD_DIM
B_WIDTH = B_Q_HEADS * HEAD_DIM
C_WIDTH = C_HEADS * HEAD_DIM
B_GROUP = B_Q_HEADS // B_KV_HEADS
N_A_GROUP_HEADS = len(A_GROUPS) * A_HEADS
N_BIAS_HEADS = N_A_GROUP_HEADS + B_Q_HEADS
A_QKV_COLS = 3 * N_A_GROUP_HEADS * HEAD_DIM
OFF_B_Q = A_QKV_COLS
OFF_B_KV = OFF_B_Q + B_WIDTH
OFF_C = OFF_B_KV + 2 * B_KV_HEADS * HEAD_DIM
OFF_GATE = OFF_C + 3 * C_WIDTH
GATE_COLS = N_BRANCH * D_MODEL

ROWS = BATCH * SEQ
N_BLK = SEQ // BLK
QKV_W = 3 * A_WIDTH
LANES = 128
B_COLS = B_WIDTH + 2 * B_KV_HEADS * LANES
NAT_COLS = QKV_W + B_COLS + QKV_W
NEG = -1e30
STICK_DEAD = -104.0

V7X_VMEM_BYTES = 64 * 1024 * 1024
VMEM_LIMIT = V7X_VMEM_BYTES * 7 // 8

TM_PROJ = 512
TILES_PER_STEP = 2
MERGE_COLS = 256
TM_FFN = 512
TN_FFN = 512
FFN_AHEAD = 2
FFN_TILES = 1
SUBLANES = 8
BAND_UNROLL = 4
STICK_UNROLL = 4

BF16 = jnp.bfloat16
F32 = jnp.float32


def _params(*sem):
    return pltpu.CompilerParams(dimension_semantics=sem, vmem_limit_bytes=VMEM_LIMIT)


def _dot(a, b):
    return jnp.dot(a, b, preferred_element_type=F32)


def _dot_nt(a, b):
    return lax.dot_general(a, b, (((1,), (1,)), ((), ())), preferred_element_type=F32)


def _rms(xv, gain):
    y = xv * lax.rsqrt(jnp.mean(xv * xv, axis=-1, keepdims=True) + EPS)
    return y * gain


def _bucket_constants():
    a = np.arange(BLK)[:, None]
    b = np.arange(2 * BLK)[None, :]
    dist = np.maximum(a + BLK - b, 0)
    max_exact = NUM_BUCKETS // 2
    out = []
    for _, d in A_GROUPS:
        n = dist * d
        nf = np.maximum(n, 1).astype(np.float64)
        large = max_exact + (np.log(nf / max_exact) / math.log(MAX_DISTANCE / max_exact)
                             * (NUM_BUCKETS - max_exact)).astype(np.int64)
        large = np.minimum(large, NUM_BUCKETS - 1)
        out.append(np.where(n < max_exact, n, large))
    return np.stack(out).astype(np.int32)


def _bias_kernel(tab_ref, bucket_ref, out_ref, *, head0, max_dist):
    step = pl.program_id(0)
    a = lax.broadcasted_iota(jnp.int32, (BLK, 2 * BLK), 0)
    b = lax.broadcasted_iota(jnp.int32, (BLK, 2 * BLK), 1)
    dist = a + BLK - b
    valid = (dist >= 0) & (dist <= max_dist)
    bk = bucket_ref[...]
    for n in range(A_HEADS):
        acc = jnp.zeros((BLK, 2 * BLK), F32)
        for k in range(NUM_BUCKETS):
            acc = jnp.where(bk == k, tab_ref[k, head0 + step * A_HEADS + n], acc)
        out_ref[0, n] = jnp.where(valid, acc, NEG)
        out_ref[1, n] = jnp.where(valid & (b >= BLK), acc, NEG)


def _band_bias(rel_bias, head0, n_heads, max_dist, dilated):
    buckets = jnp.asarray(_bucket_constants())
    return pl.pallas_call(
        functools.partial(_bias_kernel, head0=head0, max_dist=max_dist),
        grid=(n_heads // A_HEADS,),
        in_specs=[
            pl.BlockSpec(memory_space=pltpu.SMEM),
            pl.BlockSpec((None, BLK, 2 * BLK), lambda s: (s if dilated else 0, 0, 0)),
        ],
        out_specs=pl.BlockSpec((2, A_HEADS, BLK, 2 * BLK), lambda s: (0, s, 0, 0)),
        out_shape=jax.ShapeDtypeStruct((2, n_heads, BLK, 2 * BLK), F32),
        compiler_params=_params("arbitrary"),
        name="band_bias",
    )(rel_bias, buckets)


def _in_proj_kernel(x_ref, g_ref, w_ref, a0_ref, a1_ref, a2_ref, b_ref, c_ref, ys_s):
    n_lane_tiles = D_MODEL // LANES

    def proj(h, col0, width):
        return _dot(h.astype(BF16), w_ref[:, col0:col0 + width]).astype(BF16)

    y = _rms(x_ref[...], g_ref[...])
    nat = proj(y, 0, NAT_COLS)
    a0_ref[...] = nat[:, :QKV_W]
    b_ref[...] = nat[:, QKV_W:QKV_W + B_COLS]
    c_ref[...] = nat[:, QKV_W + B_COLS:]
    for c in range(n_lane_tiles):
        ys_s[c] = y[:, c * LANES:(c + 1) * LANES]
    for ref, (_, d), col0 in ((a1_ref, A_GROUPS[1], NAT_COLS), (a2_ref, A_GROUPS[2], NAT_COLS + QKV_W)):
        per = TM_PROJ // d
        regrouped = jnp.concatenate(
            [jnp.concatenate([ys_s[c, pl.ds(r, per, stride=d), :] for c in range(n_lane_tiles)], axis=1)
             for r in range(d)], axis=0)
        ref[...] = proj(regrouped, col0, QKV_W).reshape(d, per, QKV_W)


def _layer_spec(layer, shape):
    return pl.BlockSpec((None,) + shape, lambda i: (layer, 0, 0))


def _in_proj(layer, x2, gain, w_qkv):
    tiles_per_seq = SEQ // TM_PROJ
    d1, d2 = A_GROUPS[1][1], A_GROUPS[2][1]

    def nat(width):
        return pl.BlockSpec((TM_PROJ, width), lambda i: (i, 0))

    return pl.pallas_call(
        _in_proj_kernel,
        grid=(ROWS // TM_PROJ,),
        in_specs=[
            pl.BlockSpec((TM_PROJ, D_MODEL), lambda i: (i, 0)),
            _layer_spec(layer, (1, D_MODEL)),
            _layer_spec(layer, (D_MODEL, NAT_COLS + 2 * QKV_W)),
        ],
        out_specs=[
            nat(QKV_W),
            pl.BlockSpec((None, d1, TM_PROJ // d1, QKV_W),
                         lambda i: (i // tiles_per_seq, 0, i % tiles_per_seq, 0)),
            pl.BlockSpec((None, d2, TM_PROJ // d2, QKV_W),
                         lambda i: (i // tiles_per_seq, 0, i % tiles_per_seq, 0)),
            nat(B_COLS),
            nat(QKV_W),
        ],
        out_shape=[
            jax.ShapeDtypeStruct((ROWS, QKV_W), BF16),
            jax.ShapeDtypeStruct((BATCH, d1, SEQ // d1, QKV_W), BF16),
            jax.ShapeDtypeStruct((BATCH, d2, SEQ // d2, QKV_W), BF16),
            jax.ShapeDtypeStruct((ROWS, B_COLS), BF16),
            jax.ShapeDtypeStruct((ROWS, QKV_W), BF16),
        ],
        scratch_shapes=[pltpu.VMEM((D_MODEL // LANES, TM_PROJ, LANES), F32)],
        compiler_params=_params("arbitrary"),
        name="in_proj",
    )(x2, gain, w_qkv)


def _low_lanes(rows):
    return lax.broadcasted_iota(jnp.int32, (rows, LANES), 1) < HEAD_DIM


def _stack_pair(q2):
    low = _low_lanes(BLK)
    zero = jnp.zeros_like(q2)
    return jnp.concatenate([jnp.where(low, q2, zero), jnp.where(low, zero, q2)], axis=0)


def _unstack_pair(o):
    return jnp.where(_low_lanes(BLK), o[:BLK], o[BLK:])


def _band_blocks(qs, kps, kcs, vps, vcs, biases):
    ss = [_dot_nt(q, jnp.concatenate([kp, kc], axis=0)) for q, kp, kc in zip(qs, kps, kcs)]
    ps, stats = [], []
    for s, bias in zip(ss, biases):
        s = s + bias
        m = jnp.max(s, axis=-1, keepdims=True)
        p = jnp.exp(s - m)
        stats.append((m, jnp.sum(p, axis=-1, keepdims=True)))
        ps.append(p.astype(BF16))
    nums = [_dot(p, jnp.concatenate([vp, vc], axis=0)) for p, vp, vc in zip(ps, vps, vcs)]
    return nums, stats


def _attn_a_kernel(a0_ref, a1_ref, a2_ref, bias_ref, o_ref, out_s, lse_s):
    n_pairs = A_HEADS // 2

    def cols(part, p):
        return slice(part * A_WIDTH + p * LANES, part * A_WIDTH + (p + 1) * LANES)

    for g, (ref, (_, d)) in enumerate(zip((a0_ref, a1_ref, a2_ref), A_GROUPS)):
        nb = N_BLK // d

        def step(n, carry, g=g, ref=ref, d=d, nb=nb):
            units, qs, kps, kcs, vps, vcs, biases = [], [], [], [], [], [], []
            for u in range(BAND_UNROLL):
                blk = n * BAND_UNROLL + u
                r = blk // nb
                i = blk % nb
                row_c = pl.multiple_of(i * BLK, BLK)
                row_p = pl.multiple_of(jnp.maximum(i - 1, 0) * BLK, BLK)
                first = jnp.where(i > 0, 0, 1)
                if d == 1:
                    rows = pl.ds(row_c, BLK)
                else:
                    rows = pl.ds(i * (BLK * d) + r, BLK, stride=d)
                for p in range(n_pairs):
                    units.append((p, rows))
                    heads = slice(g * A_HEADS + 2 * p, g * A_HEADS + 2 * p + 2)
                    qs.append(_stack_pair(ref[r, pl.ds(row_c, BLK), cols(0, p)]))
                    kps.append(ref[r, pl.ds(row_p, BLK), cols(1, p)])
                    kcs.append(ref[r, pl.ds(row_c, BLK), cols(1, p)])
                    vps.append(ref[r, pl.ds(row_p, BLK), cols(2, p)])
                    vcs.append(ref[r, pl.ds(row_c, BLK), cols(2, p)])
                    biases.append(bias_ref[first, heads].reshape(2 * BLK, 2 * BLK))
            nums, stats = _band_blocks(qs, kps, kcs, vps, vcs, biases)
            for (p, rows), num, (m, l) in zip(units, nums, stats):
                out_s[g, p, rows, :] = _unstack_pair(num / l)
                lse_s[g, p, rows, :] = _unstack_pair(jnp.broadcast_to(m + jnp.log(l), (2 * BLK, LANES)))
            return carry

        lax.fori_loop(0, N_BLK // BAND_UNROLL, step, 0)

    def combine(t, carry):
        rows = pl.ds(pl.multiple_of(t * BLK, BLK), BLK)
        for p in range(n_pairs):
            l0, l1, l2 = lse_s[0, p, rows, :], lse_s[1, p, rows, :], lse_s[2, p, rows, :]
            mx = jnp.maximum(jnp.maximum(l0, l1), l2)
            e0, e1, e2 = jnp.exp(l0 - mx), jnp.exp(l1 - mx), jnp.exp(l2 - mx)
            top = e0 * out_s[0, p, rows, :] + e1 * out_s[1, p, rows, :] + e2 * out_s[2, p, rows, :]
            o_ref[rows, p * LANES:(p + 1) * LANES] = (top / (e0 + e1 + e2)).astype(BF16)
        return carry

    lax.fori_loop(0, N_BLK, combine, 0)


def _attn_a(a0, a1, a2, bias):
    d1, d2 = A_GROUPS[1][1], A_GROUPS[2][1]
    return pl.pallas_call(
        _attn_a_kernel,
        grid=(BATCH,),
        in_specs=[
            pl.BlockSpec((None, 1, SEQ, QKV_W), lambda b: (b, 0, 0, 0)),
            pl.BlockSpec((None, d1, SEQ // d1, QKV_W), lambda b: (b, 0, 0, 0)),
            pl.BlockSpec((None, d2, SEQ // d2, QKV_W), lambda b: (b, 0, 0, 0)),
            pl.BlockSpec((2, N_A_GROUP_HEADS, BLK, 2 * BLK), lambda b: (0, 0, 0, 0)),
        ],
        out_specs=pl.BlockSpec((SEQ, A_WIDTH), lambda b: (b, 0)),
        out_shape=jax.ShapeDtypeStruct((ROWS, A_WIDTH), BF16),
        scratch_shapes=[
            pltpu.VMEM((len(A_GROUPS), A_HEADS // 2, SEQ, LANES), F32),
            pltpu.VMEM((len(A_GROUPS), A_HEADS // 2, SEQ, LANES), F32),
        ],
        compiler_params=_params("arbitrary"),
        name="attn_dilated",
    )(a0.reshape(BATCH, 1, SEQ, QKV_W), a1, a2, bias)


def _attn_b_kernel(sink_ref, qkv_ref, bias_ref, o_ref):
    def kv_cols(part, kvh):
        start = B_WIDTH + (part * B_KV_HEADS + kvh) * LANES
        return slice(start, start + LANES)

    def step(n, carry):
        blocks, qs, kps, kcs, vps, vcs, biases = [], [], [], [], [], [], []
        for u in range(BAND_UNROLL):
            i = n * BAND_UNROLL + u
            row_c = pl.multiple_of(i * BLK, BLK)
            row_p = pl.multiple_of(jnp.maximum(i - 1, 0) * BLK, BLK)
            first = jnp.where(i > 0, 0, 1)
            blocks.append(row_c)
            for kvh in range(B_KV_HEADS):
                heads = slice(kvh * B_GROUP, (kvh + 1) * B_GROUP)
                tiles = range(kvh * B_GROUP // 2, (kvh + 1) * B_GROUP // 2)
                qs.append(jnp.concatenate(
                    [_stack_pair(qkv_ref[pl.ds(row_c, BLK), t * LANES:(t + 1) * LANES]) for t in tiles],
                    axis=0))
                kps.append(qkv_ref[pl.ds(row_p, BLK), kv_cols(0, kvh)])
                kcs.append(qkv_ref[pl.ds(row_c, BLK), kv_cols(0, kvh)])
                vps.append(qkv_ref[pl.ds(row_p, BLK), kv_cols(1, kvh)])
                vcs.append(qkv_ref[pl.ds(row_c, BLK), kv_cols(1, kvh)])
                biases.append(bias_ref[first, heads].reshape(B_GROUP * BLK, 2 * BLK))
        nums, stats = _band_blocks(qs, kps, kcs, vps, vcs, biases)
        for u, row_c in enumerate(blocks):
            for kvh in range(B_KV_HEADS):
                num = nums[u * B_KV_HEADS + kvh]
                m, l = stats[u * B_KV_HEADS + kvh]
                scaled = []
                for gq in range(B_GROUP):
                    rows = slice(gq * BLK, (gq + 1) * BLK)
                    sink = sink_ref[0, kvh * B_GROUP + gq]
                    scaled.append(num[rows] / (l[rows] + jnp.exp(sink - m[rows])))
                for t in range(B_GROUP // 2):
                    tile = kvh * B_GROUP // 2 + t
                    o_ref[pl.ds(row_c, BLK), tile * LANES:(tile + 1) * LANES] = _unstack_pair(
                        jnp.concatenate(scaled[2 * t:2 * t + 2], axis=0)).astype(BF16)
        return carry

    lax.fori_loop(0, N_BLK // BAND_UNROLL, step, 0)


def _attn_b(qkv, bias, sinks):
    return pl.pallas_call(
        _attn_b_kernel,
        grid=(BATCH,),
        in_specs=[
            pl.BlockSpec(memory_space=pltpu.SMEM),
            pl.BlockSpec((SEQ, B_COLS), lambda b: (b, 0)),
            pl.BlockSpec((2, B_Q_HEADS, BLK, 2 * BLK), lambda b: (0, 0, 0, 0)),
        ],
        out_specs=pl.BlockSpec((SEQ, B_WIDTH), lambda b: (b, 0)),
        out_shape=jax.ShapeDtypeStruct((ROWS, B_WIDTH), BF16),
        compiler_params=_params("arbitrary"),
        name="attn_window",
    )(sinks.reshape(1, B_Q_HEADS), qkv, bias)


def _suffix_matrix():
    j = np.arange(2 * BLK)[:, None] % BLK
    s = np.arange(2 * BLK)[None, :]
    return ((s >= BLK) | (j > s)).astype(np.float32)


def _stick_blocks(qs, ks, vs, suffix, laters, strict_lower):
    zs = [_dot_nt(q, k) for q, k in zip(qs, ks)]
    log_betas, splits = [], []
    for z in zs:
        soft = jnp.log(1.0 + jnp.exp(-jnp.abs(z)))
        log_keep = -(jnp.maximum(z, 0.0) + soft)
        log_betas.append(z + log_keep)
        if laters is None:
            log_keep = jnp.where(strict_lower, log_keep, 0.0)
        hi = log_keep.astype(BF16)
        lo = (log_keep - hi.astype(F32)).astype(BF16)
        splits.append(jnp.concatenate([hi, lo], axis=1))
    sums = [_dot(s, suffix) for s in splits]
    ws, new_laters = [], []
    for n, (log_beta, s) in enumerate(zip(log_betas, sums)):
        if laters is None:
            w = jnp.where(strict_lower, jnp.exp(log_beta + s[:, :BLK]), 0.0)
            new_laters.append(s[:, BLK:])
        else:
            w = jnp.exp(log_beta + s[:, :BLK] + laters[n])
            new_laters.append(laters[n] + s[:, BLK:])
        ws.append(w.astype(BF16))
    return [_unstack_pair(_dot(w, v)) for w, v in zip(ws, vs)], new_laters


def _stick_schedule():
    q_blk, slot, k_blk, first_step = [], [], [], [0, 0]
    for dist in range(1, N_BLK):
        group = [(i, i, i - dist) for i in range(dist, N_BLK)]
        group += [(0, N_BLK, 0)] * (-len(group) % STICK_UNROLL)
        for qi, sl, kj in group:
            q_blk.append(qi)
            slot.append(sl)
            k_blk.append(kj)
        first_step.append(len(q_blk) // STICK_UNROLL)
    return tuple(np.asarray(t, np.int32) for t in (q_blk, slot, k_blk, first_step))


def _attn_c_kernel(qblk_ref, slot_ref, kblk_ref, step_ref, qkv_ref, suffix_ref, o_ref, later_s, acc_s):
    n_pairs = C_HEADS // 2
    t_idx = lax.broadcasted_iota(jnp.int32, (2 * BLK, BLK), 0) % BLK
    s_idx = lax.broadcasted_iota(jnp.int32, (2 * BLK, BLK), 1)
    strict_lower = s_idx < t_idx

    def cols(part, p):
        return slice(part * C_WIDTH + p * LANES, part * C_WIDTH + (p + 1) * LANES)

    def rows_of(blk):
        return pl.ds(pl.multiple_of(blk * BLK, BLK), BLK)

    for p in range(n_pairs):
        later_s[p, N_BLK] = jnp.zeros((2 * BLK, BLK), F32)
        acc_s[p, N_BLK] = jnp.zeros((BLK, LANES), F32)

    def diag(n, carry):
        where = [(p, n * STICK_UNROLL + u) for u in range(STICK_UNROLL) for p in range(n_pairs)]
        parts, laters = _stick_blocks(
            [_stack_pair(qkv_ref[rows_of(i), cols(0, p)]) for p, i in where],
            [qkv_ref[rows_of(i), cols(1, p)] for p, i in where],
            [qkv_ref[rows_of(i), cols(2, p)] for p, i in where],
            suffix_ref[...], None, strict_lower)
        for (p, i), part, later in zip(where, parts, laters):
            acc_s[p, i] = part
            later_s[p, i] = later
        return carry

    lax.fori_loop(0, N_BLK // STICK_UNROLL, diag, 0)

    def off_diag(n, carry):
        where = []
        for u in range(STICK_UNROLL):
            e = n * STICK_UNROLL + u
            q_rows, k_rows, sl = rows_of(qblk_ref[e]), rows_of(kblk_ref[e]), slot_ref[e]
            where += [(p, sl, q_rows, k_rows) for p in range(n_pairs)]
        accs = [acc_s[p, sl] for p, sl, _, _ in where]
        parts, laters = _stick_blocks(
            [_stack_pair(qkv_ref[q_rows, cols(0, p)]) for p, _, q_rows, _ in where],
            [qkv_ref[k_rows, cols(1, p)] for p, _, _, k_rows in where],
            [qkv_ref[k_rows, cols(2, p)] for p, _, _, k_rows in where],
            suffix_ref[...], [later_s[p, sl] for p, sl, _, _ in where], None)
        for (p, sl, _, _), acc, part, later in zip(where, accs, parts, laters):
            acc_s[p, sl] = acc + part
            later_s[p, sl] = later
        return carry

    def group(state):
        dist, _ = state
        lax.fori_loop(step_ref[dist], step_ref[dist + 1], off_diag, 0)

        def slot_max(i, mx):
            for p in range(n_pairs):
                mx = jnp.maximum(mx, later_s[p, i])
            return mx

        mx = lax.fori_loop(dist + 1, N_BLK, slot_max, jnp.full((2 * BLK, BLK), -jnp.inf, F32))
        return dist + 1, jnp.logical_not(jnp.max(mx) < STICK_DEAD)

    lax.while_loop(lambda state: (state[0] < N_BLK) & state[1], group, (jnp.int32(1), jnp.bool_(True)))

    def emit(i, carry):
        for p in range(n_pairs):
            o_ref[rows_of(i), p * LANES:(p + 1) * LANES] = acc_s[p, i].astype(BF16)
        return carry

    lax.fori_loop(0, N_BLK, emit, 0)


def _attn_c(qkv):
    suffix = jnp.asarray(_suffix_matrix(), BF16)
    schedule = [jnp.asarray(t) for t in _stick_schedule()]
    smem = pl.BlockSpec(memory_space=pltpu.SMEM)
    return pl.pallas_call(
        _attn_c_kernel,
        grid=(BATCH,),
        in_specs=[
            smem, smem, smem, smem,
            pl.BlockSpec((SEQ, QKV_W), lambda b: (b, 0)),
            pl.BlockSpec((2 * BLK, 2 * BLK), lambda b: (0, 0)),
        ],
        out_specs=pl.BlockSpec((SEQ, C_WIDTH), lambda b: (b, 0)),
        out_shape=jax.ShapeDtypeStruct((ROWS, C_WIDTH), BF16),
        scratch_shapes=[
            pltpu.VMEM((C_HEADS // 2, N_BLK + 1, 2 * BLK, BLK), F32),
            pltpu.VMEM((C_HEADS // 2, N_BLK + 1, BLK, LANES), F32),
        ],
        compiler_params=_params("arbitrary"),
        name="attn_stick",
    )(*schedule, qkv, suffix)


def _merge_kernel(x_ref, oa_ref, ob_ref, oc_ref, gpre_ref, win_ref, bg_ref, wa_ref, wb_ref, wc_ref,
                  wo_ref, gpost_ref, y_ref):
    for t in range(TILES_PER_STEP):
        rows = slice(t * TM_PROJ, (t + 1) * TM_PROJ)
        xv = x_ref[rows, :]
        h = _rms(xv, gpre_ref[...]).astype(BF16)
        slabs = []
        for c in range(D_MODEL // MERGE_COLS):
            merged = None
            for n, (o_ref, w_ref) in enumerate(((oa_ref, wa_ref), (ob_ref, wb_ref), (oc_ref, wc_ref))):
                cols = slice(n * D_MODEL + c * MERGE_COLS, n * D_MODEL + (c + 1) * MERGE_COLS)
                gate_w = win_ref[:, OFF_GATE + cols.start:OFF_GATE + cols.stop]
                gate = jax.nn.sigmoid(_dot(h, gate_w) + bg_ref[:, cols])
                term = gate * _dot(o_ref[rows, :], w_ref[:, c * MERGE_COLS:(c + 1) * MERGE_COLS])
                merged = term if merged is None else merged + term
            slabs.append(merged.astype(BF16))
        y = _dot(jnp.concatenate(slabs, axis=1), wo_ref[...])
        y_ref[rows, :] = xv + _rms(y, gpost_ref[...])


def _merge(layer, x2, o_a, o_b, o_c, gpre, w_in16, b_gate, w_a, w_b, w_c, w_o, gpost):
    def rows(width):
        return pl.BlockSpec((TILES_PER_STEP * TM_PROJ, width), lambda i: (i, 0))

    def whole(shape):
        return _layer_spec(layer, shape)

    return pl.pallas_call(
        _merge_kernel,
        grid=(ROWS // (TILES_PER_STEP * TM_PROJ),),
        in_specs=[
            rows(D_MODEL), rows(A_WIDTH), rows(B_WIDTH), rows(C_WIDTH),
            whole((1, D_MODEL)),
            pl.BlockSpec((None, D_MODEL, OFF_GATE + GATE_COLS), lambda i: (layer, 0, 0),
                         pipeline_mode=pl.Buffered(1)),
            whole((1, GATE_COLS)),
            whole((A_WIDTH, D_MODEL)), whole((B_WIDTH, D_MODEL)), whole((C_WIDTH, D_MODEL)),
            whole((D_MODEL, D_MODEL)), whole((1, D_MODEL)),
        ],
        out_specs=rows(D_MODEL),
        out_shape=jax.ShapeDtypeStruct((ROWS, D_MODEL), F32),
        compiler_params=_params("arbitrary"),
        name="merge_out_proj",
    )(x2, o_a, o_b, o_c, gpre, w_in16, b_gate, w_a, w_b, w_c, w_o, gpost)


def _gelu_tanh_doubled(x):
    k = math.sqrt(2.0 / math.pi)
    return x * (1.0 + jnp.tanh(x * (k + (k * 0.044715) * (x * x))))


def _ffn_kernel(x_ref, gpre_ref, wup_ref, cw_ref, cb_ref, wd_ref, gpost_ref, y_ref, xs_s, ys_s, tail_s):
    i = pl.program_id(0)
    n_chunks = D_FF // TN_FFN
    steps = TM_FFN // SUBLANES
    n_lane_tiles = D_MODEL // LANES

    def strided_rows(v):
        s, j0 = divmod(v * SUBLANES, steps)
        return pl.ds(j0 * SUBLANES + s, SUBLANES, stride=SUBLANES)

    for t in range(FFN_TILES):
        row0 = t * TM_FFN
        for lt in range(n_lane_tiles):
            for v in range(steps):
                xs_s[t, lt, strided_rows(v), :] = x_ref[row0 + v * SUBLANES:row0 + (v + 1) * SUBLANES,
                                                        lt * LANES:(lt + 1) * LANES]
        xv = jnp.concatenate([xs_s[t, lt] for lt in range(n_lane_tiles)], axis=1)
        h = _rms(xv, gpre_ref[...]).astype(BF16)
        seq_start = ((i * FFN_TILES + t) % (SEQ // TM_FFN)) == 0

        def up(c, h=h):
            return [_dot(h, wup_ref[:, half * D_FF + c * TN_FFN:half * D_FF + (c + 1) * TN_FFN])
                    for half in range(2)]

        def conv(u, c, half, seq_start=seq_start):
            cols = slice(half * D_FF + c * TN_FFN, half * D_FF + (c + 1) * TN_FFN)
            last2 = u[TM_FFN - 2 * SUBLANES:, :]
            prev2 = jnp.where(seq_start, 0.0, tail_s[2 * c + half])
            tail_s[2 * c + half] = last2
            wrapped = [jnp.concatenate([prev2[g * SUBLANES + SUBLANES - 1:(g + 1) * SUBLANES],
                                        last2[g * SUBLANES:(g + 1) * SUBLANES - 1]], axis=0)
                       for g in range(2)]
            back1 = jnp.concatenate([wrapped[1], u[:TM_FFN - SUBLANES]], axis=0)
            back2 = jnp.concatenate([wrapped[0], wrapped[1], u[:TM_FFN - 2 * SUBLANES]], axis=0)
            return (cw_ref[2:3, cols] * u + cw_ref[1:2, cols] * back1 + cw_ref[0:1, cols] * back2
                    + cb_ref[:, cols])

        acc = None
        ups = [up(c) for c in range(FFN_AHEAD)]
        for c in range(n_chunks):
            if c + FFN_AHEAD < n_chunks:
                ups.append(up(c + FFN_AHEAD))
            gate_u, val_u = ups.pop(0)
            act = (_gelu_tanh_doubled(conv(gate_u, c, 0)) * conv(val_u, c, 1)).astype(BF16)
            part = _dot(act, wd_ref[c * TN_FFN:(c + 1) * TN_FFN, :])
            acc = part if acc is None else acc + part
        y = xv + _rms(acc, gpost_ref[...])
        for lt in range(n_lane_tiles):
            ys_s[lt] = y[:, lt * LANES:(lt + 1) * LANES]
            for v in range(steps):
                y_ref[row0 + v * SUBLANES:row0 + (v + 1) * SUBLANES, lt * LANES:(lt + 1) * LANES] = (
                    ys_s[lt, strided_rows(v), :])


def _ffn(layer, x2, gpre, w_up, conv_w, conv_b, w_down, gpost):
    n_chunks = D_FF // TN_FFN

    def whole(shape):
        return pl.BlockSpec((None,) + shape, lambda i: (layer, 0, 0), pipeline_mode=pl.Buffered(1))

    return pl.pallas_call(
        _ffn_kernel,
        grid=(ROWS // (FFN_TILES * TM_FFN),),
        in_specs=[
            pl.BlockSpec((FFN_TILES * TM_FFN, D_MODEL), lambda i: (i, 0)),
            whole((1, D_MODEL)),
            whole((D_MODEL, 2 * D_FF)),
            whole((CONV_WIDTH, 2 * D_FF)),
            whole((1, 2 * D_FF)),
            whole((D_FF, D_MODEL)),
            whole((1, D_MODEL)),
        ],
        out_specs=pl.BlockSpec((FFN_TILES * TM_FFN, D_MODEL), lambda i: (i, 0)),
        out_shape=jax.ShapeDtypeStruct((ROWS, D_MODEL), F32),
        scratch_shapes=[
            pltpu.VMEM((FFN_TILES, D_MODEL // LANES, TM_FFN, LANES), F32),
            pltpu.VMEM((D_MODEL // LANES, TM_FFN, LANES), F32),
            pltpu.VMEM((2 * n_chunks, 2 * SUBLANES, TN_FFN), F32),
        ],
        compiler_params=_params("arbitrary"),
        name="conv_ffn",
    )(x2, gpre, w_up, conv_w, conv_b, w_down, gpost)


def _qkv_weight(w_in):
    rows = w_in.shape[:-1]
    a = w_in[..., :A_QKV_COLS].reshape(rows + (3, len(A_GROUPS), A_WIDTH))
    a = a * jnp.asarray([SCALE, 1.0, 1.0], w_in.dtype)[:, None, None]
    a = jnp.swapaxes(a, -3, -2).reshape(rows + (len(A_GROUPS), QKV_W))
    kv = w_in[..., OFF_B_KV:OFF_C].reshape(rows + (2 * B_KV_HEADS, 1, HEAD_DIM))
    kv = jnp.broadcast_to(kv, rows + (2 * B_KV_HEADS, 2, HEAD_DIM)).reshape(rows + (-1,))
    b = jnp.concatenate([w_in[..., OFF_B_Q:OFF_B_KV] * SCALE, kv], axis=-1)
    c = jnp.concatenate([w_in[..., OFF_C:OFF_C + C_WIDTH] * SCALE, w_in[..., OFF_C + C_WIDTH:OFF_GATE]],
                        axis=-1)
    return jnp.concatenate([a[..., 0, :], b, c, a[..., 1, :], a[..., 2, :]], axis=-1).astype(BF16)


def kernel(x, rel_bias, attn_pre_norm, w_in, b_gate, sinks, w_br_a, w_br_b, w_br_c, w_out,
           attn_post_norm, ffn_pre_norm, w_up, conv_w, conv_b, w_down, ffn_post_norm):
    assert x.shape == (BATCH, SEQ, D_MODEL) and x.dtype == F32
    bias_a = _band_bias(rel_bias, 0, N_A_GROUP_HEADS, BLK, True)
    bias_b = _band_bias(rel_bias, N_A_GROUP_HEADS, B_Q_HEADS, B_WINDOW - 1, False)
    x2 = x.reshape(ROWS, D_MODEL)

    def row_param(p):
        return p.reshape(DEPTH, 1, -1)

    w_in16 = w_in.astype(BF16)
    w_qkv = _qkv_weight(w_in16)
    w_a, w_b, w_c, w_o = (w.astype(BF16) for w in (w_br_a, w_br_b, w_br_c, w_out))
    w_up16, w_down16 = w_up.astype(BF16), (w_down * 0.5).astype(BF16)
    attn_pre, attn_post, ffn_pre, ffn_post, gate_b, conv_bias = (
        row_param(p) for p in (attn_pre_norm, attn_post_norm, ffn_pre_norm, ffn_post_norm, b_gate, conv_b))
    for layer in range(DEPTH):
        a0, a1, a2, bq, cq = _in_proj(layer, x2, attn_pre, w_qkv)
        o_a = _attn_a(a0, a1, a2, bias_a)
        o_b = _attn_b(bq, bias_b, sinks[layer])
        o_c = _attn_c(cq)
        x2 = _merge(layer, x2, o_a, o_b, o_c, attn_pre, w_in16, gate_b, w_a, w_b, w_c, w_o, attn_post)
        x2 = _ffn(layer, x2, ffn_pre, w_up16, conv_w, conv_bias, w_down16, ffn_post)
    return x2.reshape(BATCH, SEQ, D_MODEL)
```

```python
import functools
import math

import numpy as np
import jax
import jax.numpy as jnp
from jax import lax
from jax.experimental import pallas as pl
from jax.experimental.pallas import tpu as pltpu

D_MODEL = 1024
BATCH = 8
SEQ = 2048
DEPTH = 2
HEAD_DIM = 64
BLK = 128
A_GROUPS = ((128, 1), (512, 4), (2048, 16))
A_HEADS = 4
B_Q_HEADS = 8
B_KV_HEADS = 2
B_WINDOW = 128
C_HEADS = 4
N_BRANCH = 3
NUM_BUCKETS = 32
MAX_DISTANCE = 2048
D_FF = 4 * D_MODEL
CONV_WIDTH = 3
EPS = 1e-6
SCALE = HEAD_DIM ** -0.5

A_WIDTH = A_HEADS * HEAD_DIM
B_WIDTH = B_Q_HEADS * HEAD_DIM
C_WIDTH = C_HEADS * HEAD_DIM
B_GROUP = B_Q_HEADS // B_KV_HEADS
N_A_GROUP_HEADS = len(A_GROUPS) * A_HEADS
N_BIAS_HEADS = N_A_GROUP_HEADS + B_Q_HEADS
A_QKV_COLS = 3 * N_A_GROUP_HEADS * HEAD_DIM
OFF_B_Q = A_QKV_COLS
OFF_B_KV = OFF_B_Q + B_WIDTH
OFF_C = OFF_B_KV + 2 * B_KV_HEADS * HEAD_DIM
OFF_GATE = OFF_C + 3 * C_WIDTH
GATE_COLS = N_BRANCH * D_MODEL

ROWS = BATCH * SEQ
N_BLK = SEQ // BLK
QKV_W = 3 * A_WIDTH
LANES = 128
B_COLS = B_WIDTH + 2 * B_KV_HEADS * LANES
NEG = -1e30
STICK_DEAD = -104.0

V7X_VMEM_BYTES = 64 * 1024 * 1024
VMEM_LIMIT = V7X_VMEM_BYTES * 7 // 8

TM_PROJ = 512
TILES_PER_STEP = 2
MERGE_COLS = 256
TM_FFN = 512
TN_FFN = 512
FFN_AHEAD = 3
FFN_TILES = 1
SUBLANES = 8
BAND_UNROLL = 4
STICK_UNROLL = 4

BF16 = jnp.bfloat16
F32 = jnp.float32


def _params(*sem):
    return pltpu.CompilerParams(dimension_semantics=sem, vmem_limit_bytes=VMEM_LIMIT)


def _dot(a, b):
    return jnp.dot(a, b, preferred_element_type=F32)


def _dot_nt(a, b):
    return lax.dot_general(a, b, (((1,), (1,)), ((), ())), preferred_element_type=F32)


def _rms(xv, gain):
    y = xv * lax.rsqrt(jnp.mean(xv * xv, axis=-1, keepdims=True) + EPS)
    return y * gain


def _bucket_constants():
    a = np.arange(BLK)[:, None]
    b = np.arange(2 * BLK)[None, :]
    dist = np.maximum(a + BLK - b, 0)
    max_exact = NUM_BUCKETS // 2
    out = []
    for _, d in A_GROUPS:
        n = dist * d
        nf = np.maximum(n, 1).astype(np.float64)
        large = max_exact + (np.log(nf / max_exact) / math.log(MAX_DISTANCE / max_exact)
                             * (NUM_BUCKETS - max_exact)).astype(np.int64)
        large = np.minimum(large, NUM_BUCKETS - 1)
        out.append(np.where(n < max_exact, n, large))
    return np.stack(out).astype(np.int32)


def _bias_kernel(tab_ref, bucket_ref, out_ref, *, head0, max_dist):
    step = pl.program_id(0)
    a = lax.broadcasted_iota(jnp.int32, (BLK, 2 * BLK), 0)
    b = lax.broadcasted_iota(jnp.int32, (BLK, 2 * BLK), 1)
    dist = a + BLK - b
    valid = (dist >= 0) & (dist <= max_dist)
    bk = bucket_ref[...]
    for n in range(A_HEADS):
        acc = jnp.zeros((BLK, 2 * BLK), F32)
        for k in range(NUM_BUCKETS):
            acc = jnp.where(bk == k, tab_ref[k, head0 + step * A_HEADS + n], acc)
        out_ref[0, n] = jnp.where(valid, acc, NEG)
        out_ref[1, n] = jnp.where(valid & (b >= BLK), acc, NEG)


def _band_bias(rel_bias, head0, n_heads, max_dist, dilated):
    buckets = jnp.asarray(_bucket_constants())
    return pl.pallas_call(
        functools.partial(_bias_kernel, head0=head0, max_dist=max_dist),
        grid=(n_heads // A_HEADS,),
        in_specs=[
            pl.BlockSpec(memory_space=pltpu.SMEM),
            pl.BlockSpec((None, BLK, 2 * BLK), lambda s: (s if dilated else 0, 0, 0)),
        ],
        out_specs=pl.BlockSpec((2, A_HEADS, BLK, 2 * BLK), lambda s: (0, s, 0, 0)),
        out_shape=jax.ShapeDtypeStruct((2, n_heads, BLK, 2 * BLK), F32),
        compiler_params=_params("arbitrary"),
        name="band_bias",
    )(rel_bias, buckets)


def _in_proj_kernel(x_ref, g_ref, w_ref, wkv_ref, a0_ref, a1_ref, a2_ref, b_ref, c_ref, ys_s):
    n_lane_tiles = D_MODEL // LANES

    def proj(h, w):
        return _dot(h, w).astype(BF16)

    def a_cols(part, g):
        start = (part * len(A_GROUPS) + g) * A_WIDTH
        return slice(start, start + A_WIDTH)

    y = _rms(x_ref[...], g_ref[...])
    h = y.astype(BF16)
    for part in range(3):
        a0_ref[:, part * A_WIDTH:(part + 1) * A_WIDTH] = proj(h, w_ref[:, a_cols(part, 0)])
    b_ref[:, :B_WIDTH] = proj(h, w_ref[:, OFF_B_Q:OFF_B_KV])
    b_ref[:, B_WIDTH:] = proj(h, wkv_ref[...])
    c_ref[...] = proj(h, w_ref[:, OFF_C:OFF_GATE])
    for c in range(n_lane_tiles):
        ys_s[c] = y[:, c * LANES:(c + 1) * LANES]
    for g, ref in ((1, a1_ref), (2, a2_ref)):
        d = A_GROUPS[g][1]
        per = TM_PROJ // d
        regrouped = jnp.concatenate(
            [jnp.concatenate([ys_s[c, pl.ds(r, per, stride=d), :] for c in range(n_lane_tiles)], axis=1)
             for r in range(d)], axis=0).astype(BF16)
        for part in range(3):
            ref[:, :, part * A_WIDTH:(part + 1) * A_WIDTH] = proj(
                regrouped, w_ref[:, a_cols(part, g)]).reshape(d, per, A_WIDTH)


def _layer_spec(layer, shape):
    return pl.BlockSpec((None,) + shape, lambda i: (layer, 0, 0))


def _in_proj(layer, x2, gain, w_in16, w_kv):
    tiles_per_seq = SEQ // TM_PROJ
    d1, d2 = A_GROUPS[1][1], A_GROUPS[2][1]

    def nat(width):
        return pl.BlockSpec((TM_PROJ, width), lambda i: (i, 0))

    return pl.pallas_call(
        _in_proj_kernel,
        grid=(ROWS // TM_PROJ,),
        in_specs=[
            pl.BlockSpec((TM_PROJ, D_MODEL), lambda i: (i, 0)),
            _layer_spec(layer, (1, D_MODEL)),
            pl.BlockSpec((None, D_MODEL, OFF_GATE + GATE_COLS), lambda i: (layer, 0, 0),
                         pipeline_mode=pl.Buffered(1)),
            _layer_spec(layer, (D_MODEL, B_COLS - B_WIDTH)),
        ],
        out_specs=[
            nat(QKV_W),
            pl.BlockSpec((None, d1, TM_PROJ // d1, QKV_W),
                         lambda i: (i // tiles_per_seq, 0, i % tiles_per_seq, 0)),
            pl.BlockSpec((None, d2, TM_PROJ // d2, QKV_W),
                         lambda i: (i // tiles_per_seq, 0, i % tiles_per_seq, 0)),
            nat(B_COLS),
            nat(QKV_W),
        ],
        out_shape=[
            jax.ShapeDtypeStruct((ROWS, QKV_W), BF16),
            jax.ShapeDtypeStruct((BATCH, d1, SEQ // d1, QKV_W), BF16),
            jax.ShapeDtypeStruct((BATCH, d2, SEQ // d2, QKV_W), BF16),
            jax.ShapeDtypeStruct((ROWS, B_COLS), BF16),
            jax.ShapeDtypeStruct((ROWS, QKV_W), BF16),
        ],
        scratch_shapes=[pltpu.VMEM((D_MODEL // LANES, TM_PROJ, LANES), F32)],
        compiler_params=_params("arbitrary"),
        name="in_proj",
    )(x2, gain, w_in16, w_kv)


def _low_lanes(rows):
    return lax.broadcasted_iota(jnp.int32, (rows, LANES), 1) < HEAD_DIM


def _stack_pair(q2):
    low = _low_lanes(BLK)
    zero = jnp.zeros_like(q2)
    return jnp.concatenate([jnp.where(low, q2, zero), jnp.where(low, zero, q2)], axis=0)


def _unstack_pair(o):
    return jnp.where(_low_lanes(BLK), o[:BLK], o[BLK:])


def _band_blocks(qs, kps, kcs, vps, vcs, biases):
    ss = [_dot_nt(q, jnp.concatenate([kp, kc], axis=0)) for q, kp, kc in zip(qs, kps, kcs)]
    ps, stats = [], []
    for s, bias in zip(ss, biases):
        s = s + bias
        m = jnp.max(s, axis=-1, keepdims=True)
        p = jnp.exp(s - m)
        stats.append((m, jnp.sum(p, axis=-1, keepdims=True)))
        ps.append(p.astype(BF16))
    nums = [_dot(p, jnp.concatenate([vp, vc], axis=0)) for p, vp, vc in zip(ps, vps, vcs)]
    return nums, stats


def _attn_a_kernel(a0_ref, a1_ref, a2_ref, bias_ref, o_ref, out_s, lse_s):
    n_pairs = A_HEADS // 2

    def cols(part, p):
        return slice(part * A_WIDTH + p * LANES, part * A_WIDTH + (p + 1) * LANES)

    for g, (ref, (_, d)) in enumerate(zip((a0_ref, a1_ref, a2_ref), A_GROUPS)):
        nb = N_BLK // d

        def step(n, carry, g=g, ref=ref, d=d, nb=nb):
            units, qs, kps, kcs, vps, vcs, biases = [], [], [], [], [], [], []
            for u in range(BAND_UNROLL):
                blk = n * BAND_UNROLL + u
                r = blk // nb
                i = blk % nb
                row_c = pl.multiple_of(i * BLK, BLK)
                row_p = pl.multiple_of(jnp.maximum(i - 1, 0) * BLK, BLK)
                first = jnp.where(i > 0, 0, 1)
                if d == 1:
                    rows = pl.ds(row_c, BLK)
                else:
                    rows = pl.ds(i * (BLK * d) + r, BLK, stride=d)
                for p in range(n_pairs):
                    units.append((p, rows))
                    heads = slice(g * A_HEADS + 2 * p, g * A_HEADS + 2 * p + 2)
                    qs.append(_stack_pair(ref[r, pl.ds(row_c, BLK), cols(0, p)]))
                    kps.append(ref[r, pl.ds(row_p, BLK), cols(1, p)])
                    kcs.append(ref[r, pl.ds(row_c, BLK), cols(1, p)])
                    vps.append(ref[r, pl.ds(row_p, BLK), cols(2, p)])
                    vcs.append(ref[r, pl.ds(row_c, BLK), cols(2, p)])
                    biases.append(bias_ref[first, heads].reshape(2 * BLK, 2 * BLK))
            nums, stats = _band_blocks(qs, kps, kcs, vps, vcs, biases)
            for (p, rows), num, (m, l) in zip(units, nums, stats):
                out_s[g, p, rows, :] = _unstack_pair(num / l)
                lse_s[g, p, rows, :] = _unstack_pair(jnp.broadcast_to(m + jnp.log(l), (2 * BLK, LANES)))
            return carry

        lax.fori_loop(0, N_BLK // BAND_UNROLL, step, 0)

    def combine(t, carry):
        rows = pl.ds(pl.multiple_of(t * BLK, BLK), BLK)
        for p in range(n_pairs):
            l0, l1, l2 = lse_s[0, p, rows, :], lse_s[1, p, rows, :], lse_s[2, p, rows, :]
            mx = jnp.maximum(jnp.maximum(l0, l1), l2)
            e0, e1, e2 = jnp.exp(l0 - mx), jnp.exp(l1 - mx), jnp.exp(l2 - mx)
            top = e0 * out_s[0, p, rows, :] + e1 * out_s[1, p, rows, :] + e2 * out_s[2, p, rows, :]
            o_ref[rows, p * LANES:(p + 1) * LANES] = (top / (e0 + e1 + e2)).astype(BF16)
        return carry

    lax.fori_loop(0, N_BLK, combine, 0)


def _attn_a(a0, a1, a2, bias):
    d1, d2 = A_GROUPS[1][1], A_GROUPS[2][1]
    return pl.pallas_call(
        _attn_a_kernel,
        grid=(BATCH,),
        in_specs=[
            pl.BlockSpec((None, 1, SEQ, QKV_W), lambda b: (b, 0, 0, 0)),
            pl.BlockSpec((None, d1, SEQ // d1, QKV_W), lambda b: (b, 0, 0, 0)),
            pl.BlockSpec((None, d2, SEQ // d2, QKV_W), lambda b: (b, 0, 0, 0)),
            pl.BlockSpec((2, N_A_GROUP_HEADS, BLK, 2 * BLK), lambda b: (0, 0, 0, 0)),
        ],
        out_specs=pl.BlockSpec((SEQ, A_WIDTH), lambda b: (b, 0)),
        out_shape=jax.ShapeDtypeStruct((ROWS, A_WIDTH), BF16),
        scratch_shapes=[
            pltpu.VMEM((len(A_GROUPS), A_HEADS // 2, SEQ, LANES), F32),
            pltpu.VMEM((len(A_GROUPS), A_HEADS // 2, SEQ, LANES), F32),
        ],
        compiler_params=_params("arbitrary"),
        name="attn_dilated",
    )(a0.reshape(BATCH, 1, SEQ, QKV_W), a1, a2, bias)


def _attn_b_kernel(sink_ref, qkv_ref, bias_ref, o_ref):
    def kv_cols(part, kvh):
        start = B_WIDTH + (part * B_KV_HEADS + kvh) * LANES
        return slice(start, start + LANES)

    def step(n, carry):
        blocks, qs, kps, kcs, vps, vcs, biases = [], [], [], [], [], [], []
        for u in range(BAND_UNROLL):
            i = n * BAND_UNROLL + u
            row_c = pl.multiple_of(i * BLK, BLK)
            row_p = pl.multiple_of(jnp.maximum(i - 1, 0) * BLK, BLK)
            first = jnp.where(i > 0, 0, 1)
            blocks.append(row_c)
            for kvh in range(B_KV_HEADS):
                heads = slice(kvh * B_GROUP, (kvh + 1) * B_GROUP)
                tiles = range(kvh * B_GROUP // 2, (kvh + 1) * B_GROUP // 2)
                qs.append(jnp.concatenate(
                    [_stack_pair(qkv_ref[pl.ds(row_c, BLK), t * LANES:(t + 1) * LANES]) for t in tiles],
                    axis=0))
                kps.append(qkv_ref[pl.ds(row_p, BLK), kv_cols(0, kvh)])
                kcs.append(qkv_ref[pl.ds(row_c, BLK), kv_cols(0, kvh)])
                vps.append(qkv_ref[pl.ds(row_p, BLK), kv_cols(1, kvh)])
                vcs.append(qkv_ref[pl.ds(row_c, BLK), kv_cols(1, kvh)])
                biases.append(bias_ref[first, heads].reshape(B_GROUP * BLK, 2 * BLK))
        nums, stats = _band_blocks(qs, kps, kcs, vps, vcs, biases)
        for u, row_c in enumerate(blocks):
            for kvh in range(B_KV_HEADS):
                num = nums[u * B_KV_HEADS + kvh]
                m, l = stats[u * B_KV_HEADS + kvh]
                scaled = []
                for gq in range(B_GROUP):
                    rows = slice(gq * BLK, (gq + 1) * BLK)
                    sink = sink_ref[0, kvh * B_GROUP + gq]
                    scaled.append(num[rows] / (l[rows] + jnp.exp(sink - m[rows])))
                for t in range(B_GROUP // 2):
                    tile = kvh * B_GROUP // 2 + t
                    o_ref[pl.ds(row_c, BLK), tile * LANES:(tile + 1) * LANES] = _unstack_pair(
                        jnp.concatenate(scaled[2 * t:2 * t + 2], axis=0)).astype(BF16)
        return carry

    lax.fori_loop(0, N_BLK // BAND_UNROLL, step, 0)


def _attn_b(qkv, bias, sinks):
    return pl.pallas_call(
        _attn_b_kernel,
        grid=(BATCH,),
        in_specs=[
            pl.BlockSpec(memory_space=pltpu.SMEM),
            pl.BlockSpec((SEQ, B_COLS), lambda b: (b, 0)),
            pl.BlockSpec((2, B_Q_HEADS, BLK, 2 * BLK), lambda b: (0, 0, 0, 0)),
        ],
        out_specs=pl.BlockSpec((SEQ, B_WIDTH), lambda b: (b, 0)),
        out_shape=jax.ShapeDtypeStruct((ROWS, B_WIDTH), BF16),
        compiler_params=_params("arbitrary"),
        name="attn_window",
    )(sinks.reshape(1, B_Q_HEADS), qkv, bias)


def _suffix_matrix():
    j = np.arange(2 * BLK)[:, None] % BLK
    s = np.arange(2 * BLK)[None, :]
    return ((s >= BLK) | (j > s)).astype(np.float32)


def _stick_blocks(qs, ks, vs, suffix, laters, strict_lower):
    zs = [_dot_nt(q, k) for q, k in zip(qs, ks)]
    log_betas, splits = [], []
    for z in zs:
        soft = jnp.log(1.0 + jnp.exp(-jnp.abs(z)))
        log_keep = -(jnp.maximum(z, 0.0) + soft)
        log_betas.append(z + log_keep)
        if laters is None:
            log_keep = jnp.where(strict_lower, log_keep, 0.0)
        hi = log_keep.astype(BF16)
        lo = (log_keep - hi.astype(F32)).astype(BF16)
        splits.append(jnp.concatenate([hi, lo], axis=1))
    sums = [_dot(s, suffix) for s in splits]
    ws, new_laters = [], []
    for n, (log_beta, s) in enumerate(zip(log_betas, sums)):
        if laters is None:
            w = jnp.where(strict_lower, jnp.exp(log_beta + s[:, :BLK]), 0.0)
            new_laters.append(s[:, BLK:])
        else:
            w = jnp.exp(log_beta + s[:, :BLK] + laters[n])
            new_laters.append(laters[n] + s[:, BLK:])
        ws.append(w.astype(BF16))
    return [_unstack_pair(_dot(w, v)) for w, v in zip(ws, vs)], new_laters


def _stick_schedule():
    q_blk, slot, k_blk, first_step = [], [], [], [0, 0]
    for dist in range(1, N_BLK):
        group = [(i, i, i - dist) for i in range(dist, N_BLK)]
        group += [(0, N_BLK, 0)] * (-len(group) % STICK_UNROLL)
        for qi, sl, kj in group:
            q_blk.append(qi)
            slot.append(sl)
            k_blk.append(kj)
        first_step.append(len(q_blk) // STICK_UNROLL)
    return tuple(np.asarray(t, np.int32) for t in (q_blk, slot, k_blk, first_step))


def _attn_c_kernel(qblk_ref, slot_ref, kblk_ref, step_ref, qkv_ref, suffix_ref, o_ref, later_s, acc_s):
    n_pairs = C_HEADS // 2
    t_idx = lax.broadcasted_iota(jnp.int32, (2 * BLK, BLK), 0) % BLK
    s_idx = lax.broadcasted_iota(jnp.int32, (2 * BLK, BLK), 1)
    strict_lower = s_idx < t_idx

    def cols(part, p):
        return slice(part * C_WIDTH + p * LANES, part * C_WIDTH + (p + 1) * LANES)

    def rows_of(blk):
        return pl.ds(pl.multiple_of(blk * BLK, BLK), BLK)

    for p in range(n_pairs):
        later_s[p, N_BLK] = jnp.zeros((2 * BLK, BLK), F32)
        acc_s[p, N_BLK] = jnp.zeros((BLK, LANES), F32)

    def diag(n, carry):
        where = [(p, n * STICK_UNROLL + u) for u in range(STICK_UNROLL) for p in range(n_pairs)]
        parts, laters = _stick_blocks(
            [_stack_pair(qkv_ref[rows_of(i), cols(0, p)]) for p, i in where],
            [qkv_ref[rows_of(i), cols(1, p)] for p, i in where],
            [qkv_ref[rows_of(i), cols(2, p)] for p, i in where],
            suffix_ref[...], None, strict_lower)
        for (p, i), part, later in zip(where, parts, laters):
            acc_s[p, i] = part
            later_s[p, i] = later
        return carry

    lax.fori_loop(0, N_BLK // STICK_UNROLL, diag, 0)

    def off_diag(n, carry):
        where = []
        for u in range(STICK_UNROLL):
            e = n * STICK_UNROLL + u
            q_rows, k_rows, sl = rows_of(qblk_ref[e]), rows_of(kblk_ref[e]), slot_ref[e]
            where += [(p, sl, q_rows, k_rows) for p in range(n_pairs)]
        accs = [acc_s[p, sl] for p, sl, _, _ in where]
        parts, laters = _stick_blocks(
            [_stack_pair(qkv_ref[q_rows, cols(0, p)]) for p, _, q_rows, _ in where],
            [qkv_ref[k_rows, cols(1, p)] for p, _, _, k_rows in where],
            [qkv_ref[k_rows, cols(2, p)] for p, _, _, k_rows in where],
            suffix_ref[...], [later_s[p, sl] for p, sl, _, _ in where], None)
        for (p, sl, _, _), acc, part, later in zip(where, accs, parts, laters):
            acc_s[p, sl] = acc + part
            later_s[p, sl] = later
        return carry

    def group(state):
        dist, _ = state
        lax.fori_loop(step_ref[dist], step_ref[dist + 1], off_diag, 0)

        def slot_max(i, mx):
            for p in range(n_pairs):
                mx = jnp.maximum(mx, later_s[p, i])
            return mx

        mx = lax.fori_loop(dist + 1, N_BLK, slot_max, jnp.full((2 * BLK, BLK), -jnp.inf, F32))
        return dist + 1, jnp.logical_not(jnp.max(mx) < STICK_DEAD)

    lax.while_loop(lambda state: (state[0] < N_BLK) & state[1], group, (jnp.int32(1), jnp.bool_(True)))

    def emit(i, carry):
        for p in range(n_pairs):
            o_ref[rows_of(i), p * LANES:(p + 1) * LANES] = acc_s[p, i].astype(BF16)
        return carry

    lax.fori_loop(0, N_BLK, emit, 0)


def _attn_c(qkv):
    suffix = jnp.asarray(_suffix_matrix(), BF16)
    schedule = [jnp.asarray(t) for t in _stick_schedule()]
    smem = pl.BlockSpec(memory_space=pltpu.SMEM)
    return pl.pallas_call(
        _attn_c_kernel,
        grid=(BATCH,),
        in_specs=[
            smem, smem, smem, smem,
            pl.BlockSpec((SEQ, QKV_W), lambda b: (b, 0)),
            pl.BlockSpec((2 * BLK, 2 * BLK), lambda b: (0, 0)),
        ],
        out_specs=pl.BlockSpec((SEQ, C_WIDTH), lambda b: (b, 0)),
        out_shape=jax.ShapeDtypeStruct((ROWS, C_WIDTH), BF16),
        scratch_shapes=[
            pltpu.VMEM((C_HEADS // 2, N_BLK + 1, 2 * BLK, BLK), F32),
            pltpu.VMEM((C_HEADS // 2, N_BLK + 1, BLK, LANES), F32),
        ],
        compiler_params=_params("arbitrary"),
        name="attn_stick",
    )(*schedule, qkv, suffix)


def _merge_kernel(x_ref, oa_ref, ob_ref, oc_ref, gpre_ref, win_ref, bg_ref, wa_ref, wb_ref, wc_ref,
                  wo_ref, gpost_ref, y_ref):
    for t in range(TILES_PER_STEP):
        rows = slice(t * TM_PROJ, (t + 1) * TM_PROJ)
        xv = x_ref[rows, :]
        h = _rms(xv, gpre_ref[...]).astype(BF16)
        slabs = []
        for c in range(D_MODEL // MERGE_COLS):
            merged = None
            for n, (o_ref, w_ref) in enumerate(((oa_ref, wa_ref), (ob_ref, wb_ref), (oc_ref, wc_ref))):
                cols = slice(n * D_MODEL + c * MERGE_COLS, n * D_MODEL + (c + 1) * MERGE_COLS)
                gate_w = win_ref[:, OFF_GATE + cols.start:OFF_GATE + cols.stop]
                gate = jax.nn.sigmoid(_dot(h, gate_w) + bg_ref[:, cols])
                term = gate * _dot(o_ref[rows, :], w_ref[:, c * MERGE_COLS:(c + 1) * MERGE_COLS])
                merged = term if merged is None else merged + term
            slabs.append(merged.astype(BF16))
        y = _dot(jnp.concatenate(slabs, axis=1), wo_ref[...])
        y_ref[rows, :] = xv + _rms(y, gpost_ref[...])


def _merge(layer, x2, o_a, o_b, o_c, gpre, w_in16, b_gate, w_a, w_b, w_c, w_o, gpost):
    def rows(width):
        return pl.BlockSpec((TILES_PER_STEP * TM_PROJ, width), lambda i: (i, 0))

    def whole(shape):
        return _layer_spec(layer, shape)

    return pl.pallas_call(
        _merge_kernel,
        grid=(ROWS // (TILES_PER_STEP * TM_PROJ),),
        in_specs=[
            rows(D_MODEL), rows(A_WIDTH), rows(B_WIDTH), rows(C_WIDTH),
            whole((1, D_MODEL)),
            pl.BlockSpec((None, D_MODEL, OFF_GATE + GATE_COLS), lambda i: (layer, 0, 0),
                         pipeline_mode=pl.Buffered(1)),
            whole((1, GATE_COLS)),
            whole((A_WIDTH, D_MODEL)), whole((B_WIDTH, D_MODEL)), whole((C_WIDTH, D_MODEL)),
            whole((D_MODEL, D_MODEL)), whole((1, D_MODEL)),
        ],
        out_specs=rows(D_MODEL),
        out_shape=jax.ShapeDtypeStruct((ROWS, D_MODEL), F32),
        compiler_params=_params("arbitrary"),
        name="merge_out_proj",
    )(x2, o_a, o_b, o_c, gpre, w_in16, b_gate, w_a, w_b, w_c, w_o, gpost)


def _gelu_tanh_doubled(x):
    k = math.sqrt(2.0 / math.pi)
    return x * (1.0 + jnp.tanh(x * (k + (k * 0.044715) * (x * x))))


def _ffn_kernel(x_ref, gpre_ref, wup_ref, cw_ref, cb_ref, wd_ref, gpost_ref, y_ref, xs_s, ys_s, tail_s):
    i = pl.program_id(0)
    n_chunks = D_FF // TN_FFN
    steps = TM_FFN // SUBLANES
    n_lane_tiles = D_MODEL // LANES

    def strided_rows(v):
        s, j0 = divmod(v * SUBLANES, steps)
        return pl.ds(j0 * SUBLANES + s, SUBLANES, stride=SUBLANES)

    for t in range(FFN_TILES):
        row0 = t * TM_FFN
        for lt in range(n_lane_tiles):
            for v in range(steps):
                xs_s[t, lt, strided_rows(v), :] = x_ref[row0 + v * SUBLANES:row0 + (v + 1) * SUBLANES,
                                                        lt * LANES:(lt + 1) * LANES]
        xv = jnp.concatenate([xs_s[t, lt] for lt in range(n_lane_tiles)], axis=1)
        h = _rms(xv, gpre_ref[...]).astype(BF16)
        seq_start = ((i * FFN_TILES + t) % (SEQ // TM_FFN)) == 0

        def up(c, h=h):
            return [_dot(h, wup_ref[:, half * D_FF + c * TN_FFN:half * D_FF + (c + 1) * TN_FFN])
                    for half in range(2)]

        def conv(u, c, half, seq_start=seq_start):
            cols = slice(half * D_FF + c * TN_FFN, half * D_FF + (c + 1) * TN_FFN)
            last2 = u[TM_FFN - 2 * SUBLANES:, :]
            prev2 = jnp.where(seq_start, 0.0, tail_s[2 * c + half])
            tail_s[2 * c + half] = last2
            wrapped = [jnp.concatenate([prev2[g * SUBLANES + SUBLANES - 1:(g + 1) * SUBLANES],
                                        last2[g * SUBLANES:(g + 1) * SUBLANES - 1]], axis=0)
                       for g in range(2)]
            back1 = jnp.concatenate([wrapped[1], u[:TM_FFN - SUBLANES]], axis=0)
            back2 = jnp.concatenate([wrapped[0], wrapped[1], u[:TM_FFN - 2 * SUBLANES]], axis=0)
            return (cw_ref[2:3, cols] * u + cw_ref[1:2, cols] * back1 + cw_ref[0:1, cols] * back2
                    + cb_ref[:, cols])

        acc = None
        ups = [up(c) for c in range(FFN_AHEAD)]
        for c in range(n_chunks):
            if c + FFN_AHEAD < n_chunks:
                ups.append(up(c + FFN_AHEAD))
            gate_u, val_u = ups.pop(0)
            act = (_gelu_tanh_doubled(conv(gate_u, c, 0)) * conv(val_u, c, 1)).astype(BF16)
            part = _dot(act, wd_ref[c * TN_FFN:(c + 1) * TN_FFN, :])
            acc = part if acc is None else acc + part
        y = xv + _rms(acc, gpost_ref[...])
        for lt in range(n_lane_tiles):
            ys_s[lt] = y[:, lt * LANES:(lt + 1) * LANES]
            for v in range(steps):
                y_ref[row0 + v * SUBLANES:row0 + (v + 1) * SUBLANES, lt * LANES:(lt + 1) * LANES] = (
                    ys_s[lt, strided_rows(v), :])


def _ffn(layer, x2, gpre, w_up, conv_w, conv_b, w_down, gpost):
    n_chunks = D_FF // TN_FFN

    def whole(shape):
        return pl.BlockSpec((None,) + shape, lambda i: (layer, 0, 0), pipeline_mode=pl.Buffered(1))

    return pl.pallas_call(
        _ffn_kernel,
        grid=(ROWS // (FFN_TILES * TM_FFN),),
        in_specs=[
            pl.BlockSpec((FFN_TILES * TM_FFN, D_MODEL), lambda i: (i, 0)),
            whole((1, D_MODEL)),
            whole((D_MODEL, 2 * D_FF)),
            whole((CONV_WIDTH, 2 * D_FF)),
            whole((1, 2 * D_FF)),
            whole((D_FF, D_MODEL)),
            whole((1, D_MODEL)),
        ],
        out_specs=pl.BlockSpec((FFN_TILES * TM_FFN, D_MODEL), lambda i: (i, 0)),
        out_shape=jax.ShapeDtypeStruct((ROWS, D_MODEL), F32),
        scratch_shapes=[
            pltpu.VMEM((FFN_TILES, D_MODEL // LANES, TM_FFN, LANES), F32),
            pltpu.VMEM((D_MODEL // LANES, TM_FFN, LANES), F32),
            pltpu.VMEM((2 * n_chunks, 2 * SUBLANES, TN_FFN), F32),
        ],
        compiler_params=_params("arbitrary"),
        name="conv_ffn",
    )(x2, gpre, w_up, conv_w, conv_b, w_down, gpost)


def _q_column_scale():
    scale = np.ones((OFF_GATE + GATE_COLS,), np.float32)
    scale[:N_A_GROUP_HEADS * HEAD_DIM] = SCALE
    scale[OFF_B_Q:OFF_B_KV] = SCALE
    scale[OFF_C:OFF_C + C_WIDTH] = SCALE
    return scale


def _doubled_kv(w_in16):
    rows = w_in16.shape[:-1]
    kv = w_in16[..., OFF_B_KV:OFF_C].reshape(rows + (2 * B_KV_HEADS, 1, HEAD_DIM))
    return jnp.broadcast_to(kv, rows + (2 * B_KV_HEADS, 2, HEAD_DIM)).reshape(rows + (-1,))


def kernel(x, rel_bias, attn_pre_norm, w_in, b_gate, sinks, w_br_a, w_br_b, w_br_c, w_out,
           attn_post_norm, ffn_pre_norm, w_up, conv_w, conv_b, w_down, ffn_post_norm):
    assert x.shape == (BATCH, SEQ, D_MODEL) and x.dtype == F32
    bias_a = _band_bias(rel_bias, 0, N_A_GROUP_HEADS, BLK, True)
    bias_b = _band_bias(rel_bias, N_A_GROUP_HEADS, B_Q_HEADS, B_WINDOW - 1, False)
    x2 = x.reshape(ROWS, D_MODEL)

    def row_param(p):
        return p.reshape(DEPTH, 1, -1)

    w_in16 = (w_in * _q_column_scale()).astype(BF16)
    w_kv = _doubled_kv(w_in16)
    w_a, w_b, w_c, w_o = (w.astype(BF16) for w in (w_br_a, w_br_b, w_br_c, w_out))
    w_up16, w_down16 = w_up.astype(BF16), (w_down * 0.5).astype(BF16)
    attn_pre, attn_post, ffn_pre, ffn_post, gate_b, conv_bias = (
        row_param(p) for p in (attn_pre_norm, attn_post_norm, ffn_pre_norm, ffn_post_norm, b_gate, conv_b))
    for layer in range(DEPTH):
        a0, a1, a2, bq, cq = _in_proj(layer, x2, attn_pre, w_in16, w_kv)
        o_a = _attn_a(a0, a1, a2, bias_a)
        o_b = _attn_b(bq, bias_b, sinks[layer])
        o_c = _attn_c(cq)
        x2 = _merge(layer, x2, o_a, o_b, o_c, attn_pre, w_in16, gate_b, w_a, w_b, w_c, w_o, attn_post)
        x2 = _ffn(layer, x2, ffn_pre, w_up16, conv_w, conv_bias, w_down16, ffn_post)
    return x2.reshape(BATCH, SEQ, D_MODEL)
```

```python
import functools
import math

import numpy as np
import jax
import jax.numpy as jnp
from jax import lax
from jax.experimental import pallas as pl
from jax.experimental.pallas import tpu as pltpu

D_MODEL = 1024
BATCH = 8
SEQ = 2048
DEPTH = 2
HEAD_DIM = 64
BLK = 128
A_GROUPS = ((128, 1), (512, 4), (2048, 16))
A_HEADS = 4
B_Q_HEADS = 8
B_KV_HEADS = 2
B_WINDOW = 128
C_HEADS = 4
N_BRANCH = 3
NUM_BUCKETS = 32
MAX_DISTANCE = 2048
D_FF = 4 * D_MODEL
CONV_WIDTH = 3
EPS = 1e-6
SCALE = HEAD_DIM ** -0.5

A_WIDTH = A_HEADS * HEAD_DIM
B_WIDTH = B_Q_HEADS * HEAD_DIM
C_WIDTH = C_HEADS * HEAD_DIM
B_GROUP = B_Q_HEADS // B_KV_HEADS
N_A_GROUP_HEADS = len(A_GROUPS) * A_HEADS
N_BIAS_HEADS = N_A_GROUP_HEADS + B_Q_HEADS
A_QKV_COLS = 3 * N_A_GROUP_HEADS * HEAD_DIM
OFF_B_Q = A_QKV_COLS
OFF_B_KV = OFF_B_Q + B_WIDTH
OFF_C = OFF_B_KV + 2 * B_KV_HEADS * HEAD_DIM
OFF_GATE = OFF_C + 3 * C_WIDTH
GATE_COLS = N_BRANCH * D_MODEL

ROWS = BATCH * SEQ
N_BLK = SEQ // BLK
QKV_W = 3 * A_WIDTH
LANES = 128
B_COLS = B_WIDTH + 2 * B_KV_HEADS * LANES
NEG = -1e30
STICK_DEAD = -104.0

V7X_VMEM_BYTES = 64 * 1024 * 1024
VMEM_LIMIT = V7X_VMEM_BYTES * 7 // 8

TM_PROJ = 512
TILES_PER_STEP = 2
MERGE_COLS = 256
TM_FFN = 512
TN_FFN = 512
FFN_AHEAD = 3
FFN_TILES = 1
SUBLANES = 8
BAND_UNROLL = 4
STICK_UNROLL = 4

BF16 = jnp.bfloat16
F32 = jnp.float32


def _params(*sem):
    return pltpu.CompilerParams(dimension_semantics=sem, vmem_limit_bytes=VMEM_LIMIT)


def _dot(a, b):
    return jnp.dot(a, b, preferred_element_type=F32)


def _dot_nt(a, b):
    return lax.dot_general(a, b, (((1,), (1,)), ((), ())), preferred_element_type=F32)


def _rms(xv, gain):
    y = xv * lax.rsqrt(jnp.mean(xv * xv, axis=-1, keepdims=True) + EPS)
    return y * gain


def _bucket_constants():
    a = np.arange(BLK)[:, None]
    b = np.arange(2 * BLK)[None, :]
    dist = np.maximum(a + BLK - b, 0)
    max_exact = NUM_BUCKETS // 2
    out = []
    for _, d in A_GROUPS:
        n = dist * d
        nf = np.maximum(n, 1).astype(np.float64)
        large = max_exact + (np.log(nf / max_exact) / math.log(MAX_DISTANCE / max_exact)
                             * (NUM_BUCKETS - max_exact)).astype(np.int64)
        large = np.minimum(large, NUM_BUCKETS - 1)
        out.append(np.where(n < max_exact, n, large))
    return np.stack(out).astype(np.int32)


def _bias_kernel(tab_ref, bucket_ref, out_ref, *, head0, max_dist):
    step = pl.program_id(0)
    a = lax.broadcasted_iota(jnp.int32, (BLK, 2 * BLK), 0)
    b = lax.broadcasted_iota(jnp.int32, (BLK, 2 * BLK), 1)
    dist = a + BLK - b
    valid = (dist >= 0) & (dist <= max_dist)
    bk = bucket_ref[...]
    for n in range(A_HEADS):
        acc = jnp.zeros((BLK, 2 * BLK), F32)
        for k in range(NUM_BUCKETS):
            acc = jnp.where(bk == k, tab_ref[k, head0 + step * A_HEADS + n], acc)
        out_ref[0, n] = jnp.where(valid, acc, NEG)
        out_ref[1, n] = jnp.where(valid & (b >= BLK), acc, NEG)


def _band_bias(rel_bias, head0, n_heads, max_dist, dilated):
    buckets = jnp.asarray(_bucket_constants())
    return pl.pallas_call(
        functools.partial(_bias_kernel, head0=head0, max_dist=max_dist),
        grid=(n_heads // A_HEADS,),
        in_specs=[
            pl.BlockSpec(memory_space=pltpu.SMEM),
            pl.BlockSpec((None, BLK, 2 * BLK), lambda s: (s if dilated else 0, 0, 0)),
        ],
        out_specs=pl.BlockSpec((2, A_HEADS, BLK, 2 * BLK), lambda s: (0, s, 0, 0)),
        out_shape=jax.ShapeDtypeStruct((2, n_heads, BLK, 2 * BLK), F32),
        compiler_params=_params("arbitrary"),
        name="band_bias",
    )(rel_bias, buckets)


def _in_proj_kernel(x_ref, g_ref, w_ref, wkv_ref, a0_ref, a1_ref, a2_ref, b_ref, c_ref, ys_s):
    n_lane_tiles = D_MODEL // LANES

    def proj(h, w):
        return _dot(h, w).astype(BF16)

    def a_cols(part, g):
        start = (part * len(A_GROUPS) + g) * A_WIDTH
        return slice(start, start + A_WIDTH)

    for t in range(TILES_PER_STEP):
        rows = slice(t * TM_PROJ, (t + 1) * TM_PROJ)
        y = _rms(x_ref[rows, :], g_ref[...])
        h = y.astype(BF16)
        for part in range(3):
            a0_ref[rows, part * A_WIDTH:(part + 1) * A_WIDTH] = proj(h, w_ref[:, a_cols(part, 0)])
        b_ref[rows, :B_WIDTH] = proj(h, w_ref[:, OFF_B_Q:OFF_B_KV])
        b_ref[rows, B_WIDTH:] = proj(h, wkv_ref[...])
        c_ref[rows, :] = proj(h, w_ref[:, OFF_C:OFF_GATE])
        for c in range(n_lane_tiles):
            ys_s[t, c] = y[:, c * LANES:(c + 1) * LANES]
        for g, ref in ((1, a1_ref), (2, a2_ref)):
            d = A_GROUPS[g][1]
            per = TM_PROJ // d
            regrouped = jnp.concatenate(
                [jnp.concatenate([ys_s[t, c, pl.ds(r, per, stride=d), :] for c in range(n_lane_tiles)],
                                 axis=1) for r in range(d)], axis=0).astype(BF16)
            for part in range(3):
                ref[:, t * per:(t + 1) * per, part * A_WIDTH:(part + 1) * A_WIDTH] = proj(
                    regrouped, w_ref[:, a_cols(part, g)]).reshape(d, per, A_WIDTH)


def _layer_spec(layer, shape):
    return pl.BlockSpec((None,) + shape, lambda i: (layer, 0, 0))


def _in_proj(layer, x2, gain, w_in16, w_kv):
    step_rows = TILES_PER_STEP * TM_PROJ
    tiles_per_seq = SEQ // step_rows
    d1, d2 = A_GROUPS[1][1], A_GROUPS[2][1]

    def nat(width):
        return pl.BlockSpec((step_rows, width), lambda i: (i, 0))

    return pl.pallas_call(
        _in_proj_kernel,
        grid=(ROWS // step_rows,),
        in_specs=[
            pl.BlockSpec((step_rows, D_MODEL), lambda i: (i, 0)),
            _layer_spec(layer, (1, D_MODEL)),
            pl.BlockSpec((None, D_MODEL, OFF_GATE + GATE_COLS), lambda i: (layer, 0, 0),
                         pipeline_mode=pl.Buffered(1)),
            _layer_spec(layer, (D_MODEL, B_COLS - B_WIDTH)),
        ],
        out_specs=[
            nat(QKV_W),
            pl.BlockSpec((None, d1, step_rows // d1, QKV_W),
                         lambda i: (i // tiles_per_seq, 0, i % tiles_per_seq, 0)),
            pl.BlockSpec((None, d2, step_rows // d2, QKV_W),
                         lambda i: (i // tiles_per_seq, 0, i % tiles_per_seq, 0)),
            nat(B_COLS),
            nat(QKV_W),
        ],
        out_shape=[
            jax.ShapeDtypeStruct((ROWS, QKV_W), BF16),
            jax.ShapeDtypeStruct((BATCH, d1, SEQ // d1, QKV_W), BF16),
            jax.ShapeDtypeStruct((BATCH, d2, SEQ // d2, QKV_W), BF16),
            jax.ShapeDtypeStruct((ROWS, B_COLS), BF16),
            jax.ShapeDtypeStruct((ROWS, QKV_W), BF16),
        ],
        scratch_shapes=[pltpu.VMEM((TILES_PER_STEP, D_MODEL // LANES, TM_PROJ, LANES), F32)],
        compiler_params=_params("arbitrary"),
        name="in_proj",
    )(x2, gain, w_in16, w_kv)


def _low_lanes(rows):
    return lax.broadcasted_iota(jnp.int32, (rows, LANES), 1) < HEAD_DIM


def _stack_pair(q2):
    low = _low_lanes(BLK)
    zero = jnp.zeros_like(q2)
    return jnp.concatenate([jnp.where(low, q2, zero), jnp.where(low, zero, q2)], axis=0)


def _unstack_pair(o):
    return jnp.where(_low_lanes(BLK), o[:BLK], o[BLK:])


def _band_blocks(qs, kps, kcs, vps, vcs, biases):
    ss = [_dot_nt(q, jnp.concatenate([kp, kc], axis=0)) for q, kp, kc in zip(qs, kps, kcs)]
    ps, stats = [], []
    for s, bias in zip(ss, biases):
        s = s + bias
        m = jnp.max(s, axis=-1, keepdims=True)
        p = jnp.exp(s - m)
        stats.append((m, jnp.sum(p, axis=-1, keepdims=True)))
        ps.append(p.astype(BF16))
    nums = [_dot(p, jnp.concatenate([vp, vc], axis=0)) for p, vp, vc in zip(ps, vps, vcs)]
    return nums, stats


def _attn_a_kernel(a0_ref, a1_ref, a2_ref, bias_ref, o_ref, out_s, lse_s):
    n_pairs = A_HEADS // 2

    def cols(part, p):
        return slice(part * A_WIDTH + p * LANES, part * A_WIDTH + (p + 1) * LANES)

    for g, (ref, (_, d)) in enumerate(zip((a0_ref, a1_ref, a2_ref), A_GROUPS)):
        nb = N_BLK // d

        def step(n, carry, g=g, ref=ref, d=d, nb=nb):
            units, qs, kps, kcs, vps, vcs, biases = [], [], [], [], [], [], []
            for u in range(BAND_UNROLL):
                blk = n * BAND_UNROLL + u
                r = blk // nb
                i = blk % nb
                row_c = pl.multiple_of(i * BLK, BLK)
                row_p = pl.multiple_of(jnp.maximum(i - 1, 0) * BLK, BLK)
                first = jnp.where(i > 0, 0, 1)
                if d == 1:
                    rows = pl.ds(row_c, BLK)
                else:
                    rows = pl.ds(i * (BLK * d) + r, BLK, stride=d)
                for p in range(n_pairs):
                    units.append((p, rows))
                    heads = slice(g * A_HEADS + 2 * p, g * A_HEADS + 2 * p + 2)
                    qs.append(_stack_pair(ref[r, pl.ds(row_c, BLK), cols(0, p)]))
                    kps.append(ref[r, pl.ds(row_p, BLK), cols(1, p)])
                    kcs.append(ref[r, pl.ds(row_c, BLK), cols(1, p)])
                    vps.append(ref[r, pl.ds(row_p, BLK), cols(2, p)])
                    vcs.append(ref[r, pl.ds(row_c, BLK), cols(2, p)])
                    biases.append(bias_ref[first, heads].reshape(2 * BLK, 2 * BLK))
            nums, stats = _band_blocks(qs, kps, kcs, vps, vcs, biases)
            for (p, rows), num, (m, l) in zip(units, nums, stats):
                out_s[g, p, rows, :] = _unstack_pair(num / l)
                lse_s[g, p, rows, :] = _unstack_pair(jnp.broadcast_to(m + jnp.log(l), (2 * BLK, LANES)))
            return carry

        lax.fori_loop(0, N_BLK // BAND_UNROLL, step, 0)

    def combine(t, carry):
        rows = pl.ds(pl.multiple_of(t * BLK, BLK), BLK)
        for p in range(n_pairs):
            l0, l1, l2 = lse_s[0, p, rows, :], lse_s[1, p, rows, :], lse_s[2, p, rows, :]
            mx = jnp.maximum(jnp.maximum(l0, l1), l2)
            e0, e1, e2 = jnp.exp(l0 - mx), jnp.exp(l1 - mx), jnp.exp(l2 - mx)
            top = e0 * out_s[0, p, rows, :] + e1 * out_s[1, p, rows, :] + e2 * out_s[2, p, rows, :]
            o_ref[rows, p * LANES:(p + 1) * LANES] = (top / (e0 + e1 + e2)).astype(BF16)
        return carry

    lax.fori_loop(0, N_BLK, combine, 0)


def _attn_a(a0, a1, a2, bias):
    d1, d2 = A_GROUPS[1][1], A_GROUPS[2][1]
    return pl.pallas_call(
        _attn_a_kernel,
        grid=(BATCH,),
        in_specs=[
            pl.BlockSpec((None, 1, SEQ, QKV_W), lambda b: (b, 0, 0, 0)),
            pl.BlockSpec((None, d1, SEQ // d1, QKV_W), lambda b: (b, 0, 0, 0)),
            pl.BlockSpec((None, d2, SEQ // d2, QKV_W), lambda b: (b, 0, 0, 0)),
            pl.BlockSpec((2, N_A_GROUP_HEADS, BLK, 2 * BLK), lambda b: (0, 0, 0, 0)),
        ],
        out_specs=pl.BlockSpec((SEQ, A_WIDTH), lambda b: (b, 0)),
        out_shape=jax.ShapeDtypeStruct((ROWS, A_WIDTH), BF16),
        scratch_shapes=[
            pltpu.VMEM((len(A_GROUPS), A_HEADS // 2, SEQ, LANES), F32),
            pltpu.VMEM((len(A_GROUPS), A_HEADS // 2, SEQ, LANES), F32),
        ],
        compiler_params=_params("arbitrary"),
        name="attn_dilated",
    )(a0.reshape(BATCH, 1, SEQ, QKV_W), a1, a2, bias)


def _attn_b_kernel(sink_ref, qkv_ref, bias_ref, o_ref):
    def kv_cols(part, kvh):
        start = B_WIDTH + (part * B_KV_HEADS + kvh) * LANES
        return slice(start, start + LANES)

    def step(n, carry):
        blocks, qs, kps, kcs, vps, vcs, biases = [], [], [], [], [], [], []
        for u in range(BAND_UNROLL):
            i = n * BAND_UNROLL + u
            row_c = pl.multiple_of(i * BLK, BLK)
            row_p = pl.multiple_of(jnp.maximum(i - 1, 0) * BLK, BLK)
            first = jnp.where(i > 0, 0, 1)
            blocks.append(row_c)
            for kvh in range(B_KV_HEADS):
                heads = slice(kvh * B_GROUP, (kvh + 1) * B_GROUP)
                tiles = range(kvh * B_GROUP // 2, (kvh + 1) * B_GROUP // 2)
                qs.append(jnp.concatenate(
                    [_stack_pair(qkv_ref[pl.ds(row_c, BLK), t * LANES:(t + 1) * LANES]) for t in tiles],
                    axis=0))
                kps.append(qkv_ref[pl.ds(row_p, BLK), kv_cols(0, kvh)])
                kcs.append(qkv_ref[pl.ds(row_c, BLK), kv_cols(0, kvh)])
                vps.append(qkv_ref[pl.ds(row_p, BLK), kv_cols(1, kvh)])
                vcs.append(qkv_ref[pl.ds(row_c, BLK), kv_cols(1, kvh)])
                biases.append(bias_ref[first, heads].reshape(B_GROUP * BLK, 2 * BLK))
        nums, stats = _band_blocks(qs, kps, kcs, vps, vcs, biases)
        for u, row_c in enumerate(blocks):
            for kvh in range(B_KV_HEADS):
                num = nums[u * B_KV_HEADS + kvh]
                m, l = stats[u * B_KV_HEADS + kvh]
                scaled = []
                for gq in range(B_GROUP):
                    rows = slice(gq * BLK, (gq + 1) * BLK)
                    sink = sink_ref[0, kvh * B_GROUP + gq]
                    scaled.append(num[rows] / (l[rows] + jnp.exp(sink - m[rows])))
                for t in range(B_GROUP // 2):
                    tile = kvh * B_GROUP // 2 + t
                    o_ref[pl.ds(row_c, BLK), tile * LANES:(tile + 1) * LANES] = _unstack_pair(
                        jnp.concatenate(scaled[2 * t:2 * t + 2], axis=0)).astype(BF16)
        return carry

    lax.fori_loop(0, N_BLK // BAND_UNROLL, step, 0)


def _attn_b(qkv, bias, sinks):
    return pl.pallas_call(
        _attn_b_kernel,
        grid=(BATCH,),
        in_specs=[
            pl.BlockSpec(memory_space=pltpu.SMEM),
            pl.BlockSpec((SEQ, B_COLS), lambda b: (b, 0)),
            pl.BlockSpec((2, B_Q_HEADS, BLK, 2 * BLK), lambda b: (0, 0, 0, 0)),
        ],
        out_specs=pl.BlockSpec((SEQ, B_WIDTH), lambda b: (b, 0)),
        out_shape=jax.ShapeDtypeStruct((ROWS, B_WIDTH), BF16),
        compiler_params=_params("arbitrary"),
        name="attn_window",
    )(sinks.reshape(1, B_Q_HEADS), qkv, bias)


def _suffix_matrix():
    j = np.arange(2 * BLK)[:, None] % BLK
    s = np.arange(2 * BLK)[None, :]
    return ((s >= BLK) | (j > s)).astype(np.float32)


def _stick_blocks(qs, ks, vs, suffix, laters, strict_lower):
    zs = [_dot_nt(q, k) for q, k in zip(qs, ks)]
    log_betas, splits = [], []
    for z in zs:
        soft = jnp.log(1.0 + jnp.exp(-jnp.abs(z)))
        log_keep = -(jnp.maximum(z, 0.0) + soft)
        log_betas.append(z + log_keep)
        if laters is None:
            log_keep = jnp.where(strict_lower, log_keep, 0.0)
        hi = log_keep.astype(BF16)
        lo = (log_keep - hi.astype(F32)).astype(BF16)
        splits.append(jnp.concatenate([hi, lo], axis=1))
    sums = [_dot(s, suffix) for s in splits]
    ws, new_laters = [], []
    for n, (log_beta, s) in enumerate(zip(log_betas, sums)):
        if laters is None:
            w = jnp.where(strict_lower, jnp.exp(log_beta + s[:, :BLK]), 0.0)
            new_laters.append(s[:, BLK:])
        else:
            w = jnp.exp(log_beta + s[:, :BLK] + laters[n])
            new_laters.append(laters[n] + s[:, BLK:])
        ws.append(w.astype(BF16))
    return [_unstack_pair(_dot(w, v)) for w, v in zip(ws, vs)], new_laters


def _stick_schedule():
    q_blk, slot, k_blk, first_step = [], [], [], [0, 0]
    for dist in range(1, N_BLK):
        group = [(i, i, i - dist) for i in range(dist, N_BLK)]
        group += [(0, N_BLK, 0)] * (-len(group) % STICK_UNROLL)
        for qi, sl, kj in group:
            q_blk.append(qi)
            slot.append(sl)
            k_blk.append(kj)
        first_step.append(len(q_blk) // STICK_UNROLL)
    return tuple(np.asarray(t, np.int32) for t in (q_blk, slot, k_blk, first_step))


def _attn_c_kernel(qblk_ref, slot_ref, kblk_ref, step_ref, qkv_ref, suffix_ref, o_ref, later_s, acc_s):
    n_pairs = C_HEADS // 2
    t_idx = lax.broadcasted_iota(jnp.int32, (2 * BLK, BLK), 0) % BLK
    s_idx = lax.broadcasted_iota(jnp.int32, (2 * BLK, BLK), 1)
    strict_lower = s_idx < t_idx

    def cols(part, p):
        return slice(part * C_WIDTH + p * LANES, part * C_WIDTH + (p + 1) * LANES)

    def rows_of(blk):
        return pl.ds(pl.multiple_of(blk * BLK, BLK), BLK)

    for p in range(n_pairs):
        later_s[p, N_BLK] = jnp.zeros((2 * BLK, BLK), F32)
        acc_s[p, N_BLK] = jnp.zeros((BLK, LANES), F32)

    def diag(n, carry):
        where = [(p, n * STICK_UNROLL + u) for u in range(STICK_UNROLL) for p in range(n_pairs)]
        parts, laters = _stick_blocks(
            [_stack_pair(qkv_ref[rows_of(i), cols(0, p)]) for p, i in where],
            [qkv_ref[rows_of(i), cols(1, p)] for p, i in where],
            [qkv_ref[rows_of(i), cols(2, p)] for p, i in where],
            suffix_ref[...], None, strict_lower)
        for (p, i), part, later in zip(where, parts, laters):
            acc_s[p, i] = part
            later_s[p, i] = later
        return carry

    lax.fori_loop(0, N_BLK // STICK_UNROLL, diag, 0)

    def off_diag(n, carry):
        where = []
        for u in range(STICK_UNROLL):
            e = n * STICK_UNROLL + u
            q_rows, k_rows, sl = rows_of(qblk_ref[e]), rows_of(kblk_ref[e]), slot_ref[e]
            where += [(p, sl, q_rows, k_rows) for p in range(n_pairs)]
        accs = [acc_s[p, sl] for p, sl, _, _ in where]
        parts, laters = _stick_blocks(
            [_stack_pair(qkv_ref[q_rows, cols(0, p)]) for p, _, q_rows, _ in where],
            [qkv_ref[k_rows, cols(1, p)] for p, _, _, k_rows in where],
            [qkv_ref[k_rows, cols(2, p)] for p, _, _, k_rows in where],
            suffix_ref[...], [later_s[p, sl] for p, sl, _, _ in where], None)
        for (p, sl, _, _), acc, part, later in zip(where, accs, parts, laters):
            acc_s[p, sl] = acc + part
            later_s[p, sl] = later
        return carry

    def group(state):
        dist, _ = state
        lax.fori_loop(step_ref[dist], step_ref[dist + 1], off_diag, 0)

        def slot_max(i, mx):
            for p in range(n_pairs):
                mx = jnp.maximum(mx, later_s[p, i])
            return mx

        mx = lax.fori_loop(dist + 1, N_BLK, slot_max, jnp.full((2 * BLK, BLK), -jnp.inf, F32))
        return dist + 1, jnp.logical_not(jnp.max(mx) < STICK_DEAD)

    lax.while_loop(lambda state: (state[0] < N_BLK) & state[1], group, (jnp.int32(1), jnp.bool_(True)))

    def emit(i, carry):
        for p in range(n_pairs):
            o_ref[rows_of(i), p * LANES:(p + 1) * LANES] = acc_s[p, i].astype(BF16)
        return carry

    lax.fori_loop(0, N_BLK, emit, 0)


def _attn_c(qkv):
    suffix = jnp.asarray(_suffix_matrix(), BF16)
    schedule = [jnp.asarray(t) for t in _stick_schedule()]
    smem = pl.BlockSpec(memory_space=pltpu.SMEM)
    return pl.pallas_call(
        _attn_c_kernel,
        grid=(BATCH,),
        in_specs=[
            smem, smem, smem, smem,
            pl.BlockSpec((SEQ, QKV_W), lambda b: (b, 0)),
            pl.BlockSpec((2 * BLK, 2 * BLK), lambda b: (0, 0)),
        ],
        out_specs=pl.BlockSpec((SEQ, C_WIDTH), lambda b: (b, 0)),
        out_shape=jax.ShapeDtypeStruct((ROWS, C_WIDTH), BF16),
        scratch_shapes=[
            pltpu.VMEM((C_HEADS // 2, N_BLK + 1, 2 * BLK, BLK), F32),
            pltpu.VMEM((C_HEADS // 2, N_BLK + 1, BLK, LANES), F32),
        ],
        compiler_params=_params("arbitrary"),
        name="attn_stick",
    )(*schedule, qkv, suffix)


def _merge_kernel(x_ref, oa_ref, ob_ref, oc_ref, gpre_ref, win_ref, bg_ref, wa_ref, wb_ref, wc_ref,
                  wo_ref, gpost_ref, y_ref):
    for t in range(TILES_PER_STEP):
        rows = slice(t * TM_PROJ, (t + 1) * TM_PROJ)
        xv = x_ref[rows, :]
        h = _rms(xv, gpre_ref[...]).astype(BF16)
        slabs = []
        for c in range(D_MODEL // MERGE_COLS):
            merged = None
            for n, (o_ref, w_ref) in enumerate(((oa_ref, wa_ref), (ob_ref, wb_ref), (oc_ref, wc_ref))):
                cols = slice(n * D_MODEL + c * MERGE_COLS, n * D_MODEL + (c + 1) * MERGE_COLS)
                gate_w = win_ref[:, OFF_GATE + cols.start:OFF_GATE + cols.stop]
                gate = jax.nn.sigmoid(_dot(h, gate_w) + bg_ref[:, cols])
                term = gate * _dot(o_ref[rows, :], w_ref[:, c * MERGE_COLS:(c + 1) * MERGE_COLS])
                merged = term if merged is None else merged + term
            slabs.append(merged.astype(BF16))
        y = _dot(jnp.concatenate(slabs, axis=1), wo_ref[...])
        y_ref[rows, :] = xv + _rms(y, gpost_ref[...])


def _merge(layer, x2, o_a, o_b, o_c, gpre, w_in16, b_gate, w_a, w_b, w_c, w_o, gpost):
    def rows(width):
        return pl.BlockSpec((TILES_PER_STEP * TM_PROJ, width), lambda i: (i, 0))

    def whole(shape):
        return _layer_spec(layer, shape)

    return pl.pallas_call(
        _merge_kernel,
        grid=(ROWS // (TILES_PER_STEP * TM_PROJ),),
        in_specs=[
            rows(D_MODEL), rows(A_WIDTH), rows(B_WIDTH), rows(C_WIDTH),
            whole((1, D_MODEL)),
            pl.BlockSpec((None, D_MODEL, OFF_GATE + GATE_COLS), lambda i: (layer, 0, 0),
                         pipeline_mode=pl.Buffered(1)),
            whole((1, GATE_COLS)),
            whole((A_WIDTH, D_MODEL)), whole((B_WIDTH, D_MODEL)), whole((C_WIDTH, D_MODEL)),
            whole((D_MODEL, D_MODEL)), whole((1, D_MODEL)),
        ],
        out_specs=rows(D_MODEL),
        out_shape=jax.ShapeDtypeStruct((ROWS, D_MODEL), F32),
        compiler_params=_params("arbitrary"),
        name="merge_out_proj",
    )(x2, o_a, o_b, o_c, gpre, w_in16, b_gate, w_a, w_b, w_c, w_o, gpost)


def _gelu_tanh_doubled(x):
    k = math.sqrt(2.0 / math.pi)
    return x * (1.0 + jnp.tanh(x * (k + (k * 0.044715) * (x * x))))


def _ffn_kernel(x_ref, gpre_ref, wup_ref, cw_ref, cb_ref, wd_ref, gpost_ref, y_ref, xs_s, ys_s, tail_s):
    i = pl.program_id(0)
    n_chunks = D_FF // TN_FFN
    steps = TM_FFN // SUBLANES
    n_lane_tiles = D_MODEL // LANES

    def strided_rows(v):
        s, j0 = divmod(v * SUBLANES, steps)
        return pl.ds(j0 * SUBLANES + s, SUBLANES, stride=SUBLANES)

    for t in range(FFN_TILES):
        row0 = t * TM_FFN
        for lt in range(n_lane_tiles):
            for v in range(steps):
                xs_s[t, lt, strided_rows(v), :] = x_ref[row0 + v * SUBLANES:row0 + (v + 1) * SUBLANES,
                                                        lt * LANES:(lt + 1) * LANES]
        xv = jnp.concatenate([xs_s[t, lt] for lt in range(n_lane_tiles)], axis=1)
        h = _rms(xv, gpre_ref[...]).astype(BF16)
        seq_start = ((i * FFN_TILES + t) % (SEQ // TM_FFN)) == 0

        def up(c, h=h):
            return [_dot(h, wup_ref[:, half * D_FF + c * TN_FFN:half * D_FF + (c + 1) * TN_FFN])
                    for half in range(2)]

        def conv(u, c, half, seq_start=seq_start):
            cols = slice(half * D_FF + c * TN_FFN, half * D_FF + (c + 1) * TN_FFN)
            last2 = u[TM_FFN - 2 * SUBLANES:, :]
            prev2 = jnp.where(seq_start, 0.0, tail_s[2 * c + half])
            tail_s[2 * c + half] = last2
            wrapped = [jnp.concatenate([prev2[g * SUBLANES + SUBLANES - 1:(g + 1) * SUBLANES],
                                        last2[g * SUBLANES:(g + 1) * SUBLANES - 1]], axis=0)
                       for g in range(2)]
            back1 = jnp.concatenate([wrapped[1], u[:TM_FFN - SUBLANES]], axis=0)
            back2 = jnp.concatenate([wrapped[0], wrapped[1], u[:TM_FFN - 2 * SUBLANES]], axis=0)
            return (cw_ref[2:3, cols] * u + cw_ref[1:2, cols] * back1 + cw_ref[0:1, cols] * back2
                    + cb_ref[:, cols])

        acc = None
        ups = [up(c) for c in range(FFN_AHEAD)]
        for c in range(n_chunks):
            if c + FFN_AHEAD < n_chunks:
                ups.append(up(c + FFN_AHEAD))
            gate_u, val_u = ups.pop(0)
            act = (_gelu_tanh_doubled(conv(gate_u, c, 0)) * conv(val_u, c, 1)).astype(BF16)
            part = _dot(act, wd_ref[c * TN_FFN:(c + 1) * TN_FFN, :])
            acc = part if acc is None else acc + part
        y = xv + _rms(acc, gpost_ref[...])
        for lt in range(n_lane_tiles):
            ys_s[lt] = y[:, lt * LANES:(lt + 1) * LANES]
            for v in range(steps):
                y_ref[row0 + v * SUBLANES:row0 + (v + 1) * SUBLANES, lt * LANES:(lt + 1) * LANES] = (
                    ys_s[lt, strided_rows(v), :])


def _ffn(layer, x2, gpre, w_up, conv_w, conv_b, w_down, gpost):
    n_chunks = D_FF // TN_FFN

    def whole(shape):
        return pl.BlockSpec((None,) + shape, lambda i: (layer, 0, 0), pipeline_mode=pl.Buffered(1))

    return pl.pallas_call(
        _ffn_kernel,
        grid=(ROWS // (FFN_TILES * TM_FFN),),
        in_specs=[
            pl.BlockSpec((FFN_TILES * TM_FFN, D_MODEL), lambda i: (i, 0)),
            whole((1, D_MODEL)),
            whole((D_MODEL, 2 * D_FF)),
            whole((CONV_WIDTH, 2 * D_FF)),
            whole((1, 2 * D_FF)),
            whole((D_FF, D_MODEL)),
            whole((1, D_MODEL)),
        ],
        out_specs=pl.BlockSpec((FFN_TILES * TM_FFN, D_MODEL), lambda i: (i, 0)),
        out_shape=jax.ShapeDtypeStruct((ROWS, D_MODEL), F32),
        scratch_shapes=[
            pltpu.VMEM((FFN_TILES, D_MODEL // LANES, TM_FFN, LANES), F32),
            pltpu.VMEM((D_MODEL // LANES, TM_FFN, LANES), F32),
            pltpu.VMEM((2 * n_chunks, 2 * SUBLANES, TN_FFN), F32),
        ],
        compiler_params=_params("arbitrary"),
        name="conv_ffn",
    )(x2, gpre, w_up, conv_w, conv_b, w_down, gpost)


def _q_column_scale():
    scale = np.ones((OFF_GATE + GATE_COLS,), np.float32)
    scale[:N_A_GROUP_HEADS * HEAD_DIM] = SCALE
    scale[OFF_B_Q:OFF_B_KV] = SCALE
    scale[OFF_C:OFF_C + C_WIDTH] = SCALE
    return scale


def _doubled_kv(w_in16):
    rows = w_in16.shape[:-1]
    kv = w_in16[..., OFF_B_KV:OFF_C].reshape(rows + (2 * B_KV_HEADS, 1, HEAD_DIM))
    return jnp.broadcast_to(kv, rows + (2 * B_KV_HEADS, 2, HEAD_DIM)).reshape(rows + (-1,))


def kernel(x, rel_bias, attn_pre_norm, w_in, b_gate, sinks, w_br_a, w_br_b, w_br_c, w_out,
           attn_post_norm, ffn_pre_norm, w_up, conv_w, conv_b, w_down, ffn_post_norm):
    assert x.shape == (BATCH, SEQ, D_MODEL) and x.dtype == F32
    bias_a = _band_bias(rel_bias, 0, N_A_GROUP_HEADS, BLK, True)
    bias_b = _band_bias(rel_bias, N_A_GROUP_HEADS, B_Q_HEADS, B_WINDOW - 1, False)
    x2 = x.reshape(ROWS, D_MODEL)

    def row_param(p):
        return p.reshape(DEPTH, 1, -1)

    w_in16 = (w_in * _q_column_scale()).astype(BF16)
    w_kv = _doubled_kv(w_in16)
    w_a, w_b, w_c, w_o = (w.astype(BF16) for w in (w_br_a, w_br_b, w_br_c, w_out))
    w_up16, w_down16 = w_up.astype(BF16), (w_down * 0.5).astype(BF16)
    attn_pre, attn_post, ffn_pre, ffn_post, gate_b, conv_bias = (
        row_param(p) for p in (attn_pre_norm, attn_post_norm, ffn_pre_norm, ffn_post_norm, b_gate, conv_b))
    for layer in range(DEPTH):
        a0, a1, a2, bq, cq = _in_proj(layer, x2, attn_pre, w_in16, w_kv)
        o_a = _attn_a(a0, a1, a2, bias_a)
        o_b = _attn_b(bq, bias_b, sinks[layer])
        o_c = _attn_c(cq)
        x2 = _merge(layer, x2, o_a, o_b, o_c, attn_pre, w_in16, gate_b, w_a, w_b, w_c, w_o, attn_post)
        x2 = _ffn(layer, x2, ffn_pre, w_up16, conv_w, conv_bias, w_down16, ffn_post)
    return x2.reshape(BATCH, SEQ, D_MODEL)
```

```python
import functools
import math

import numpy as np
import jax
import jax.numpy as jnp
from jax import lax
from jax.experimental import pallas as pl
from jax.experimental.pallas import tpu as pltpu

D_MODEL = 1024
BATCH = 8
SEQ = 2048
DEPTH = 2
HEAD_DIM = 64
BLK = 128
A_GROUPS = ((128, 1), (512, 4), (2048, 16))
A_HEADS = 4
B_Q_HEADS = 8
B_KV_HEADS = 2
B_WINDOW = 128
C_HEADS = 4
N_BRANCH = 3
NUM_BUCKETS = 32
MAX_DISTANCE = 2048
D_FF = 4 * D_MODEL
CONV_WIDTH = 3
EPS = 1e-6
SCALE = HEAD_DIM ** -0.5

A_WIDTH = A_HEADS * HEAD_DIM
B_WIDTH = B_Q_HEADS * HEAD_DIM
C_WIDTH = C_HEADS * HEAD_DIM
B_GROUP = B_Q_HEADS // B_KV_HEADS
N_A_GROUP_HEADS = len(A_GROUPS) * A_HEADS
N_BIAS_HEADS = N_A_GROUP_HEADS + B_Q_HEADS
A_QKV_COLS = 3 * N_A_GROUP_HEADS * HEAD_DIM
OFF_B_Q = A_QKV_COLS
OFF_B_KV = OFF_B_Q + B_WIDTH
OFF_C = OFF_B_KV + 2 * B_KV_HEADS * HEAD_DIM
OFF_GATE = OFF_C + 3 * C_WIDTH
GATE_COLS = N_BRANCH * D_MODEL

ROWS = BATCH * SEQ
N_BLK = SEQ // BLK
QKV_W = 3 * A_WIDTH
LANES = 128
B_COLS = B_WIDTH + 2 * B_KV_HEADS * LANES
NEG = -1e30
STICK_DEAD = -104.0

V7X_VMEM_BYTES = 64 * 1024 * 1024
VMEM_LIMIT = V7X_VMEM_BYTES * 7 // 8

TM_PROJ = 512
TILES_PER_STEP = 2
MERGE_COLS = 256
TM_FFN = 512
TN_FFN = 512
FFN_AHEAD = 3
FFN_TILES = 1
SUBLANES = 8
BAND_UNROLL = 4
STICK_UNROLL = 4

BF16 = jnp.bfloat16
F32 = jnp.float32


def _params(*sem):
    return pltpu.CompilerParams(dimension_semantics=sem, vmem_limit_bytes=VMEM_LIMIT)


def _dot(a, b):
    return jnp.dot(a, b, preferred_element_type=F32)


def _dot_nt(a, b):
    return lax.dot_general(a, b, (((1,), (1,)), ((), ())), preferred_element_type=F32)


def _rms(xv, gain):
    y = xv * lax.rsqrt(jnp.mean(xv * xv, axis=-1, keepdims=True) + EPS)
    return y * gain


def _bucket_constants():
    a = np.arange(BLK)[:, None]
    b = np.arange(2 * BLK)[None, :]
    dist = np.maximum(a + BLK - b, 0)
    max_exact = NUM_BUCKETS // 2
    out = []
    for _, d in A_GROUPS:
        n = dist * d
        nf = np.maximum(n, 1).astype(np.float64)
        large = max_exact + (np.log(nf / max_exact) / math.log(MAX_DISTANCE / max_exact)
                             * (NUM_BUCKETS - max_exact)).astype(np.int64)
        large = np.minimum(large, NUM_BUCKETS - 1)
        out.append(np.where(n < max_exact, n, large))
    return np.stack(out).astype(np.int32)


def _bias_kernel(tab_ref, bucket_ref, out_ref, *, head0, max_dist):
    step = pl.program_id(0)
    a = lax.broadcasted_iota(jnp.int32, (BLK, 2 * BLK), 0)
    b = lax.broadcasted_iota(jnp.int32, (BLK, 2 * BLK), 1)
    dist = a + BLK - b
    valid = (dist >= 0) & (dist <= max_dist)
    bk = bucket_ref[...]
    for n in range(A_HEADS):
        acc = jnp.zeros((BLK, 2 * BLK), F32)
        for k in range(NUM_BUCKETS):
            acc = jnp.where(bk == k, tab_ref[k, head0 + step * A_HEADS + n], acc)
        out_ref[0, n] = jnp.where(valid, acc, NEG)
        out_ref[1, n] = jnp.where(valid & (b >= BLK), acc, NEG)


def _band_bias(rel_bias, head0, n_heads, max_dist, dilated):
    buckets = jnp.asarray(_bucket_constants())
    return pl.pallas_call(
        functools.partial(_bias_kernel, head0=head0, max_dist=max_dist),
        grid=(n_heads // A_HEADS,),
        in_specs=[
            pl.BlockSpec(memory_space=pltpu.SMEM),
            pl.BlockSpec((None, BLK, 2 * BLK), lambda s: (s if dilated else 0, 0, 0)),
        ],
        out_specs=pl.BlockSpec((2, A_HEADS, BLK, 2 * BLK), lambda s: (0, s, 0, 0)),
        out_shape=jax.ShapeDtypeStruct((2, n_heads, BLK, 2 * BLK), F32),
        compiler_params=_params("arbitrary"),
        name="band_bias",
    )(rel_bias, buckets)


def _in_proj_kernel(x_ref, g_ref, w_ref, wkv_ref, a0_ref, a1_ref, a2_ref, b_ref, c_ref, ys_s):
    n_lane_tiles = D_MODEL // LANES

    def proj(h, w):
        return _dot(h, w).astype(BF16)

    def a_cols(part, g):
        start = (part * len(A_GROUPS) + g) * A_WIDTH
        return slice(start, start + A_WIDTH)

    for t in range(TILES_PER_STEP):
        rows = slice(t * TM_PROJ, (t + 1) * TM_PROJ)
        y = _rms(x_ref[rows, :], g_ref[...])
        h = y.astype(BF16)
        for part in range(3):
            a0_ref[rows, part * A_WIDTH:(part + 1) * A_WIDTH] = proj(h, w_ref[:, a_cols(part, 0)])
        b_ref[rows, :B_WIDTH] = proj(h, w_ref[:, OFF_B_Q:OFF_B_KV])
        b_ref[rows, B_WIDTH:] = proj(h, wkv_ref[...])
        c_ref[rows, :] = proj(h, w_ref[:, OFF_C:OFF_GATE])
        for c in range(n_lane_tiles):
            ys_s[t, c] = y[:, c * LANES:(c + 1) * LANES]
        for g, ref in ((1, a1_ref), (2, a2_ref)):
            d = A_GROUPS[g][1]
            per = TM_PROJ // d
            regrouped = jnp.concatenate(
                [jnp.concatenate([ys_s[t, c, pl.ds(r, per, stride=d), :] for c in range(n_lane_tiles)],
                                 axis=1) for r in range(d)], axis=0).astype(BF16)
            for part in range(3):
                ref[:, t * per:(t + 1) * per, part * A_WIDTH:(part + 1) * A_WIDTH] = proj(
                    regrouped, w_ref[:, a_cols(part, g)]).reshape(d, per, A_WIDTH)


def _layer_spec(layer, shape):
    return pl.BlockSpec((None,) + shape, lambda i: (layer, 0, 0))


def _in_proj(layer, x2, gain, w_in16, w_kv):
    step_rows = TILES_PER_STEP * TM_PROJ
    tiles_per_seq = SEQ // step_rows
    d1, d2 = A_GROUPS[1][1], A_GROUPS[2][1]

    def nat(width):
        return pl.BlockSpec((step_rows, width), lambda i: (i, 0))

    return pl.pallas_call(
        _in_proj_kernel,
        grid=(ROWS // step_rows,),
        in_specs=[
            pl.BlockSpec((step_rows, D_MODEL), lambda i: (i, 0)),
            _layer_spec(layer, (1, D_MODEL)),
            pl.BlockSpec((None, D_MODEL, OFF_GATE + GATE_COLS), lambda i: (layer, 0, 0),
                         pipeline_mode=pl.Buffered(1)),
            _layer_spec(layer, (D_MODEL, B_COLS - B_WIDTH)),
        ],
        out_specs=[
            nat(QKV_W),
            pl.BlockSpec((None, d1, step_rows // d1, QKV_W),
                         lambda i: (i // tiles_per_seq, 0, i % tiles_per_seq, 0)),
            pl.BlockSpec((None, d2, step_rows // d2, QKV_W),
                         lambda i: (i // tiles_per_seq, 0, i % tiles_per_seq, 0)),
            nat(B_COLS),
            nat(QKV_W),
        ],
        out_shape=[
            jax.ShapeDtypeStruct((ROWS, QKV_W), BF16),
            jax.ShapeDtypeStruct((BATCH, d1, SEQ // d1, QKV_W), BF16),
            jax.ShapeDtypeStruct((BATCH, d2, SEQ // d2, QKV_W), BF16),
            jax.ShapeDtypeStruct((ROWS, B_COLS), BF16),
            jax.ShapeDtypeStruct((ROWS, QKV_W), BF16),
        ],
        scratch_shapes=[pltpu.VMEM((TILES_PER_STEP, D_MODEL // LANES, TM_PROJ, LANES), F32)],
        compiler_params=_params("arbitrary"),
        name="in_proj",
    )(x2, gain, w_in16, w_kv)


def _low_lanes(rows):
    return lax.broadcasted_iota(jnp.int32, (rows, LANES), 1) < HEAD_DIM


def _stack_pair(q2):
    low = _low_lanes(BLK)
    zero = jnp.zeros_like(q2)
    return jnp.concatenate([jnp.where(low, q2, zero), jnp.where(low, zero, q2)], axis=0)


def _unstack_pair(o):
    return jnp.where(_low_lanes(BLK), o[:BLK], o[BLK:])


def _band_blocks(qs, kps, kcs, vps, vcs, biases):
    ss = [_dot_nt(q, jnp.concatenate([kp, kc], axis=0)) for q, kp, kc in zip(qs, kps, kcs)]
    ps, stats = [], []
    for s, bias in zip(ss, biases):
        s = s + bias
        m = jnp.max(s, axis=-1, keepdims=True)
        p = jnp.exp(s - m)
        stats.append((m, jnp.sum(p, axis=-1, keepdims=True)))
        ps.append(p.astype(BF16))
    nums = [_dot(p, jnp.concatenate([vp, vc], axis=0)) for p, vp, vc in zip(ps, vps, vcs)]
    return nums, stats


def _attn_a_kernel(a0_ref, a1_ref, a2_ref, bias_ref, o_ref, out_s, lse_s, sum_s):
    n_pairs = A_HEADS // 2

    def cols(part, p):
        return slice(part * A_WIDTH + p * LANES, part * A_WIDTH + (p + 1) * LANES)

    for g, (ref, (_, d)) in enumerate(zip((a0_ref, a1_ref, a2_ref), A_GROUPS)):
        nb = N_BLK // d

        def step(n, carry, g=g, ref=ref, d=d, nb=nb):
            units, qs, kps, kcs, vps, vcs, biases = [], [], [], [], [], [], []
            for u in range(BAND_UNROLL):
                blk = n * BAND_UNROLL + u
                r = blk // nb
                i = blk % nb
                row_c = pl.multiple_of(i * BLK, BLK)
                row_p = pl.multiple_of(jnp.maximum(i - 1, 0) * BLK, BLK)
                first = jnp.where(i > 0, 0, 1)
                if d == 1:
                    rows = pl.ds(row_c, BLK)
                else:
                    rows = pl.ds(i * (BLK * d) + r, BLK, stride=d)
                for p in range(n_pairs):
                    units.append((p, rows))
                    heads = slice(g * A_HEADS + 2 * p, g * A_HEADS + 2 * p + 2)
                    qs.append(_stack_pair(ref[r, pl.ds(row_c, BLK), cols(0, p)]))
                    kps.append(ref[r, pl.ds(row_p, BLK), cols(1, p)])
                    kcs.append(ref[r, pl.ds(row_c, BLK), cols(1, p)])
                    vps.append(ref[r, pl.ds(row_p, BLK), cols(2, p)])
                    vcs.append(ref[r, pl.ds(row_c, BLK), cols(2, p)])
                    biases.append(bias_ref[first, heads].reshape(2 * BLK, 2 * BLK))
            nums, stats = _band_blocks(qs, kps, kcs, vps, vcs, biases)
            for (p, rows), num, (m, l) in zip(units, nums, stats):
                out_s[g, p, rows, :] = _unstack_pair(num)
                lse_s[g, p, rows, :] = _unstack_pair(jnp.broadcast_to(m, (2 * BLK, LANES)))
                sum_s[g, p, rows, :] = _unstack_pair(jnp.broadcast_to(l, (2 * BLK, LANES)))
            return carry

        lax.fori_loop(0, N_BLK // BAND_UNROLL, step, 0)

    def combine(t, carry):
        rows = pl.ds(pl.multiple_of(t * BLK, BLK), BLK)
        for p in range(n_pairs):
            m0, m1, m2 = lse_s[0, p, rows, :], lse_s[1, p, rows, :], lse_s[2, p, rows, :]
            mx = jnp.maximum(jnp.maximum(m0, m1), m2)
            c0, c1, c2 = jnp.exp(m0 - mx), jnp.exp(m1 - mx), jnp.exp(m2 - mx)
            top = c0 * out_s[0, p, rows, :] + c1 * out_s[1, p, rows, :] + c2 * out_s[2, p, rows, :]
            bottom = c0 * sum_s[0, p, rows, :] + c1 * sum_s[1, p, rows, :] + c2 * sum_s[2, p, rows, :]
            o_ref[rows, p * LANES:(p + 1) * LANES] = (top / bottom).astype(BF16)
        return carry

    lax.fori_loop(0, N_BLK, combine, 0)


def _attn_a(a0, a1, a2, bias):
    d1, d2 = A_GROUPS[1][1], A_GROUPS[2][1]
    return pl.pallas_call(
        _attn_a_kernel,
        grid=(BATCH,),
        in_specs=[
            pl.BlockSpec((None, 1, SEQ, QKV_W), lambda b: (b, 0, 0, 0)),
            pl.BlockSpec((None, d1, SEQ // d1, QKV_W), lambda b: (b, 0, 0, 0)),
            pl.BlockSpec((None, d2, SEQ // d2, QKV_W), lambda b: (b, 0, 0, 0)),
            pl.BlockSpec((2, N_A_GROUP_HEADS, BLK, 2 * BLK), lambda b: (0, 0, 0, 0)),
        ],
        out_specs=pl.BlockSpec((SEQ, A_WIDTH), lambda b: (b, 0)),
        out_shape=jax.ShapeDtypeStruct((ROWS, A_WIDTH), BF16),
        scratch_shapes=[pltpu.VMEM((len(A_GROUPS), A_HEADS // 2, SEQ, LANES), F32)] * 3,
        compiler_params=_params("arbitrary"),
        name="attn_dilated",
    )(a0.reshape(BATCH, 1, SEQ, QKV_W), a1, a2, bias)


def _attn_b_kernel(sink_ref, qkv_ref, bias_ref, o_ref):
    def kv_cols(part, kvh):
        start = B_WIDTH + (part * B_KV_HEADS + kvh) * LANES
        return slice(start, start + LANES)

    def step(n, carry):
        blocks, qs, kps, kcs, vps, vcs, biases = [], [], [], [], [], [], []
        for u in range(BAND_UNROLL):
            i = n * BAND_UNROLL + u
            row_c = pl.multiple_of(i * BLK, BLK)
            row_p = pl.multiple_of(jnp.maximum(i - 1, 0) * BLK, BLK)
            first = jnp.where(i > 0, 0, 1)
            blocks.append(row_c)
            for kvh in range(B_KV_HEADS):
                heads = slice(kvh * B_GROUP, (kvh + 1) * B_GROUP)
                tiles = range(kvh * B_GROUP // 2, (kvh + 1) * B_GROUP // 2)
                qs.append(jnp.concatenate(
                    [_stack_pair(qkv_ref[pl.ds(row_c, BLK), t * LANES:(t + 1) * LANES]) for t in tiles],
                    axis=0))
                kps.append(qkv_ref[pl.ds(row_p, BLK), kv_cols(0, kvh)])
                kcs.append(qkv_ref[pl.ds(row_c, BLK), kv_cols(0, kvh)])
                vps.append(qkv_ref[pl.ds(row_p, BLK), kv_cols(1, kvh)])
                vcs.append(qkv_ref[pl.ds(row_c, BLK), kv_cols(1, kvh)])
                biases.append(bias_ref[first, heads].reshape(B_GROUP * BLK, 2 * BLK))
        nums, stats = _band_blocks(qs, kps, kcs, vps, vcs, biases)
        for u, row_c in enumerate(blocks):
            for kvh in range(B_KV_HEADS):
                num = nums[u * B_KV_HEADS + kvh]
                m, l = stats[u * B_KV_HEADS + kvh]
                scaled = []
                for gq in range(B_GROUP):
                    rows = slice(gq * BLK, (gq + 1) * BLK)
                    sink = sink_ref[0, kvh * B_GROUP + gq]
                    scaled.append(num[rows] / (l[rows] + jnp.exp(sink - m[rows])))
                for t in range(B_GROUP // 2):
                    tile = kvh * B_GROUP // 2 + t
                    o_ref[pl.ds(row_c, BLK), tile * LANES:(tile + 1) * LANES] = _unstack_pair(
                        jnp.concatenate(scaled[2 * t:2 * t + 2], axis=0)).astype(BF16)
        return carry

    lax.fori_loop(0, N_BLK // BAND_UNROLL, step, 0)


def _attn_b(qkv, bias, sinks):
    return pl.pallas_call(
        _attn_b_kernel,
        grid=(BATCH,),
        in_specs=[
            pl.BlockSpec(memory_space=pltpu.SMEM),
            pl.BlockSpec((SEQ, B_COLS), lambda b: (b, 0)),
            pl.BlockSpec((2, B_Q_HEADS, BLK, 2 * BLK), lambda b: (0, 0, 0, 0)),
        ],
        out_specs=pl.BlockSpec((SEQ, B_WIDTH), lambda b: (b, 0)),
        out_shape=jax.ShapeDtypeStruct((ROWS, B_WIDTH), BF16),
        compiler_params=_params("arbitrary"),
        name="attn_window",
    )(sinks.reshape(1, B_Q_HEADS), qkv, bias)


def _suffix_matrix():
    j = np.arange(2 * BLK)[:, None] % BLK
    s = np.arange(2 * BLK)[None, :]
    return ((s >= BLK) | (j > s)).astype(np.float32)


def _stick_blocks(qs, ks, vs, suffix, laters, strict_lower):
    zs = [_dot_nt(q, k) for q, k in zip(qs, ks)]
    log_betas, splits = [], []
    for z in zs:
        soft = jnp.log(1.0 + jnp.exp(-jnp.abs(z)))
        log_keep = -(jnp.maximum(z, 0.0) + soft)
        log_betas.append(z + log_keep)
        if laters is None:
            log_keep = jnp.where(strict_lower, log_keep, 0.0)
        hi = log_keep.astype(BF16)
        lo = (log_keep - hi.astype(F32)).astype(BF16)
        splits.append(jnp.concatenate([hi, lo], axis=1))
    sums = [_dot(s, suffix) for s in splits]
    ws, new_laters = [], []
    for n, (log_beta, s) in enumerate(zip(log_betas, sums)):
        if laters is None:
            w = jnp.where(strict_lower, jnp.exp(log_beta + s[:, :BLK]), 0.0)
            new_laters.append(s[:, BLK:])
        else:
            w = jnp.exp(log_beta + s[:, :BLK] + laters[n])
            new_laters.append(laters[n] + s[:, BLK:])
        ws.append(w.astype(BF16))
    return [_unstack_pair(_dot(w, v)) for w, v in zip(ws, vs)], new_laters


def _stick_schedule():
    q_blk, slot, k_blk, first_step = [], [], [], [0, 0]
    for dist in range(1, N_BLK):
        group = [(i, i, i - dist) for i in range(dist, N_BLK)]
        group += [(0, N_BLK, 0)] * (-len(group) % STICK_UNROLL)
        for qi, sl, kj in group:
            q_blk.append(qi)
            slot.append(sl)
            k_blk.append(kj)
        first_step.append(len(q_blk) // STICK_UNROLL)
    return tuple(np.asarray(t, np.int32) for t in (q_blk, slot, k_blk, first_step))


def _attn_c_kernel(qblk_ref, slot_ref, kblk_ref, step_ref, qkv_ref, suffix_ref, o_ref, later_s, acc_s):
    n_pairs = C_HEADS // 2
    t_idx = lax.broadcasted_iota(jnp.int32, (2 * BLK, BLK), 0) % BLK
    s_idx = lax.broadcasted_iota(jnp.int32, (2 * BLK, BLK), 1)
    strict_lower = s_idx < t_idx

    def cols(part, p):
        return slice(part * C_WIDTH + p * LANES, part * C_WIDTH + (p + 1) * LANES)

    def rows_of(blk):
        return pl.ds(pl.multiple_of(blk * BLK, BLK), BLK)

    for p in range(n_pairs):
        later_s[p, N_BLK] = jnp.zeros((2 * BLK, BLK), F32)
        acc_s[p, N_BLK] = jnp.zeros((BLK, LANES), F32)

    def diag(n, carry):
        where = [(p, n * STICK_UNROLL + u) for u in range(STICK_UNROLL) for p in range(n_pairs)]
        parts, laters = _stick_blocks(
            [_stack_pair(qkv_ref[rows_of(i), cols(0, p)]) for p, i in where],
            [qkv_ref[rows_of(i), cols(1, p)] for p, i in where],
            [qkv_ref[rows_of(i), cols(2, p)] for p, i in where],
            suffix_ref[...], None, strict_lower)
        for (p, i), part, later in zip(where, parts, laters):
            acc_s[p, i] = part
            later_s[p, i] = later
        return carry

    lax.fori_loop(0, N_BLK // STICK_UNROLL, diag, 0)

    def off_diag(n, carry):
        where = []
        for u in range(STICK_UNROLL):
            e = n * STICK_UNROLL + u
            q_rows, k_rows, sl = rows_of(qblk_ref[e]), rows_of(kblk_ref[e]), slot_ref[e]
            where += [(p, sl, q_rows, k_rows) for p in range(n_pairs)]
        accs = [acc_s[p, sl] for p, sl, _, _ in where]
        parts, laters = _stick_blocks(
            [_stack_pair(qkv_ref[q_rows, cols(0, p)]) for p, _, q_rows, _ in where],
            [qkv_ref[k_rows, cols(1, p)] for p, _, _, k_rows in where],
            [qkv_ref[k_rows, cols(2, p)] for p, _, _, k_rows in where],
            suffix_ref[...], [later_s[p, sl] for p, sl, _, _ in where], None)
        for (p, sl, _, _), acc, part, later in zip(where, accs, parts, laters):
            acc_s[p, sl] = acc + part
            later_s[p, sl] = later
        return carry

    def group(state):
        dist, _ = state
        lax.fori_loop(step_ref[dist], step_ref[dist + 1], off_diag, 0)

        def slot_max(i, mx):
            for p in range(n_pairs):
                mx = jnp.maximum(mx, later_s[p, i])
            return mx

        mx = lax.fori_loop(dist + 1, N_BLK, slot_max, jnp.full((2 * BLK, BLK), -jnp.inf, F32))
        return dist + 1, jnp.logical_not(jnp.max(mx) < STICK_DEAD)

    lax.while_loop(lambda state: (state[0] < N_BLK) & state[1], group, (jnp.int32(1), jnp.bool_(True)))

    def emit(i, carry):
        for p in range(n_pairs):
            o_ref[rows_of(i), p * LANES:(p + 1) * LANES] = acc_s[p, i].astype(BF16)
        return carry

    lax.fori_loop(0, N_BLK, emit, 0)


def _attn_c(qkv):
    suffix = jnp.asarray(_suffix_matrix(), BF16)
    schedule = [jnp.asarray(t) for t in _stick_schedule()]
    smem = pl.BlockSpec(memory_space=pltpu.SMEM)
    return pl.pallas_call(
        _attn_c_kernel,
        grid=(BATCH,),
        in_specs=[
            smem, smem, smem, smem,
            pl.BlockSpec((SEQ, QKV_W), lambda b: (b, 0)),
            pl.BlockSpec((2 * BLK, 2 * BLK), lambda b: (0, 0)),
        ],
        out_specs=pl.BlockSpec((SEQ, C_WIDTH), lambda b: (b, 0)),
        out_shape=jax.ShapeDtypeStruct((ROWS, C_WIDTH), BF16),
        scratch_shapes=[
            pltpu.VMEM((C_HEADS // 2, N_BLK + 1, 2 * BLK, BLK), F32),
            pltpu.VMEM((C_HEADS // 2, N_BLK + 1, BLK, LANES), F32),
        ],
        compiler_params=_params("arbitrary"),
        name="attn_stick",
    )(*schedule, qkv, suffix)


def _merge_kernel(x_ref, oa_ref, ob_ref, oc_ref, gpre_ref, win_ref, bg_ref, wa_ref, wb_ref, wc_ref,
                  wo_ref, gpost_ref, y_ref):
    for t in range(TILES_PER_STEP):
        rows = slice(t * TM_PROJ, (t + 1) * TM_PROJ)
        xv = x_ref[rows, :]
        h = _rms(xv, gpre_ref[...]).astype(BF16)
        slabs = []
        for c in range(D_MODEL // MERGE_COLS):
            merged = None
            for n, (o_ref, w_ref) in enumerate(((oa_ref, wa_ref), (ob_ref, wb_ref), (oc_ref, wc_ref))):
                cols = slice(n * D_MODEL + c * MERGE_COLS, n * D_MODEL + (c + 1) * MERGE_COLS)
                gate_w = win_ref[:, OFF_GATE + cols.start:OFF_GATE + cols.stop]
                gate = jax.nn.sigmoid(_dot(h, gate_w) + bg_ref[:, cols])
                term = gate * _dot(o_ref[rows, :], w_ref[:, c * MERGE_COLS:(c + 1) * MERGE_COLS])
                merged = term if merged is None else merged + term
            slabs.append(merged.astype(BF16))
        y = _dot(jnp.concatenate(slabs, axis=1), wo_ref[...])
        y_ref[rows, :] = xv + _rms(y, gpost_ref[...])


def _merge(layer, x2, o_a, o_b, o_c, gpre, w_in16, b_gate, w_a, w_b, w_c, w_o, gpost):
    def rows(width):
        return pl.BlockSpec((TILES_PER_STEP * TM_PROJ, width), lambda i: (i, 0))

    def whole(shape):
        return _layer_spec(layer, shape)

    return pl.pallas_call(
        _merge_kernel,
        grid=(ROWS // (TILES_PER_STEP * TM_PROJ),),
        in_specs=[
            rows(D_MODEL), rows(A_WIDTH), rows(B_WIDTH), rows(C_WIDTH),
            whole((1, D_MODEL)),
            pl.BlockSpec((None, D_MODEL, OFF_GATE + GATE_COLS), lambda i: (layer, 0, 0),
                         pipeline_mode=pl.Buffered(1)),
            whole((1, GATE_COLS)),
            whole((A_WIDTH, D_MODEL)), whole((B_WIDTH, D_MODEL)), whole((C_WIDTH, D_MODEL)),
            whole((D_MODEL, D_MODEL)), whole((1, D_MODEL)),
        ],
        out_specs=rows(D_MODEL),
        out_shape=jax.ShapeDtypeStruct((ROWS, D_MODEL), F32),
        compiler_params=_params("arbitrary"),
        name="merge_out_proj",
    )(x2, o_a, o_b, o_c, gpre, w_in16, b_gate, w_a, w_b, w_c, w_o, gpost)


def _gelu_tanh_doubled(x):
    k = math.sqrt(2.0 / math.pi)
    return x * (1.0 + jnp.tanh(x * (k + (k * 0.044715) * (x * x))))


def _ffn_kernel(x_ref, gpre_ref, wup_ref, cw_ref, cb_ref, wd_ref, gpost_ref, y_ref, xs_s, ys_s, tail_s):
    i = pl.program_id(0)
    n_chunks = D_FF // TN_FFN
    steps = TM_FFN // SUBLANES
    n_lane_tiles = D_MODEL // LANES

    def strided_rows(v):
        s, j0 = divmod(v * SUBLANES, steps)
        return pl.ds(j0 * SUBLANES + s, SUBLANES, stride=SUBLANES)

    for t in range(FFN_TILES):
        row0 = t * TM_FFN
        for lt in range(n_lane_tiles):
            for v in range(steps):
                xs_s[t, lt, strided_rows(v), :] = x_ref[row0 + v * SUBLANES:row0 + (v + 1) * SUBLANES,
                                                        lt * LANES:(lt + 1) * LANES]
        xv = jnp.concatenate([xs_s[t, lt] for lt in range(n_lane_tiles)], axis=1)
        h = _rms(xv, gpre_ref[...]).astype(BF16)
        seq_start = ((i * FFN_TILES + t) % (SEQ // TM_FFN)) == 0

        def up(c, h=h):
            return [_dot(h, wup_ref[:, half * D_FF + c * TN_FFN:half * D_FF + (c + 1) * TN_FFN])
                    for half in range(2)]

        def conv(u, c, half, seq_start=seq_start):
            cols = slice(half * D_FF + c * TN_FFN, half * D_FF + (c + 1) * TN_FFN)
            last2 = u[TM_FFN - 2 * SUBLANES:, :]
            prev2 = jnp.where(seq_start, 0.0, tail_s[2 * c + half])
            tail_s[2 * c + half] = last2
            wrapped = [jnp.concatenate([prev2[g * SUBLANES + SUBLANES - 1:(g + 1) * SUBLANES],
                                        last2[g * SUBLANES:(g + 1) * SUBLANES - 1]], axis=0)
                       for g in range(2)]
            back1 = jnp.concatenate([wrapped[1], u[:TM_FFN - SUBLANES]], axis=0)
            back2 = jnp.concatenate([wrapped[0], wrapped[1], u[:TM_FFN - 2 * SUBLANES]], axis=0)
            return (cw_ref[2:3, cols] * u + cw_ref[1:2, cols] * back1 + cw_ref[0:1, cols] * back2
                    + cb_ref[:, cols])

        acc = None
        ups = [up(c) for c in range(FFN_AHEAD)]
        for c in range(n_chunks):
            if c + FFN_AHEAD < n_chunks:
                ups.append(up(c + FFN_AHEAD))
            gate_u, val_u = ups.pop(0)
            act = (_gelu_tanh_doubled(conv(gate_u, c, 0)) * conv(val_u, c, 1)).astype(BF16)
            part = _dot(act, wd_ref[c * TN_FFN:(c + 1) * TN_FFN, :])
            acc = part if acc is None else acc + part
        y = xv + _rms(acc, gpost_ref[...])
        for lt in range(n_lane_tiles):
            ys_s[lt] = y[:, lt * LANES:(lt + 1) * LANES]
            for v in range(steps):
                y_ref[row0 + v * SUBLANES:row0 + (v + 1) * SUBLANES, lt * LANES:(lt + 1) * LANES] = (
                    ys_s[lt, strided_rows(v), :])


def _ffn(layer, x2, gpre, w_up, conv_w, conv_b, w_down, gpost):
    n_chunks = D_FF // TN_FFN

    def whole(shape):
        return pl.BlockSpec((None,) + shape, lambda i: (layer, 0, 0), pipeline_mode=pl.Buffered(1))

    return pl.pallas_call(
        _ffn_kernel,
        grid=(ROWS // (FFN_TILES * TM_FFN),),
        in_specs=[
            pl.BlockSpec((FFN_TILES * TM_FFN, D_MODEL), lambda i: (i, 0)),
            whole((1, D_MODEL)),
            whole((D_MODEL, 2 * D_FF)),
            whole((CONV_WIDTH, 2 * D_FF)),
            whole((1, 2 * D_FF)),
            whole((D_FF, D_MODEL)),
            whole((1, D_MODEL)),
        ],
        out_specs=pl.BlockSpec((FFN_TILES * TM_FFN, D_MODEL), lambda i: (i, 0)),
        out_shape=jax.ShapeDtypeStruct((ROWS, D_MODEL), F32),
        scratch_shapes=[
            pltpu.VMEM((FFN_TILES, D_MODEL // LANES, TM_FFN, LANES), F32),
            pltpu.VMEM((D_MODEL // LANES, TM_FFN, LANES), F32),
            pltpu.VMEM((2 * n_chunks, 2 * SUBLANES, TN_FFN), F32),
        ],
        compiler_params=_params("arbitrary"),
        name="conv_ffn",
    )(x2, gpre, w_up, conv_w, conv_b, w_down, gpost)


def _q_column_scale():
    scale = np.ones((OFF_GATE + GATE_COLS,), np.float32)
    scale[:N_A_GROUP_HEADS * HEAD_DIM] = SCALE
    scale[OFF_B_Q:OFF_B_KV] = SCALE
    scale[OFF_C:OFF_C + C_WIDTH] = SCALE
    return scale


def _doubled_kv(w_in16):
    rows = w_in16.shape[:-1]
    kv = w_in16[..., OFF_B_KV:OFF_C].reshape(rows + (2 * B_KV_HEADS, 1, HEAD_DIM))
    return jnp.broadcast_to(kv, rows + (2 * B_KV_HEADS, 2, HEAD_DIM)).reshape(rows + (-1,))


def kernel(x, rel_bias, attn_pre_norm, w_in, b_gate, sinks, w_br_a, w_br_b, w_br_c, w_out,
           attn_post_norm, ffn_pre_norm, w_up, conv_w, conv_b, w_down, ffn_post_norm):
    assert x.shape == (BATCH, SEQ, D_MODEL) and x.dtype == F32
    bias_a = _band_bias(rel_bias, 0, N_A_GROUP_HEADS, BLK, True)
    bias_b = _band_bias(rel_bias, N_A_GROUP_HEADS, B_Q_HEADS, B_WINDOW - 1, False)
    x2 = x.reshape(ROWS, D_MODEL)

    def row_param(p):
        return p.reshape(DEPTH, 1, -1)

    w_in16 = (w_in * _q_column_scale()).astype(BF16)
    w_kv = _doubled_kv(w_in16)
    w_a, w_b, w_c, w_o = (w.astype(BF16) for w in (w_br_a, w_br_b, w_br_c, w_out))
    w_up16, w_down16 = w_up.astype(BF16), (w_down * 0.5).astype(BF16)
    attn_pre, attn_post, ffn_pre, ffn_post, gate_b, conv_bias = (
        row_param(p) for p in (attn_pre_norm, attn_post_norm, ffn_pre_norm, ffn_post_norm, b_gate, conv_b))
    for layer in range(DEPTH):
        a0, a1, a2, bq, cq = _in_proj(layer, x2, attn_pre, w_in16, w_kv)
        o_a = _attn_a(a0, a1, a2, bias_a)
        o_b = _attn_b(bq, bias_b, sinks[layer])
        o_c = _attn_c(cq)
        x2 = _merge(layer, x2, o_a, o_b, o_c, attn_pre, w_in16, gate_b, w_a, w_b, w_c, w_o, attn_post)
        x2 = _ffn(layer, x2, ffn_pre, w_up16, conv_w, conv_bias, w_down16, ffn_post)
    return x2.reshape(BATCH, SEQ, D_MODEL)
```
